```python
import math
import jax, jax.numpy as jnp
from jax import lax
import numpy as np

D_MODEL = 2048
BATCH = 4
SEQ = 4096
DEPTH = 1

N_META = 16
MIX_WIDTH = D_MODEL
RET_WIDTH = MIX_WIDTH // 2
DIFF_WIDTH = MIX_WIDTH - RET_WIDTH
RET_HEADS = 4
RET_V_DIM = RET_WIDTH // RET_HEADS
RET_QK_DIM = RET_V_DIM // 2
DIFF_HEADS = 4
DIFF_V_DIM = DIFF_WIDTH // DIFF_HEADS
DIFF_QK_DIM = DIFF_V_DIM // 2
CHUNK = 128
Q_BLOCK = 128
N_EXPERTS = 32
TOP_K = 4
D_FF = D_MODEL
SWIGLU_ALPHA = 1.702
SWIGLU_LIMIT = 7.0
EXPERT_BLOCK = 512
NORM_EPS = 1e-6
RET_DECAY_BASE = 5.0
IN_SIZES = (RET_HEADS * RET_QK_DIM, RET_HEADS * RET_QK_DIM, RET_WIDTH, RET_WIDTH,
            DIFF_HEADS * 2 * DIFF_QK_DIM, DIFF_HEADS * 2 * DIFF_QK_DIM, DIFF_WIDTH)
IN_COLS = sum(IN_SIZES)

kernel_name = "hymba_retention_diffattn_moe_encoder"


def _rms(x, gain):
    xf = x.astype(jnp.float32)
    y = xf * lax.rsqrt(jnp.mean(xf * xf, axis=-1, keepdims=True) + NORM_EPS)
    return (y * gain.astype(jnp.float32)).astype(x.dtype)


def _alibi_slopes(n_heads):
    return jnp.exp2(-8.0 * jnp.arange(1, n_heads + 1, dtype=jnp.float32) / n_heads)


def _retention_one_way(q, k, v, log_gamma, strict):
    B, H, Lp, dk = q.shape
    dv = v.shape[-1]
    N = Lp // CHUNK
    qc = q.reshape(B, H, N, CHUNK, dk)
    kc = k.reshape(B, H, N, CHUNK, dk)
    vc = v.reshape(B, H, N, CHUNK, dv)
    pos = jnp.arange(CHUNK, dtype=jnp.float32)
    lg = log_gamma[:, None]
    rel = pos[:, None] - pos[None, :]
    visible = (rel > 0) if strict else (rel >= 0)
    decay_in = jnp.where(visible[None], jnp.exp(lg[:, :, None] * jnp.maximum(rel, 0.0)[None]), 0.0)
    scores = jnp.einsum('bhnid,bhnjd->bhnij', qc, kc) * decay_in[None, :, None]
    inner = jnp.einsum('bhnij,bhnje->bhnie', scores, vc)
    k_dec = kc * jnp.exp(lg * (CHUNK - 1 - pos))[None, :, None, :, None]
    kv = jnp.einsum('bhncd,bhnce->bhnde', k_dec, vc)
    chunk_decay = jnp.exp(lg * CHUNK)[None, :, :, None]

    def step(state, kv_n):
        return state * chunk_decay + kv_n, state

    s0 = jnp.zeros((B, H, dk, dv), kv.dtype)
    _, s_prev = lax.scan(step, s0, jnp.moveaxis(kv, 2, 0))
    s_prev = jnp.moveaxis(s_prev, 0, 2)
    q_dec = qc * jnp.exp(lg * (pos + 1.0))[None, :, None, :, None]
    cross = jnp.einsum('bhncd,bhnde->bhnce', q_dec, s_prev)
    return (inner + cross).reshape(B, H, Lp, dv)


def _bidir_retention(q, k, v, p_fwd, p_bwd):
    L = q.shape[2]
    pad = (-L) % CHUNK
    padw = ((0, 0), (0, 0), (pad, 0), (0, 0))
    q, k, v = (jnp.pad(t, padw) for t in (q, k, v))
    lg_f = jnp.log1p(-jnp.exp2(-p_fwd.astype(jnp.float32)))
    lg_b = jnp.log1p(-jnp.exp2(-p_bwd.astype(jnp.float32)))
    fwd = _retention_one_way(q, k, v, lg_f, strict=False)
    flip = lambda t: jnp.flip(t, axis=2)
    bwd = flip(_retention_one_way(flip(q), flip(k), flip(v), lg_b, strict=True))
    return (fwd + bwd)[:, :, pad:]


def _diff_attention(q, k, v, lam):
    B, L, H, _, d = q.shape
    pad = (-L) % Q_BLOCK
    nb = (L + pad) // Q_BLOCK
    qp = jnp.pad(q, ((0, 0), (pad, 0), (0, 0), (0, 0), (0, 0)))
    qb = qp.reshape(B, nb, Q_BLOCK, H, 2, d).transpose(1, 0, 2, 3, 4, 5)
    q_start = jnp.arange(nb) * Q_BLOCK - pad
    pos_k = jnp.arange(L)
    key_real = pos_k >= N_META
    slopes = _alibi_slopes(H)

    def one_block(args):
        qblk, start = args
        pos_q = start + jnp.arange(Q_BLOCK)
        dist = jnp.abs(pos_q[:, None] - pos_k[None, :]).astype(jnp.float32)
        both_real = (pos_q >= N_META)[:, None] & key_real[None, :]
        bias = -slopes[:, None, None] * jnp.where(both_real, dist, 0.0)[None]
        s = jnp.einsum('bqhmd,bkhmd->bhmqk', qblk, k).astype(jnp.float32) + bias[None, :, None]
        p = jax.nn.softmax(s, axis=-1)
        a = p[:, :, 0] - lam * p[:, :, 1]
        return jnp.einsum('bhqk,bkhe->bqhe', a.astype(v.dtype), v)

    out = lax.map(one_block, (qb, q_start))
    out = out.transpose(1, 0, 2, 3, 4).reshape(B, nb * Q_BLOCK, H, v.shape[-1])
    return out[:, pad:]


def _mixer(h, norm_gain, w_in, decay_f, decay_b, ret_gain, q_gain, k_gain,
           lq1, lk1, lq2, lk2, diff_gain, w_out, layer_idx):
    B, L, _ = h.shape
    u = _rms(h, norm_gain)
    z = u @ w_in
    split_at = [int(i) for i in np.cumsum(IN_SIZES)[:-1]]
    rq, rk, rv, rg, dq, dk, dv = jnp.split(z, split_at, axis=-1)
    rq = rq.reshape(B, L, RET_HEADS, RET_QK_DIM).transpose(0, 2, 1, 3)
    rk = rk.reshape(B, L, RET_HEADS, RET_QK_DIM).transpose(0, 2, 1, 3) * (RET_QK_DIM ** -0.5)
    rv = rv.reshape(B, L, RET_HEADS, RET_V_DIM).transpose(0, 2, 1, 3)
    ret = _bidir_retention(rq, rk, rv, decay_f, decay_b).transpose(0, 2, 1, 3)
    ret = _rms(ret, ret_gain.reshape(RET_HEADS, RET_V_DIM)).reshape(B, L, RET_WIDTH)
    ret = ret * jax.nn.silu(rg)
    lam_init = 0.8 - 0.6 * math.exp(-0.3 * layer_idx)
    lam = (jnp.exp(jnp.sum(lq1.astype(jnp.float32) * lk1.astype(jnp.float32)))
           - jnp.exp(jnp.sum(lq2.astype(jnp.float32) * lk2.astype(jnp.float32))) + lam_init)
    dq = _rms(dq.reshape(B, L, DIFF_HEADS, 2, DIFF_QK_DIM), q_gain) * (DIFF_QK_DIM ** -0.5)
    dk = _rms(dk.reshape(B, L, DIFF_HEADS, 2, DIFF_QK_DIM), k_gain)
    dv = dv.reshape(B, L, DIFF_HEADS, DIFF_V_DIM)
    att = _diff_attention(dq, dk, dv, lam)
    att = (_rms(att, diff_gain.reshape(DIFF_HEADS, DIFF_V_DIM)) * (1.0 - lam_init)).reshape(B, L, DIFF_WIDTH)
    y = jnp.concatenate([ret.astype(h.dtype), att.astype(h.dtype)], axis=-1) @ w_out
    return y.astype(h.dtype)


def _moe(h, norm_gain, w_router, b_router, w_gate_up, b_gate_up, w_down, b_down):
    B, L, D = h.shape
    T = B * L
    xt = _rms(h, norm_gain).reshape(T, D)
    logits = (xt @ w_router + b_router).astype(jnp.float32)
    top_val, top_idx = lax.top_k(logits, TOP_K)
    gates = jax.nn.softmax(top_val, axis=-1)
    P = T * TOP_K
    flat_e = top_idx.reshape(P)
    flat_tok = jnp.arange(P) // TOP_K
    flat_g = gates.reshape(P)
    order = jnp.argsort(flat_e)
    se, stok, sg = flat_e[order], flat_tok[order], flat_g[order]
    counts = jnp.bincount(flat_e, length=N_EXPERTS)
    starts = jnp.cumsum(counts) - counts
    padded = (counts + EXPERT_BLOCK - 1) // EXPERT_BLOCK * EXPERT_BLOCK
    pend = jnp.cumsum(padded)
    pstarts = pend - padded
    dest = pstarts[se] + (jnp.arange(P) - starts[se])
    n_blocks = -(-P // EXPERT_BLOCK) + N_EXPERTS
    cap = n_blocks * EXPERT_BLOCK
    buf_tok = jnp.full((cap,), T, jnp.int32).at[dest].set(stok.astype(jnp.int32))
    buf_g = jnp.zeros((cap,), jnp.float32).at[dest].set(sg)
    block_e = jnp.minimum(jnp.searchsorted(pend, jnp.arange(n_blocks) * EXPERT_BLOCK, side='right'),
                          N_EXPERTS - 1)
    x_pad = jnp.concatenate([xt, jnp.zeros((1, D), xt.dtype)], axis=0)
    xb = x_pad[buf_tok].reshape(n_blocks, EXPERT_BLOCK, D)

    def expert_block(args):
        xblk, e = args
        gu = xblk @ w_gate_up[e] + b_gate_up[e]
        gate = jnp.minimum(gu[:, :D_FF], SWIGLU_LIMIT)
        up = jnp.clip(gu[:, D_FF:], -SWIGLU_LIMIT, SWIGLU_LIMIT)
        glu = gate * jax.nn.sigmoid(SWIGLU_ALPHA * gate)
        return ((up + 1.0) * glu) @ w_down[e] + b_down[e]

    out = lax.map(expert_block, (xb, block_e)).reshape(cap, D)
    y = jax.ops.segment_sum(out * buf_g[:, None].astype(out.dtype), buf_tok, num_segments=T + 1)[:T]
    return y.reshape(B, L, D).astype(h.dtype)


def setup_inputs(seed: int = 0) -> dict:
    key = jax.random.key(seed)
    ks = jax.random.split(key, 24)
    f32 = jnp.float32
    nrm = lambda k, shape, scale: scale * jax.random.normal(k, shape, f32)
    Dp = DEPTH
    heads_p = jnp.arange(RET_HEADS, dtype=f32)
    return {
        "x": nrm(ks[0], (BATCH, SEQ, D_MODEL), 1.0),
        "meta_tokens": nrm(ks[1], (N_META, D_MODEL), 1.0),
        "attn_norm_gain": 1.0 + nrm(ks[2], (Dp, D_MODEL), 0.02),
        "w_in": nrm(ks[3], (Dp, D_MODEL, IN_COLS), D_MODEL ** -0.5),
        "ret_decay_fwd": RET_DECAY_BASE + heads_p[None] + nrm(ks[4], (Dp, RET_HEADS), 0.1),
        "ret_decay_bwd": RET_DECAY_BASE + heads_p[None] + nrm(ks[5], (Dp, RET_HEADS), 0.1),
        "ret_out_gain": 1.0 + nrm(ks[6], (Dp, RET_WIDTH), 0.02),
        "diff_q_gain": 1.0 + nrm(ks[7], (Dp, DIFF_QK_DIM), 0.02),
        "diff_k_gain": 1.0 + nrm(ks[8], (Dp, DIFF_QK_DIM), 0.02),
        "diff_lambda_q1": nrm(ks[9], (Dp, DIFF_QK_DIM), 0.1),
        "diff_lambda_k1": nrm(ks[10], (Dp, DIFF_QK_DIM), 0.1),
        "diff_lambda_q2": nrm(ks[11], (Dp, DIFF_QK_DIM), 0.1),
        "diff_lambda_k2": nrm(ks[12], (Dp, DIFF_QK_DIM), 0.1),
        "diff_out_gain": 1.0 + nrm(ks[13], (Dp, DIFF_WIDTH), 0.02),
        "w_out": nrm(ks[14], (Dp, MIX_WIDTH, D_MODEL), MIX_WIDTH ** -0.5),
        "ffn_norm_gain": 1.0 + nrm(ks[15], (Dp, D_MODEL), 0.02),
        "w_router": nrm(ks[16], (Dp, D_MODEL, N_EXPERTS), D_MODEL ** -0.5),
        "b_router": nrm(ks[17], (Dp, N_EXPERTS), 0.01),
        "w_gate_up": nrm(ks[18], (Dp, N_EXPERTS, D_MODEL, 2 * D_FF), D_MODEL ** -0.5),
        "b_gate_up": nrm(ks[19], (Dp, N_EXPERTS, 2 * D_FF), 0.01),
        "w_down": nrm(ks[20], (Dp, N_EXPERTS, D_FF, D_MODEL), D_FF ** -0.5),
        "b_down": nrm(ks[21], (Dp, N_EXPERTS, D_MODEL), 0.01),
    }


def reference(x, meta_tokens, attn_norm_gain, w_in, ret_decay_fwd, ret_decay_bwd, ret_out_gain,
              diff_q_gain, diff_k_gain, diff_lambda_q1, diff_lambda_k1, diff_lambda_q2, diff_lambda_k2,
              diff_out_gain, w_out, ffn_norm_gain, w_router, b_router, w_gate_up, b_gate_up,
              w_down, b_down):
    B, S, D = x.shape
    meta = jnp.broadcast_to(meta_tokens.astype(x.dtype)[None], (B, N_META, D))
    h = jnp.concatenate([meta, x], axis=1)
    for l in range(DEPTH):
        h = h + _mixer(h, attn_norm_gain[l], w_in[l], ret_decay_fwd[l], ret_decay_bwd[l], ret_out_gain[l],
                       diff_q_gain[l], diff_k_gain[l], diff_lambda_q1[l], diff_lambda_k1[l],
                       diff_lambda_q2[l], diff_lambda_k2[l], diff_out_gain[l], w_out[l], l)
        h = h + _moe(h, ffn_norm_gain[l], w_router[l], b_router[l], w_gate_up[l], b_gate_up[l],
                     w_down[l], b_down[l])
    return h[:, N_META:]
```

```python
import functools
import math

import jax
import jax.numpy as jnp
from jax import lax
from jax.experimental import pallas as pl
from jax.experimental.pallas import tpu as pltpu

N_META = 16
CHUNK = 128
RET_HEADS = 4
RET_QK_DIM = 128
RET_V_DIM = 256
DIFF_HEADS = 4
DIFF_QK_DIM = 128
DIFF_V_DIM = 256
TOP_K = 4
SWIGLU_ALPHA = 1.702
SWIGLU_LIMIT = 7.0
NORM_EPS = 1e-6
RET_DECAY_BASE = 5.0
LAMBDA_INIT = 0.8 - 0.6 * math.exp(-0.3 * 0)
NEG_BIG = -1e30

V7X_VMEM_BYTES = 64 * 1024 * 1024
VMEM_LIMIT_BYTES = V7X_VMEM_BYTES - 8 * 1024 * 1024

F32 = jnp.float32
BF16 = jnp.bfloat16


def _compiler_params(semantics):
    return pltpu.CompilerParams(dimension_semantics=semantics, vmem_limit_bytes=VMEM_LIMIT_BYTES)


def _prep_kernel(x_ref, meta_ref, g_ref, hp_ref, u_ref, *, n_pad):
    c = pl.program_id(1)

    def emit(h):
        hp_ref[0] = h
        ms = jnp.mean(h * h, axis=-1, keepdims=True)
        u_ref[0] = (h * lax.rsqrt(ms + NORM_EPS) * g_ref[...]).astype(u_ref.dtype)

    @pl.when(c == 0)
    def _():
        d = meta_ref.shape[-1]
        emit(jnp.concatenate([jnp.zeros((n_pad, d), F32), meta_ref[...]], axis=0))

    @pl.when(c > 0)
    def _():
        emit(x_ref[0])


def _prep(x, meta, gain):
    b, s, d = x.shape
    n_pad = CHUNK - N_META
    nc = s // CHUNK + 1
    lp = nc * CHUNK
    return pl.pallas_call(
        functools.partial(_prep_kernel, n_pad=n_pad),
        grid=(b, nc),
        in_specs=[
            pl.BlockSpec((1, CHUNK, d), lambda i, c: (i, jnp.maximum(c - 1, 0), 0)),
            pl.BlockSpec((N_META, d), lambda i, c: (0, 0)),
            pl.BlockSpec((1, d), lambda i, c: (0, 0)),
        ],
        out_specs=[
            pl.BlockSpec((1, CHUNK, d), lambda i, c: (i, c, 0)),
            pl.BlockSpec((1, CHUNK, d), lambda i, c: (i, c, 0)),
        ],
        out_shape=[
            jax.ShapeDtypeStruct((b, lp, d), F32),
            jax.ShapeDtypeStruct((b, lp, d), BF16),
        ],
        compiler_params=_compiler_params(("parallel", "arbitrary")),
        name="prep",
    )(x, meta, gain.reshape(1, d))


def _matmul_kernel(a_ref, b_ref, o_ref):
    o_ref[...] = jnp.dot(a_ref[...], b_ref[...], preferred_element_type=F32).astype(o_ref.dtype)


def _in_proj(u2d, w_bf16, *, tm, tn):
    m, k = u2d.shape
    n = w_bf16.shape[1]
    return pl.pallas_call(
        _matmul_kernel,
        grid=(m // tm, n // tn),
        in_specs=[
            pl.BlockSpec((tm, k), lambda i, j: (i, 0)),
            pl.BlockSpec((k, tn), lambda i, j: (0, j)),
        ],
        out_specs=pl.BlockSpec((tm, tn), lambda i, j: (i, j)),
        out_shape=jax.ShapeDtypeStruct((m, n), BF16),
        compiler_params=_compiler_params(("parallel", "arbitrary")),
        name="in_proj",
    )(u2d, w_bf16)


def _retention_kernel(lgf_ref, lgb_ref, q_ref, k_ref, v_ref, g_ref, gain_ref, o_ref,
                      ob_ref, sf_ref, sb_ref, *, nc):
    h = pl.program_id(1)
    lgf = lgf_ref[h]
    lgb = lgb_ref[h]
    c = CHUNK
    scale = RET_QK_DIM ** -0.5
    row = lax.broadcasted_iota(jnp.int32, (c, c), 0)
    col = lax.broadcasted_iota(jnp.int32, (c, c), 1)
    rel = (row - col).astype(F32)
    decay = jnp.where(rel >= 0, jnp.exp(lgf * jnp.maximum(rel, 0.0)),
                      jnp.exp(lgb * jnp.maximum(-rel, 0.0))) * scale
    pos = lax.broadcasted_iota(jnp.int32, (c, 1), 0).astype(F32)
    q_dec_f = jnp.exp(lgf * (pos + 1.0))
    k_dec_f = jnp.exp(lgf * (c - 1.0 - pos)) * scale
    q_dec_b = jnp.exp(lgb * (c - pos))
    k_dec_b = jnp.exp(lgb * pos) * scale
    chunk_f = jnp.exp(lgf * c)
    chunk_b = jnp.exp(lgb * c)

    def rows(n):
        return pl.ds(pl.multiple_of(n * c, c), c)

    def state_update(s_ref, kn, vn, k_dec, chunk_decay):
        kd_t = (kn.astype(F32) * k_dec).T.astype(BF16)
        s_ref[...] = s_ref[...] * chunk_decay + jnp.dot(kd_t, vn, preferred_element_type=F32)

    sb_ref[...] = jnp.zeros_like(sb_ref)
    sf_ref[...] = jnp.zeros_like(sf_ref)

    def bwd_body(t, carry):
        n = nc - 1 - t
        qn = q_ref[0, rows(n), :]
        kn = k_ref[0, rows(n), :]
        vn = v_ref[0, rows(n), :]
        qd = (qn.astype(F32) * q_dec_b).astype(BF16)
        ob_ref[rows(n), :] = jnp.dot(qd, sb_ref[...].astype(BF16), preferred_element_type=F32)
        state_update(sb_ref, kn, vn, k_dec_b, chunk_b)
        return carry

    lax.fori_loop(0, nc, bwd_body, 0)

    gain = gain_ref[...]

    def fwd_body(n, carry):
        qn = q_ref[0, rows(n), :]
        kn = k_ref[0, rows(n), :]
        vn = v_ref[0, rows(n), :]
        sc = lax.dot_general(qn, kn, (((1,), (1,)), ((), ())), preferred_element_type=F32)
        inner = jnp.dot((sc * decay).astype(BF16), vn, preferred_element_type=F32)
        qd = (qn.astype(F32) * q_dec_f).astype(BF16)
        cross = jnp.dot(qd, sf_ref[...].astype(BF16), preferred_element_type=F32)
        o = inner + cross + ob_ref[rows(n), :]
        state_update(sf_ref, kn, vn, k_dec_f, chunk_f)
        o = o * lax.rsqrt(jnp.mean(o * o, axis=-1, keepdims=True) + NORM_EPS) * gain
        gate = g_ref[0, rows(n), :].astype(F32)
        o_ref[0, rows(n), :] = (o * (gate * jax.nn.sigmoid(gate))).astype(o_ref.dtype)
        return carry

    lax.fori_loop(0, nc, fwd_body, 0)


def _retention(z, lg_f, lg_b, ret_gain):
    b, lp, _ = z.shape
    nc = lp // CHUNK
    dk, dv, nh = RET_QK_DIM, RET_V_DIM, RET_HEADS
    k_blk0 = nh * dk // dk
    v_blk0 = 2 * nh * dk // dv
    g_blk0 = v_blk0 + nh
    smem = pl.BlockSpec(memory_space=pltpu.SMEM)
    return pl.pallas_call(
        functools.partial(_retention_kernel, nc=nc),
        grid=(b, nh),
        in_specs=[
            smem, smem,
            pl.BlockSpec((1, lp, dk), lambda i, h: (i, 0, h)),
            pl.BlockSpec((1, lp, dk), lambda i, h: (i, 0, k_blk0 + h)),
            pl.BlockSpec((1, lp, dv), lambda i, h: (i, 0, v_blk0 + h)),
            pl.BlockSpec((1, lp, dv), lambda i, h: (i, 0, g_blk0 + h)),
            pl.BlockSpec((1, dv), lambda i, h: (0, h)),
        ],
        out_specs=pl.BlockSpec((1, lp, dv), lambda i, h: (i, 0, h)),
        out_shape=jax.ShapeDtypeStruct((b, lp, nh * dv), BF16),
        scratch_shapes=[
            pltpu.VMEM((lp, dv), F32),
            pltpu.VMEM((dk, dv), F32),
            pltpu.VMEM((dk, dv), F32),
        ],
        compiler_params=_compiler_params(("parallel", "arbitrary")),
        name="retention",
    )(lg_f, lg_b, z, z, z, z, ret_gain.reshape(1, nh * dv))


def _attn_kernel(lam_ref, slope_ref, q_ref, k_ref, v_ref, qg_ref, kg_ref, og_ref, o_ref,
                 kn_ref, m_ref, l_ref, acc_ref, *, tq, tk, n_pad):
    h = pl.program_id(1)
    qi = pl.program_id(2)
    lp = k_ref.shape[1]
    d = DIFF_QK_DIM
    lam = lam_ref[0]
    slope = slope_ref[h]

    def qk_norm(x, gain):
        return x * lax.rsqrt(jnp.mean(x * x, axis=-1, keepdims=True) + NORM_EPS) * gain

    @pl.when(qi == 0)
    def _():
        def body(c, carry):
            r = pl.ds(pl.multiple_of(c * tk, tk), tk)
            kk = k_ref[0, r, :].astype(F32)
            for m in range(2):
                kn_ref[r, m * d:(m + 1) * d] = qk_norm(kk[:, m * d:(m + 1) * d], kg_ref[...]).astype(BF16)
            return carry
        lax.fori_loop(0, lp // tk, body, 0)

    qq = q_ref[0].astype(F32)
    qs = [(qk_norm(qq[:, m * d:(m + 1) * d], qg_ref[...]) * (d ** -0.5)).astype(BF16) for m in range(2)]

    m_ref[...] = jnp.full_like(m_ref, NEG_BIG)
    l_ref[...] = jnp.zeros_like(l_ref)
    acc_ref[...] = jnp.zeros_like(acc_ref)

    q_pos = qi * tq + lax.broadcasted_iota(jnp.int32, (tq, tk), 0)
    k_off = lax.broadcasted_iota(jnp.int32, (tq, tk), 1)

    def body(c, carry):
        r = pl.ds(pl.multiple_of(c * tk, tk), tk)
        kc = kn_ref[r, :]
        vc = v_ref[0, r, :]
        k_pos = c * tk + k_off
        dist = jnp.abs(q_pos - k_pos).astype(F32)
        both_real = jnp.logical_and(q_pos >= CHUNK, k_pos >= CHUNK)
        bias = jnp.where(both_real, -slope * dist, 0.0)
        valid = k_pos >= n_pad
        for m in range(2):
            s = lax.dot_general(qs[m], kc[:, m * d:(m + 1) * d], (((1,), (1,)), ((), ())),
                                preferred_element_type=F32)
            s = jnp.where(valid, s + bias, NEG_BIG)
            m_old = m_ref[m]
            m_new = jnp.maximum(m_old, jnp.max(s, axis=-1, keepdims=True))
            alpha = jnp.exp(m_old - m_new)
            e = jnp.exp(s - m_new)
            l_ref[m] = alpha * l_ref[m] + jnp.sum(e, axis=-1, keepdims=True)
            acc_ref[m] = alpha * acc_ref[m] + jnp.dot(e.astype(BF16), vc, preferred_element_type=F32)
            m_ref[m] = m_new
        return carry

    lax.fori_loop(0, lp // tk, body, 0)

    o = acc_ref[0] / l_ref[0] - lam * (acc_ref[1] / l_ref[1])
    o = o * lax.rsqrt(jnp.mean(o * o, axis=-1, keepdims=True) + NORM_EPS) * og_ref[...]
    o_ref[0] = (o * (1.0 - LAMBDA_INIT)).astype(o_ref.dtype)


def _attention(z, lam, slopes, q_gain, k_gain, out_gain, *, tq, tk):
    b, lp, _ = z.shape
    nh, d, dv = DIFF_HEADS, DIFF_QK_DIM, DIFF_V_DIM
    ret_cols = 2 * RET_HEADS * RET_QK_DIM + 2 * RET_HEADS * RET_V_DIM
    q_blk0 = ret_cols // (2 * d)
    k_blk0 = q_blk0 + nh
    v_blk0 = k_blk0 + nh
    smem = pl.BlockSpec(memory_space=pltpu.SMEM)
    return pl.pallas_call(
        functools.partial(_attn_kernel, tq=tq, tk=tk, n_pad=CHUNK - N_META),
        grid=(b, nh, lp // tq),
        in_specs=[
            smem, smem,
            pl.BlockSpec((1, tq, 2 * d), lambda i, h, q: (i, q, q_blk0 + h)),
            pl.BlockSpec((1, lp, 2 * d), lambda i, h, q: (i, 0, k_blk0 + h)),
            pl.BlockSpec((1, lp, dv), lambda i, h, q: (i, 0, v_blk0 + h)),
            pl.BlockSpec((1, d), lambda i, h, q: (0, 0)),
            pl.BlockSpec((1, d), lambda i, h, q: (0, 0)),
            pl.BlockSpec((1, dv), lambda i, h, q: (0, h)),
        ],
        out_specs=pl.BlockSpec((1, tq, dv), lambda i, h, q: (i, q, h)),
        out_shape=jax.ShapeDtypeStruct((b, lp, nh * dv), BF16),
        scratch_shapes=[
            pltpu.VMEM((lp, 2 * d), BF16),
            pltpu.VMEM((2, tq, 1), F32),
            pltpu.VMEM((2, tq, 1), F32),
            pltpu.VMEM((2, tq, dv), F32),
        ],
        compiler_params=_compiler_params(("parallel", "parallel", "arbitrary")),
        name="attention",
    )(lam, slopes, z, z, z, q_gain.reshape(1, d), k_gain.reshape(1, d), out_gain.reshape(1, nh * dv))


def _pack_bf16_pairs(x):
    n = x.shape[1] // 2
    xb = x.astype(BF16).astype(F32)
    lo = lax.bitcast_convert_type(xb[:, :n], jnp.uint32)
    hi = lax.bitcast_convert_type(xb[:, n:], jnp.uint32)
    return (hi & jnp.uint32(0xFFFF0000)) | (lo >> 16)


def _unpack_bf16_pairs(p):
    lo = lax.bitcast_convert_type(p << 16, F32).astype(BF16)
    hi = lax.bitcast_convert_type(p & jnp.uint32(0xFFFF0000), F32).astype(BF16)
    return lo, hi


def _out_proj_kernel(ret_ref, att_ref, hp_ref, w_ref, g_ref, wr_ref, br_ref,
                     h1_ref, xt_ref, route_ref):
    kr = ret_ref.shape[1]
    y = jnp.dot(ret_ref[...], w_ref[:kr, :], preferred_element_type=F32)
    y = y + jnp.dot(att_ref[...], w_ref[kr:, :], preferred_element_type=F32)
    h1 = hp_ref[...] + y
    h1_ref[...] = h1
    xt = h1 * lax.rsqrt(jnp.mean(h1 * h1, axis=-1, keepdims=True) + NORM_EPS) * g_ref[...]
    xt_ref[...] = _pack_bf16_pairs(xt)
    logits = jnp.dot(xt, wr_ref[...], preferred_element_type=F32,
                     precision=lax.Precision.HIGHEST) + br_ref[...]
    tm, ne = logits.shape
    lane = lax.broadcasted_iota(jnp.int32, (tm, ne), 1).astype(F32)
    out_lane = lax.broadcasted_iota(jnp.int32, route_ref.shape, 1)
    vals, idxs = [], []
    work = logits
    for _ in range(TOP_K):
        top = jnp.max(work, axis=-1, keepdims=True)
        idx = jnp.min(jnp.where(work == top, lane, float(ne)), axis=-1, keepdims=True)
        work = jnp.where(lane == idx, -jnp.inf, work)
        vals.append(top)
        idxs.append(idx)
    exps = [jnp.exp(v - vals[0]) for v in vals]
    denom = exps[0] + exps[1] + exps[2] + exps[3]
    slab = jnp.zeros(route_ref.shape, F32)
    for k in range(TOP_K):
        slab = jnp.where(out_lane == k, idxs[k], slab)
        slab = jnp.where(out_lane == TOP_K + k, exps[k] / denom, slab)
    route_ref[...] = slab


def _out_proj(ret2d, att2d, hp2d, w_bf16, gain, w_router, b_router, *, tm):
    m, kr = ret2d.shape
    d = hp2d.shape[1]
    ne = w_router.shape[1]
    row = lambda i: (i, 0)
    const = lambda i: (0, 0)
    return pl.pallas_call(
        _out_proj_kernel,
        grid=(m // tm,),
        in_specs=[
            pl.BlockSpec((tm, kr), row),
            pl.BlockSpec((tm, kr), row),
            pl.BlockSpec((tm, d), row),
            pl.BlockSpec((2 * kr, d), const),
            pl.BlockSpec((1, d), const),
            pl.BlockSpec((d, ne), const),
            pl.BlockSpec((1, ne), const),
        ],
        out_specs=[
            pl.BlockSpec((tm, d), row),
            pl.BlockSpec((tm, d // 2), row),
            pl.BlockSpec((tm, 128), row),
        ],
        out_shape=[
            jax.ShapeDtypeStruct((m, d), F32),
            jax.ShapeDtypeStruct((m, d // 2), jnp.uint32),
            jax.ShapeDtypeStruct((m, 128), F32),
        ],
        compiler_params=_compiler_params(("parallel",)),
        name="out_proj",
    )(ret2d, att2d, hp2d, w_bf16, gain.reshape(1, d), w_router, b_router.reshape(1, ne))


def _gather_rows_kernel(src_ref, x_hbm, o_ref, sem, *, rows):
    base = pl.program_id(0) * rows

    def copy(r):
        return pltpu.make_async_copy(x_hbm.at[src_ref[base + r]], o_ref.at[r], sem)

    def start(r, carry):
        copy(r).start()
        return carry

    def wait(r, carry):
        copy(r).wait()
        return carry

    lax.fori_loop(0, rows, start, 0)
    lax.fori_loop(0, rows, wait, 0)


def _gather_rows(src, x2d, *, rows):
    n = src.shape[0]
    d = x2d.shape[1]
    return pl.pallas_call(
        functools.partial(_gather_rows_kernel, rows=rows),
        grid_spec=pltpu.PrefetchScalarGridSpec(
            num_scalar_prefetch=1,
            grid=(n // rows,),
            in_specs=[pl.BlockSpec(memory_space=pl.ANY)],
            out_specs=pl.BlockSpec((rows, d), lambda i, src: (i, 0)),
            scratch_shapes=[pltpu.SemaphoreType.DMA],
        ),
        out_shape=jax.ShapeDtypeStruct((n, d), x2d.dtype),
        compiler_params=_compiler_params(("arbitrary",)),
        name="dispatch",
    )(src, x2d)


def _expert_kernel(te_ref, nt_ref, x_ref, wg_ref, wu_ref, bg_ref, bu_ref, wd_ref, bd_ref, o_ref, xb_ref):
    i = pl.program_id(0)
    j = pl.program_id(1)
    half = x_ref.shape[1]

    @pl.when(jnp.logical_and(i < nt_ref[0], j == 0))
    def _():
        lo, hi = _unpack_bf16_pairs(x_ref[...])
        xb_ref[:, :half] = lo
        xb_ref[:, half:] = hi

    @pl.when(i < nt_ref[0])
    def _():
        x = xb_ref[...]
        gate = jnp.dot(x, wg_ref[0].astype(BF16), preferred_element_type=F32) + bg_ref[0]
        up = jnp.dot(x, wu_ref[0].astype(BF16), preferred_element_type=F32) + bu_ref[0]
        gate = jnp.minimum(gate, SWIGLU_LIMIT)
        up = jnp.clip(up, -SWIGLU_LIMIT, SWIGLU_LIMIT)
        glu = gate * jax.nn.sigmoid(SWIGLU_ALPHA * gate)
        act = ((up + 1.0) * glu).astype(BF16)
        y = jnp.dot(act, wd_ref[0].astype(BF16), preferred_element_type=F32)

        @pl.when(j == 0)
        def _():
            o_ref[...] = y + bd_ref[0]

        @pl.when(j > 0)
        def _():
            o_ref[...] += y

    @pl.when(jnp.logical_and(i >= nt_ref[0], j == 0))
    def _():
        o_ref[...] = jnp.zeros_like(o_ref)


def _experts(tile_expert, n_used, xs, w_gate_up, b_gate_up, w_down, b_down, *, tm, tf):
    cap, half = xs.shape
    ne, d, f2 = w_gate_up.shape
    assert d == 2 * half
    f = f2 // 2
    nf = f // tf
    n_tiles = cap // tm

    def fchunk(i, j, nt):
        return jnp.where(i < nt[0], j, nf - 1)

    b_gu = b_gate_up.reshape(ne, 1, f2)
    b_d = b_down.reshape(ne, 1, d)
    return pl.pallas_call(
        _expert_kernel,
        grid_spec=pltpu.PrefetchScalarGridSpec(
            num_scalar_prefetch=2,
            grid=(n_tiles, nf),
            in_specs=[
                pl.BlockSpec((tm, half), lambda i, j, te, nt: (i, 0)),
                pl.BlockSpec((1, d, tf), lambda i, j, te, nt: (te[i], 0, fchunk(i, j, nt))),
                pl.BlockSpec((1, d, tf), lambda i, j, te, nt: (te[i], 0, nf + fchunk(i, j, nt))),
                pl.BlockSpec((1, 1, tf), lambda i, j, te, nt: (te[i], 0, fchunk(i, j, nt))),
                pl.BlockSpec((1, 1, tf), lambda i, j, te, nt: (te[i], 0, nf + fchunk(i, j, nt))),
                pl.BlockSpec((1, tf, d), lambda i, j, te, nt: (te[i], fchunk(i, j, nt), 0)),
                pl.BlockSpec((1, 1, d), lambda i, j, te, nt: (te[i], 0, 0)),
            ],
            out_specs=pl.BlockSpec((tm, d), lambda i, j, te, nt: (i, 0)),
            scratch_shapes=[pltpu.VMEM((tm, d), BF16)],
        ),
        out_shape=jax.ShapeDtypeStruct((cap, d), F32),
        compiler_params=_compiler_params(("arbitrary", "arbitrary")),
        name="experts",
    )(tile_expert, n_used, xs, w_gate_up, w_gate_up, b_gu, b_gu, w_down, b_d)


def _combine_kernel(dest_ref, h1_ref, gate_ref, y_hbm, o_ref, buf, sem, *, tt, tiles_per_batch,
                    tok_per_batch):
    bi = pl.program_id(0)
    ti = pl.program_id(1)
    base = (bi * tok_per_batch + N_META + ti * tt) * TOP_K

    def copy(r, k):
        return pltpu.make_async_copy(y_hbm.at[dest_ref[base + r * TOP_K + k]], buf.at[k, r], sem)

    def start(r, carry):
        for k in range(TOP_K):
            copy(r, k).start()
        return carry

    def wait(r, carry):
        for k in range(TOP_K):
            copy(r, k).wait()
        return carry

    lax.fori_loop(0, tt, start, 0)
    lax.fori_loop(0, tt, wait, 0)
    g = gate_ref[0]
    acc = h1_ref[0]
    moe = buf[0] * g[:, 0:1]
    for k in range(1, TOP_K):
        moe = moe + buf[k] * g[:, k:k + 1]
    o_ref[0] = acc + moe


def _combine(dest_flat, h1, gates, y_sorted, *, tt):
    b, lp, d = h1.shape
    s = lp - CHUNK
    tok_per_batch = s + N_META
    tiles = s // tt
    off = CHUNK // tt
    gates_tok = gates[:, N_META:, :]
    return pl.pallas_call(
        functools.partial(_combine_kernel, tt=tt, tiles_per_batch=tiles, tok_per_batch=tok_per_batch),
        grid_spec=pltpu.PrefetchScalarGridSpec(
            num_scalar_prefetch=1,
            grid=(b, tiles),
            in_specs=[
                pl.BlockSpec((1, tt, d), lambda i, t, dest: (i, off + t, 0)),
                pl.BlockSpec((1, tt, TOP_K), lambda i, t, dest: (i, t, 0)),
                pl.BlockSpec(memory_space=pl.ANY),
            ],
            out_specs=pl.BlockSpec((1, tt, d), lambda i, t, dest: (i, t, 0)),
            scratch_shapes=[pltpu.VMEM((TOP_K, tt, d), F32), pltpu.SemaphoreType.DMA],
        ),
        out_shape=jax.ShapeDtypeStruct((b, s, d), F32),
        compiler_params=_compiler_params(("arbitrary", "arbitrary")),
        name="combine",
    )(dest_flat, h1, gates_tok, y_sorted)


def _routing(top_idx, lp, n_pad, *, n_experts, tm):
    b, l, k = top_idx.shape
    p = b * l * k
    n_tiles = -(-p // tm) + n_experts
    cap = n_tiles * tm
    flat_e = top_idx.reshape(p)
    order = jnp.argsort(flat_e, stable=True).astype(jnp.int32)
    counts = jnp.zeros((n_experts,), jnp.int32).at[flat_e].add(1)
    starts = jnp.cumsum(counts) - counts
    padded = (counts + tm - 1) // tm * tm
    pend = jnp.cumsum(padded)
    pstarts = pend - padded
    tile_expert = jnp.minimum(
        jnp.searchsorted(pend, jnp.arange(n_tiles, dtype=jnp.int32) * tm, side="right"),
        n_experts - 1).astype(jnp.int32)
    n_used = (pend[-1] // tm).astype(jnp.int32).reshape(1)
    slot = jnp.arange(cap, dtype=jnp.int32)
    slot_e = jnp.repeat(tile_expert, tm)
    within = slot - pstarts[slot_e]
    valid = within < counts[slot_e]
    assign = order[jnp.clip(starts[slot_e] + within, 0, p - 1)]
    tok = assign // k
    src_row = (tok // l) * lp + n_pad + tok % l
    src_row = jnp.where(valid, src_row, 0).astype(jnp.int32)
    rank = jnp.zeros((p,), jnp.int32).at[order].set(jnp.arange(p, dtype=jnp.int32) - starts[flat_e[order]])
    dest = (pstarts[flat_e] + rank).astype(jnp.int32)
    return src_row, tile_expert, n_used, dest


def kernel(x, meta_tokens, attn_norm_gain, w_in, ret_decay_fwd, ret_decay_bwd, ret_out_gain,
           diff_q_gain, diff_k_gain, diff_lambda_q1, diff_lambda_k1, diff_lambda_q2, diff_lambda_k2,
           diff_out_gain, w_out, ffn_norm_gain, w_router, b_router, w_gate_up, b_gate_up,
           w_down, b_down):
    b, s, d = x.shape
    assert s % CHUNK == 0 and w_in.shape[0] == 1, "single layer, token count a chunk multiple"
    n_pad = CHUNK - N_META
    lp = s + CHUNK
    l = s + N_META

    hp, u = _prep(x, meta_tokens.astype(x.dtype), attn_norm_gain[0])
    z = _in_proj(u.reshape(b * lp, d), w_in[0].astype(BF16), tm=lp // 4, tn=1024)
    z = z.reshape(b, lp, -1)

    lg_f = jnp.log1p(-jnp.exp2(-ret_decay_fwd[0].astype(F32)))
    lg_b = jnp.log1p(-jnp.exp2(-ret_decay_bwd[0].astype(F32)))
    ret = _retention(z, lg_f, lg_b, ret_out_gain[0])

    lam = (jnp.exp(jnp.sum(diff_lambda_q1[0].astype(F32) * diff_lambda_k1[0].astype(F32)))
           - jnp.exp(jnp.sum(diff_lambda_q2[0].astype(F32) * diff_lambda_k2[0].astype(F32)))
           + LAMBDA_INIT).reshape(1)
    slopes = jnp.exp2(-8.0 * jnp.arange(1, DIFF_HEADS + 1, dtype=F32) / DIFF_HEADS)
    att = _attention(z, lam, slopes, diff_q_gain[0], diff_k_gain[0], diff_out_gain[0], tq=384, tk=384)

    h1, xt, route = _out_proj(ret.reshape(b * lp, -1), att.reshape(b * lp, -1), hp.reshape(b * lp, d),
                              w_out[0].astype(BF16), ffn_norm_gain[0], w_router[0], b_router[0],
                              tm=lp // 8)

    route = route.reshape(b, lp, 128)[:, n_pad:, :]
    top_idx = route[:, :, :TOP_K].astype(jnp.int32)
    gates = route[:, :, TOP_K:2 * TOP_K]

    tm = 1024
    src_row, tile_expert, n_used, dest = _routing(top_idx, lp, n_pad, n_experts=w_router.shape[-1], tm=tm)
    xs = _gather_rows(src_row, xt, rows=256)
    ys = _experts(tile_expert, n_used, xs, w_gate_up[0], b_gate_up[0], w_down[0], b_down[0],
                  tm=tm, tf=256)
    return _combine(dest, h1.reshape(b, lp, d), gates, ys, tt=CHUNK)
```

```python
import functools
import math

import jax
import jax.numpy as jnp
from jax import lax
from jax.experimental import pallas as pl
from jax.experimental.pallas import tpu as pltpu

N_META = 16
CHUNK = 128
RET_HEADS = 4
RET_QK_DIM = 128
RET_V_DIM = 256
DIFF_HEADS = 4
DIFF_QK_DIM = 128
DIFF_V_DIM = 256
TOP_K = 4
SWIGLU_ALPHA = 1.702
SWIGLU_LIMIT = 7.0
NORM_EPS = 1e-6
RET_DECAY_BASE = 5.0
LAMBDA_INIT = 0.8 - 0.6 * math.exp(-0.3 * 0)
NEG_BIG = -1e30

V7X_VMEM_BYTES = 64 * 1024 * 1024
VMEM_LIMIT_BYTES = V7X_VMEM_BYTES - 8 * 1024 * 1024

F32 = jnp.float32
BF16 = jnp.bfloat16


def _compiler_params(semantics):
    return pltpu.CompilerParams(dimension_semantics=semantics, vmem_limit_bytes=VMEM_LIMIT_BYTES)


def _prep_kernel(x_ref, meta_ref, g_ref, hp_ref, u_ref, *, n_pad):
    c = pl.program_id(1)

    def emit(h):
        hp_ref[0] = h
        ms = jnp.mean(h * h, axis=-1, keepdims=True)
        u_ref[0] = (h * lax.rsqrt(ms + NORM_EPS) * g_ref[...]).astype(u_ref.dtype)

    @pl.when(c == 0)
    def _():
        d = meta_ref.shape[-1]
        emit(jnp.concatenate([jnp.zeros((n_pad, d), F32), meta_ref[...]], axis=0))

    @pl.when(c > 0)
    def _():
        emit(x_ref[0])


def _prep(x, meta, gain):
    b, s, d = x.shape
    n_pad = CHUNK - N_META
    nc = s // CHUNK + 1
    lp = nc * CHUNK
    return pl.pallas_call(
        functools.partial(_prep_kernel, n_pad=n_pad),
        grid=(b, nc),
        in_specs=[
            pl.BlockSpec((1, CHUNK, d), lambda i, c: (i, jnp.maximum(c - 1, 0), 0)),
            pl.BlockSpec((N_META, d), lambda i, c: (0, 0)),
            pl.BlockSpec((1, d), lambda i, c: (0, 0)),
        ],
        out_specs=[
            pl.BlockSpec((1, CHUNK, d), lambda i, c: (i, c, 0)),
            pl.BlockSpec((1, CHUNK, d), lambda i, c: (i, c, 0)),
        ],
        out_shape=[
            jax.ShapeDtypeStruct((b, lp, d), F32),
            jax.ShapeDtypeStruct((b, lp, d), BF16),
        ],
        compiler_params=_compiler_params(("parallel", "arbitrary")),
        name="prep",
    )(x, meta, gain.reshape(1, d))


def _matmul_kernel(a_ref, b_ref, o_ref):
    o_ref[...] = jnp.dot(a_ref[...], b_ref[...], preferred_element_type=F32).astype(o_ref.dtype)


def _in_proj(u2d, w_bf16, *, tm, tn):
    m, k = u2d.shape
    n = w_bf16.shape[1]
    return pl.pallas_call(
        _matmul_kernel,
        grid=(m // tm, n // tn),
        in_specs=[
            pl.BlockSpec((tm, k), lambda i, j: (i, 0)),
            pl.BlockSpec((k, tn), lambda i, j: (0, j)),
        ],
        out_specs=pl.BlockSpec((tm, tn), lambda i, j: (i, j)),
        out_shape=jax.ShapeDtypeStruct((m, n), BF16),
        compiler_params=_compiler_params(("parallel", "arbitrary")),
        name="in_proj",
    )(u2d, w_bf16)


def _retention_kernel(lgf_ref, lgb_ref, q_ref, k_ref, v_ref, g_ref, gain_ref, o_ref,
                      ob_ref, sf_ref, sb_ref, *, nc):
    h = pl.program_id(1)
    lgf = lgf_ref[h]
    lgb = lgb_ref[h]
    c = CHUNK
    scale = RET_QK_DIM ** -0.5
    row = lax.broadcasted_iota(jnp.int32, (c, c), 0)
    col = lax.broadcasted_iota(jnp.int32, (c, c), 1)
    rel = (row - col).astype(F32)
    decay = jnp.where(rel >= 0, jnp.exp(lgf * jnp.maximum(rel, 0.0)),
                      jnp.exp(lgb * jnp.maximum(-rel, 0.0))) * scale
    pos = lax.broadcasted_iota(jnp.int32, (c, 1), 0).astype(F32)
    q_dec_f = jnp.exp(lgf * (pos + 1.0))
    k_dec_f = jnp.exp(lgf * (c - 1.0 - pos)) * scale
    q_dec_b = jnp.exp(lgb * (c - pos))
    k_dec_b = jnp.exp(lgb * pos) * scale
    chunk_f = jnp.exp(lgf * c)
    chunk_b = jnp.exp(lgb * c)

    def rows(n):
        return pl.ds(pl.multiple_of(n * c, c), c)

    def state_update(s_ref, kn, vn, k_dec, chunk_decay):
        kd_t = (kn.astype(F32) * k_dec).T.astype(BF16)
        s_ref[...] = s_ref[...] * chunk_decay + jnp.dot(kd_t, vn, preferred_element_type=F32)

    sb_ref[...] = jnp.zeros_like(sb_ref)
    sf_ref[...] = jnp.zeros_like(sf_ref)

    def bwd_body(t, carry):
        n = nc - 1 - t
        qn = q_ref[0, rows(n), :]
        kn = k_ref[0, rows(n), :]
        vn = v_ref[0, rows(n), :]
        qd = (qn.astype(F32) * q_dec_b).astype(BF16)
        ob_ref[rows(n), :] = jnp.dot(qd, sb_ref[...].astype(BF16), preferred_element_type=F32)
        state_update(sb_ref, kn, vn, k_dec_b, chunk_b)
        return carry

    lax.fori_loop(0, nc, bwd_body, 0)

    gain = gain_ref[...]

    def fwd_body(n, carry):
        qn = q_ref[0, rows(n), :]
        kn = k_ref[0, rows(n), :]
        vn = v_ref[0, rows(n), :]
        sc = lax.dot_general(qn, kn, (((1,), (1,)), ((), ())), preferred_element_type=F32)
        inner = jnp.dot((sc * decay).astype(BF16), vn, preferred_element_type=F32)
        qd = (qn.astype(F32) * q_dec_f).astype(BF16)
        cross = jnp.dot(qd, sf_ref[...].astype(BF16), preferred_element_type=F32)
        o = inner + cross + ob_ref[rows(n), :]
        state_update(sf_ref, kn, vn, k_dec_f, chunk_f)
        o = o * lax.rsqrt(jnp.mean(o * o, axis=-1, keepdims=True) + NORM_EPS) * gain
        gate = g_ref[0, rows(n), :].astype(F32)
        o_ref[0, rows(n), :] = (o * (gate * jax.nn.sigmoid(gate))).astype(o_ref.dtype)
        return carry

    lax.fori_loop(0, nc, fwd_body, 0)


def _retention(z, lg_f, lg_b, ret_gain):
    b, lp, _ = z.shape
    nc = lp // CHUNK
    dk, dv, nh = RET_QK_DIM, RET_V_DIM, RET_HEADS
    k_blk0 = nh * dk // dk
    v_blk0 = 2 * nh * dk // dv
    g_blk0 = v_blk0 + nh
    smem = pl.BlockSpec(memory_space=pltpu.SMEM)
    return pl.pallas_call(
        functools.partial(_retention_kernel, nc=nc),
        grid=(b, nh),
        in_specs=[
            smem, smem,
            pl.BlockSpec((1, lp, dk), lambda i, h: (i, 0, h)),
            pl.BlockSpec((1, lp, dk), lambda i, h: (i, 0, k_blk0 + h)),
            pl.BlockSpec((1, lp, dv), lambda i, h: (i, 0, v_blk0 + h)),
            pl.BlockSpec((1, lp, dv), lambda i, h: (i, 0, g_blk0 + h)),
            pl.BlockSpec((1, dv), lambda i, h: (0, h)),
        ],
        out_specs=pl.BlockSpec((1, lp, dv), lambda i, h: (i, 0, h)),
        out_shape=jax.ShapeDtypeStruct((b, lp, nh * dv), BF16),
        scratch_shapes=[
            pltpu.VMEM((lp, dv), F32),
            pltpu.VMEM((dk, dv), F32),
            pltpu.VMEM((dk, dv), F32),
        ],
        compiler_params=_compiler_params(("parallel", "arbitrary")),
        name="retention",
    )(lg_f, lg_b, z, z, z, z, ret_gain.reshape(1, nh * dv))


def _attn_kernel(lam_ref, slope_ref, q_ref, k_ref, v_ref, qg_ref, kg_ref, og_ref, o_ref,
                 kaug_ref, corr_ref, qaug_ref, s_ref, p_ref, *, tq, tk, n_pad):
    h = pl.program_id(1)
    qi = pl.program_id(2)
    lp = k_ref.shape[1]
    d = DIFF_QK_DIM
    n_chunks = lp // tk
    lanes = 128
    lam = lam_ref[0]
    log2e = math.log2(math.e)
    slope = slope_ref[h] * log2e
    contract_last = (((1,), (1,)), ((), ()))

    def qk_norm(x, gain):
        return x * lax.rsqrt(jnp.mean(x * x, axis=-1, keepdims=True) + NORM_EPS) * gain

    def split3(v):
        hi = v.astype(BF16).astype(F32)
        mid = (v - hi).astype(BF16).astype(F32)
        lo = (v - hi - mid).astype(BF16).astype(F32)
        return [hi, mid, lo]

    def extra_lanes(cols):
        rows = cols[0].shape[0]
        lane = lax.broadcasted_iota(jnp.int32, (rows, d), 1)
        out = jnp.zeros((rows, d), F32)
        for t, col in enumerate(cols):
            out = jnp.where(lane == t, col, out)
        return out.astype(BF16)

    @pl.when(qi == 0)
    def _():
        for c in range(n_chunks):
            r = slice(c * tk, (c + 1) * tk)
            kk = k_ref[0, r, :].astype(F32)
            k_row = c * tk + lax.broadcasted_iota(jnp.int32, (tk, 1), 0)
            real = k_row >= CHUNK
            one = jnp.where(real, 1.0, 0.0)
            first = jnp.where(k_row < n_pad, NEG_BIG, 0.0)
            sk = [jnp.where(real, t, 0.0) for t in split3(slope * k_row.astype(F32))]
            extra = extra_lanes([sk[0] + first, sk[1], sk[2], one, one, one])
            for m in range(2):
                kaug_ref[m, r, :d] = qk_norm(kk[:, m * d:(m + 1) * d], kg_ref[...]).astype(BF16)
                kaug_ref[m, r, d:] = extra
        i_loc = lax.broadcasted_iota(jnp.int32, (tq, tk), 0)
        j_loc = lax.broadcasted_iota(jnp.int32, (tq, tk), 1)
        corr_ref[...] = (2.0 * slope) * jnp.maximum(j_loc - i_loc, 0).astype(F32)

    qq = q_ref[0].astype(F32)
    qs = [(qk_norm(qq[:, m * d:(m + 1) * d], qg_ref[...]) * (d ** -0.5 * log2e)).astype(BF16)
          for m in range(2)]
    q_row = qi * tq + lax.broadcasted_iota(jnp.int32, (tq, 1), 0)
    real_q = q_row >= CHUNK
    sq = [jnp.where(real_q, t, 0.0) for t in split3(slope * q_row.astype(F32))]
    one = jnp.where(real_q, 1.0, 0.0)
    extra_before = extra_lanes([one, one, one] + [-t for t in sq])
    extra_after = extra_lanes([-one, -one, -one] + sq)
    for m in range(2):
        qaug_ref[0, m] = jnp.concatenate([qs[m], extra_before], axis=1)
        qaug_ref[1, m] = jnp.concatenate([qs[m], extra_after], axis=1)

    def lane_fold(x, op):
        out = x[:, :lanes]
        for t in range(1, x.shape[1] // lanes):
            out = op(out, x[:, t * lanes:(t + 1) * lanes])
        return out

    for c in range(n_chunks):
        r = slice(c * tk, (c + 1) * tk)
        side = jnp.where(c <= qi, 0, 1)
        for m in range(2):
            s_ref[m, c] = lax.dot_general(qaug_ref[side, m], kaug_ref[m, r, :], contract_last,
                                          preferred_element_type=F32)
    for m in range(2):
        s_ref[m, qi] = s_ref[m, qi] - corr_ref[...]

    @pl.when(qi == 0)
    def _():
        q_pos = lax.broadcasted_iota(jnp.int32, (tq, tk), 0)
        k_pos = lax.broadcasted_iota(jnp.int32, (tq, tk), 1)
        dist = jnp.abs(q_pos - k_pos).astype(F32)
        both_real = jnp.logical_and(q_pos >= CHUNK, k_pos >= CHUNK)
        bias = jnp.where(both_real, -slope * dist, 0.0)
        for m in range(2):
            s = lax.dot_general(qs[m], kaug_ref[m, :tk, :d], contract_last, preferred_element_type=F32)
            s_ref[m, 0] = jnp.where(k_pos >= n_pad, s + bias, NEG_BIG)

    outs = []
    band = lanes
    for m in range(2):
        rows_out = []
        for rb in range(tq // band):
            rows = slice(rb * band, (rb + 1) * band)
            m_part = lane_fold(s_ref[m, 0, rows, :], jnp.maximum)
            for c in range(1, n_chunks):
                m_part = jnp.maximum(m_part, lane_fold(s_ref[m, c, rows, :], jnp.maximum))
            row_max = jnp.max(m_part, axis=-1, keepdims=True)
            shift = jnp.concatenate([jnp.broadcast_to(row_max, (band, lanes))] * (tk // lanes), axis=1)
            l_part = jnp.zeros((band, lanes), F32)
            for c in range(n_chunks):
                e = jnp.exp2(s_ref[m, c, rows, :] - shift)
                l_part = l_part + lane_fold(e, jnp.add)
                p_ref[m, rows, c * tk:(c + 1) * tk] = e.astype(BF16)
            acc = jnp.dot(p_ref[m, rows, :], v_ref[0], preferred_element_type=F32)
            rows_out.append(acc / jnp.sum(l_part, axis=-1, keepdims=True))
        outs.append(jnp.concatenate(rows_out, axis=0))

    o = outs[0] - lam * outs[1]
    o = o * lax.rsqrt(jnp.mean(o * o, axis=-1, keepdims=True) + NORM_EPS) * og_ref[...]
    o_ref[0] = (o * (1.0 - LAMBDA_INIT)).astype(o_ref.dtype)


def _attention(z, lam, slopes, q_gain, k_gain, out_gain, *, tq, tk):
    b, lp, _ = z.shape
    nh, d, dv = DIFF_HEADS, DIFF_QK_DIM, DIFF_V_DIM
    assert tq == tk and tq >= CHUNK and lp % tq == 0
    ret_cols = 2 * RET_HEADS * RET_QK_DIM + 2 * RET_HEADS * RET_V_DIM
    q_blk0 = ret_cols // (2 * d)
    k_blk0 = q_blk0 + nh
    v_blk0 = k_blk0 + nh
    smem = pl.BlockSpec(memory_space=pltpu.SMEM)
    return pl.pallas_call(
        functools.partial(_attn_kernel, tq=tq, tk=tk, n_pad=CHUNK - N_META),
        grid=(b, nh, lp // tq),
        in_specs=[
            smem, smem,
            pl.BlockSpec((1, tq, 2 * d), lambda i, h, q: (i, q, q_blk0 + h)),
            pl.BlockSpec((1, lp, 2 * d), lambda i, h, q: (i, 0, k_blk0 + h)),
            pl.BlockSpec((1, lp, dv), lambda i, h, q: (i, 0, v_blk0 + h)),
            pl.BlockSpec((1, d), lambda i, h, q: (0, 0)),
            pl.BlockSpec((1, d), lambda i, h, q: (0, 0)),
            pl.BlockSpec((1, dv), lambda i, h, q: (0, h)),
        ],
        out_specs=pl.BlockSpec((1, tq, dv), lambda i, h, q: (i, q, h)),
        out_shape=jax.ShapeDtypeStruct((b, lp, nh * dv), BF16),
        scratch_shapes=[
            pltpu.VMEM((2, lp, 2 * d), BF16),
            pltpu.VMEM((tq, tk), F32),
            pltpu.VMEM((2, 2, tq, 2 * d), BF16),
            pltpu.VMEM((2, lp // tk, tq, tk), F32),
            pltpu.VMEM((2, tq, lp), BF16),
        ],
        compiler_params=_compiler_params(("parallel", "parallel", "arbitrary")),
        name="attention",
    )(lam, slopes, z, z, z, q_gain.reshape(1, d), k_gain.reshape(1, d), out_gain.reshape(1, nh * dv))


def _pack_bf16_pairs(x):
    n = x.shape[1] // 2
    xb = x.astype(BF16).astype(F32)
    lo = lax.bitcast_convert_type(xb[:, :n], jnp.uint32)
    hi = lax.bitcast_convert_type(xb[:, n:], jnp.uint32)
    return (hi & jnp.uint32(0xFFFF0000)) | (lo >> 16)


def _unpack_bf16_pairs(p):
    lo = lax.bitcast_convert_type(p << 16, F32).astype(BF16)
    hi = lax.bitcast_convert_type(p & jnp.uint32(0xFFFF0000), F32).astype(BF16)
    return lo, hi


def _out_proj_kernel(ret_ref, att_ref, hp_ref, w_ref, g_ref, wr_ref, br_ref,
                     h1_ref, xt_ref, route_ref, cnt_ref, run_ref, *, lp, n_pad):
    i = pl.program_id(0)

    @pl.when(i == 0)
    def _():
        run_ref[...] = jnp.zeros_like(run_ref)

    kr = ret_ref.shape[1]
    y = jnp.dot(ret_ref[...], w_ref[:kr, :], preferred_element_type=F32)
    y = y + jnp.dot(att_ref[...], w_ref[kr:, :], preferred_element_type=F32)
    h1 = hp_ref[...] + y
    h1_ref[...] = h1
    xt = h1 * lax.rsqrt(jnp.mean(h1 * h1, axis=-1, keepdims=True) + NORM_EPS) * g_ref[...]
    xt_ref[...] = _pack_bf16_pairs(xt)
    xt_hi = xt.astype(BF16)
    xt_lo = (xt - xt_hi.astype(F32)).astype(BF16)
    wr = wr_ref[...]
    wr_hi = wr.astype(BF16)
    wr_lo = (wr - wr_hi.astype(F32)).astype(BF16)
    logits = (jnp.dot(xt_hi, wr_hi, preferred_element_type=F32)
              + jnp.dot(xt_lo, wr_hi, preferred_element_type=F32)
              + jnp.dot(xt_hi, wr_lo, preferred_element_type=F32)) + br_ref[...]
    tm, ne = logits.shape
    lane = lax.broadcasted_iota(jnp.int32, (tm, ne), 1).astype(F32)
    out_lane = lax.broadcasted_iota(jnp.int32, route_ref.shape, 1)
    vals, idxs = [], []
    work = logits
    for _ in range(TOP_K):
        top = jnp.max(work, axis=-1, keepdims=True)
        idx = jnp.min(jnp.where(work == top, lane, float(ne)), axis=-1, keepdims=True)
        work = jnp.where(lane == idx, -jnp.inf, work)
        vals.append(top)
        idxs.append(idx)
    exps = [jnp.exp(v - vals[0]) for v in vals]
    denom = exps[0] + exps[1] + exps[2] + exps[3]

    local_row = (i * tm) % lp + lax.broadcasted_iota(jnp.int32, (tm, 1), 0)
    member = jnp.zeros((tm, ne), F32)
    for k in range(TOP_K):
        member = member + jnp.where(lane == idxs[k], 1.0, 0.0)
    member = jnp.where(local_row >= n_pad, member, 0.0)
    earlier = (lax.broadcasted_iota(jnp.int32, (tm, tm), 0)
               > lax.broadcasted_iota(jnp.int32, (tm, tm), 1))
    before = run_ref[...] + jnp.dot(jnp.where(earlier, 1.0, 0.0).astype(BF16), member.astype(BF16),
                                    preferred_element_type=F32)
    run_ref[...] = run_ref[...] + jnp.sum(member, axis=0, keepdims=True)
    cnt_ref[...] = run_ref[...]

    slab = jnp.zeros(route_ref.shape, F32)
    for k in range(TOP_K):
        rank = jnp.sum(jnp.where(lane == idxs[k], before, 0.0), axis=-1, keepdims=True)
        slab = jnp.where(out_lane == k, idxs[k], slab)
        slab = jnp.where(out_lane == TOP_K + k, exps[k] / denom, slab)
        slab = jnp.where(out_lane == 2 * TOP_K + k, rank, slab)
    route_ref[...] = slab


def _out_proj(ret2d, att2d, hp2d, w_bf16, gain, w_router, b_router, *, tm, lp, n_pad):
    m, kr = ret2d.shape
    d = hp2d.shape[1]
    ne = w_router.shape[1]
    assert lp % tm == 0
    row = lambda i: (i, 0)
    const = lambda i: (0, 0)
    return pl.pallas_call(
        functools.partial(_out_proj_kernel, lp=lp, n_pad=n_pad),
        grid=(m // tm,),
        in_specs=[
            pl.BlockSpec((tm, kr), row),
            pl.BlockSpec((tm, kr), row),
            pl.BlockSpec((tm, d), row),
            pl.BlockSpec((2 * kr, d), const),
            pl.BlockSpec((1, d), const),
            pl.BlockSpec((d, ne), const),
            pl.BlockSpec((1, ne), const),
        ],
        out_specs=[
            pl.BlockSpec((tm, d), row),
            pl.BlockSpec((tm, d // 2), row),
            pl.BlockSpec((tm, 128), row),
            pl.BlockSpec((1, ne), const),
        ],
        out_shape=[
            jax.ShapeDtypeStruct((m, d), F32),
            jax.ShapeDtypeStruct((m, d // 2), jnp.uint32),
            jax.ShapeDtypeStruct((m, 128), F32),
            jax.ShapeDtypeStruct((1, ne), F32),
        ],
        scratch_shapes=[pltpu.VMEM((1, ne), F32)],
        compiler_params=_compiler_params(("arbitrary",)),
        name="out_proj",
    )(ret2d, att2d, hp2d, w_bf16, gain.reshape(1, d), w_router, b_router.reshape(1, ne))


EXPERT_TILE = 1024
EXPERT_SUB = 256
EXPERT_UP_COLS = 256
EXPERT_DOWN_COLS = 512


def _expert_kernel(te_ref, nv_ref, q0_ref, nt_ref, rows_ref,
                   x_hbm, wg_ref, wu_ref, bg_ref, bu_ref, wd_ref, bd_ref, o_ref,
                   xbuf, xb_ref, act_ref, wgu_bf, wd_bf, sem, *, tm, sub, n_up, n_down, n_assign):
    i = pl.program_id(0)
    j = pl.program_id(1)
    n_used = nt_ref[0]
    used = i < n_used
    nv = nv_ref[i]
    slot = i % 2
    half = xbuf.shape[2]
    rows_per_step = tm // n_up
    up_phase = j < n_up

    def row_copy(tile, r, s):
        q = jnp.minimum(q0_ref[tile] + r, n_assign - 1)
        return pltpu.make_async_copy(x_hbm.at[rows_ref[q]], xbuf.at[s, r], sem.at[s])

    def gather_next_rows():
        for r in range(rows_per_step):
            row_copy(i + 1, j * rows_per_step + r, 1 - slot).start()

    @pl.when(jnp.logical_and(i == 0, j == 0))
    def _():
        def body(r, carry):
            row_copy(0, r, 0).start()
            return carry
        lax.fori_loop(0, tm, body, 0)

    @pl.when(jnp.logical_and(i <= n_used, j == 0))
    def _():
        pltpu.make_async_copy(xbuf.at[slot], xbuf.at[slot], sem.at[slot]).wait()

    @pl.when(jnp.logical_and(used, j == 0))
    def _():
        for s in range(tm // sub):
            r = slice(s * sub, (s + 1) * sub)
            lo, hi = _unpack_bf16_pairs(xbuf[slot, r, :])
            xb_ref[r, :half] = lo
            xb_ref[r, half:] = hi

    @pl.when(jnp.logical_and(used, up_phase))
    def _():
        wgu_bf[0] = wg_ref[0].astype(BF16)
        wgu_bf[1] = wu_ref[0].astype(BF16)

    @pl.when(jnp.logical_and(used, jnp.logical_not(up_phase)))
    def _():
        wd_bf[...] = wd_ref[0].astype(BF16)

    for s in range(tm // sub):
        r = slice(s * sub, (s + 1) * sub)
        live = jnp.logical_and(used, s * sub < nv)

        @pl.when(jnp.logical_and(live, up_phase))
        def _():
            if s == 0:
                gather_next_rows()
            x = xb_ref[r, :]
            gate = jnp.dot(x, wgu_bf[0], preferred_element_type=F32) + bg_ref[0]
            up = jnp.dot(x, wgu_bf[1], preferred_element_type=F32) + bu_ref[0]
            gate = jnp.minimum(gate, SWIGLU_LIMIT)
            up = jnp.clip(up, -SWIGLU_LIMIT, SWIGLU_LIMIT)
            glu = gate * jax.nn.sigmoid(SWIGLU_ALPHA * gate)
            act_ref[j, r, :] = ((up + 1.0) * glu).astype(BF16)

        @pl.when(jnp.logical_and(live, jnp.logical_not(up_phase)))
        def _():
            act =jnp.concatenate([act_ref[c, r, :] for c in range(n_up)], axis=1)
            o_ref[r, :] = jnp.dot(act, wd_bf[...], preferred_element_type=F32) + bd_ref[0]

        @pl.when(jnp.logical_and(jnp.logical_not(live), jnp.logical_not(up_phase)))
        def _():
            o_ref[r, :] = jnp.zeros((sub, o_ref.shape[1]), F32)


def _experts(tile_expert, tile_valid, tile_q0, n_used, sorted_rows, x_packed,
             w_gate_up, b_gate_up, w_down, b_down, *, n_tiles):
    tm, sub = EXPERT_TILE, EXPERT_SUB
    n_assign = sorted_rows.shape[0]
    half = x_packed.shape[1]
    ne, d, f2 = w_gate_up.shape
    f = f2 // 2
    assert d == 2 * half
    tf, tn = EXPERT_UP_COLS, EXPERT_DOWN_COLS
    n_up, n_down = f // tf, d // tn
    assert tm % n_up == 0 and tm % sub == 0

    def up_chunk(i, j, nt):
        return jnp.where(i < nt[0], jnp.minimum(j, n_up - 1), n_up - 1)

    def down_chunk(i, j, nt):
        return jnp.where(i < nt[0], jnp.maximum(j - n_up, 0), 0)

    b_gu = b_gate_up.reshape(ne, 1, f2)
    b_d = b_down.reshape(ne, 1, d)
    return pl.pallas_call(
        functools.partial(_expert_kernel, tm=tm, sub=sub, n_up=n_up, n_down=n_down, n_assign=n_assign),
        grid_spec=pltpu.PrefetchScalarGridSpec(
            num_scalar_prefetch=5,
            grid=(n_tiles, n_up + n_down),
            in_specs=[
                pl.BlockSpec(memory_space=pl.ANY),
                pl.BlockSpec((1, d, tf), lambda i, j, te, nv, q0, nt, rows: (te[i], 0, up_chunk(i, j, nt))),
                pl.BlockSpec((1, d, tf), lambda i, j, te, nv, q0, nt, rows: (te[i], 0, n_up + up_chunk(i, j, nt))),
                pl.BlockSpec((1, 1, tf), lambda i, j, te, nv, q0, nt, rows: (te[i], 0, up_chunk(i, j, nt))),
                pl.BlockSpec((1, 1, tf), lambda i, j, te, nv, q0, nt, rows: (te[i], 0, n_up + up_chunk(i, j, nt))),
                pl.BlockSpec((1, f, tn), lambda i, j, te, nv, q0, nt, rows: (te[i], 0, down_chunk(i, j, nt))),
                pl.BlockSpec((1, 1, tn), lambda i, j, te, nv, q0, nt, rows: (te[i], 0, down_chunk(i, j, nt))),
            ],
            out_specs=pl.BlockSpec((tm, tn), lambda i, j, te, nv, q0, nt, rows: (i, jnp.maximum(j - n_up, 0))),
            scratch_shapes=[
                pltpu.VMEM((2, tm, half), jnp.uint32),
                pltpu.VMEM((tm, d), BF16),
                pltpu.VMEM((n_up, tm, tf), BF16),
                pltpu.VMEM((2, d, tf), BF16),
                pltpu.VMEM((f, tn), BF16),
                pltpu.SemaphoreType.DMA((2,)),
            ],
        ),
        out_shape=jax.ShapeDtypeStruct((n_tiles * tm, d), F32),
        compiler_params=_compiler_params(("arbitrary", "arbitrary")),
        name="experts",
    )(tile_expert, tile_valid, tile_q0, n_used, sorted_rows, x_packed,
      w_gate_up, w_gate_up, b_gu, b_gu, w_down, b_d)


def _combine_kernel(dest_ref, h1_ref, gate_ref, y_hbm, o_ref, buf, sem, *, tt, tiles_per_batch,
                    tok_per_batch):
    bi = pl.program_id(0)
    ti = pl.program_id(1)
    base = (bi * tok_per_batch + N_META + ti * tt) * TOP_K

    def copy(r, k):
        return pltpu.make_async_copy(y_hbm.at[dest_ref[base + r * TOP_K + k]], buf.at[k, r], sem)

    def start(r, carry):
        for k in range(TOP_K):
            copy(r, k).start()
        return carry

    def wait(r, carry):
        for k in range(TOP_K):
            copy(r, k).wait()
        return carry

    lax.fori_loop(0, tt, start, 0)
    lax.fori_loop(0, tt, wait, 0)
    g = gate_ref[0]
    acc = h1_ref[0]
    moe = buf[0] * g[:, 0:1]
    for k in range(1, TOP_K):
        moe = moe + buf[k] * g[:, k:k + 1]
    o_ref[0] = acc + moe


def _combine(dest_flat, h1, gates, y_sorted, *, tt):
    b, lp, d = h1.shape
    s = lp - CHUNK
    tok_per_batch = s + N_META
    tiles = s // tt
    off = CHUNK // tt
    gates_tok = gates[:, N_META:, :]
    return pl.pallas_call(
        functools.partial(_combine_kernel, tt=tt, tiles_per_batch=tiles, tok_per_batch=tok_per_batch),
        grid_spec=pltpu.PrefetchScalarGridSpec(
            num_scalar_prefetch=1,
            grid=(b, tiles),
            in_specs=[
                pl.BlockSpec((1, tt, d), lambda i, t, dest: (i, off + t, 0)),
                pl.BlockSpec((1, tt, TOP_K), lambda i, t, dest: (i, t, 0)),
                pl.BlockSpec(memory_space=pl.ANY),
            ],
            out_specs=pl.BlockSpec((1, tt, d), lambda i, t, dest: (i, t, 0)),
            scratch_shapes=[pltpu.VMEM((TOP_K, tt, d), F32), pltpu.SemaphoreType.DMA],
        ),
        out_shape=jax.ShapeDtypeStruct((b, s, d), F32),
        compiler_params=_compiler_params(("arbitrary", "arbitrary")),
        name="combine",
    )(dest_flat, h1, gates_tok, y_sorted)


def _lookup(table, idx):
    n = table.shape[0]
    hit = idx[..., None] == jnp.arange(n, dtype=idx.dtype)
    return jnp.sum(jnp.where(hit, table, 0), axis=-1)


def _routing(top_idx, rank, counts, lp, n_pad, *, tm):
    b, l, k = top_idx.shape
    ne = counts.shape[0]
    p = b * l * k
    n_tiles = p // tm + ne + 1
    tiles_per = (counts + tm - 1) // tm
    tile_end = jnp.cumsum(tiles_per)
    tile_start = tile_end - tiles_per
    n_used = tile_end[-1]
    starts = jnp.cumsum(counts) - counts
    tile_id = jnp.arange(n_tiles, dtype=jnp.int32)
    tile_expert = jnp.minimum(jnp.sum(tile_end[None, :] <= tile_id[:, None], axis=-1), ne - 1).astype(jnp.int32)
    within = tile_id - _lookup(tile_start, tile_expert)
    tile_valid = jnp.where(tile_id < n_used, jnp.clip(_lookup(counts, tile_expert) - within * tm, 0, tm), 0)
    tile_q0 = jnp.where(tile_id < n_used, _lookup(starts, tile_expert) + within * tm, 0)
    tok = jnp.arange(b * l, dtype=jnp.int32)
    tok_bits = max(1, (b * l - 1).bit_length())
    keys = jnp.sort((top_idx.reshape(b * l, k) << tok_bits | tok[:, None]).reshape(p))
    sorted_tok = keys & ((1 << tok_bits) - 1)
    sorted_rows = (sorted_tok // l) * lp + n_pad + sorted_tok % l
    dest = _lookup(tile_start * tm, top_idx) + rank
    return (tile_expert, tile_valid.astype(jnp.int32), tile_q0.astype(jnp.int32),
            n_used.astype(jnp.int32).reshape(1), sorted_rows.astype(jnp.int32),
            dest.astype(jnp.int32).reshape(p), n_tiles)


def kernel(x, meta_tokens, attn_norm_gain, w_in, ret_decay_fwd, ret_decay_bwd, ret_out_gain,
           diff_q_gain, diff_k_gain, diff_lambda_q1, diff_lambda_k1, diff_lambda_q2, diff_lambda_k2,
           diff_out_gain, w_out, ffn_norm_gain, w_router, b_router, w_gate_up, b_gate_up,
           w_down, b_down):
    b, s, d = x.shape
    assert s % CHUNK == 0 and w_in.shape[0] == 1, "single layer, token count a chunk multiple"
    n_pad = CHUNK - N_META
    lp = s + CHUNK
    l = s + N_META

    hp, u = _prep(x, meta_tokens.astype(x.dtype), attn_norm_gain[0])
    z = _in_proj(u.reshape(b * lp, d), w_in[0].astype(BF16), tm=lp // 4, tn=1024)
    z = z.reshape(b, lp, -1)

    lg_f = jnp.log1p(-jnp.exp2(-ret_decay_fwd[0].astype(F32)))
    lg_b = jnp.log1p(-jnp.exp2(-ret_decay_bwd[0].astype(F32)))
    ret = _retention(z, lg_f, lg_b, ret_out_gain[0])

    lam = (jnp.exp(jnp.sum(diff_lambda_q1[0].astype(F32) * diff_lambda_k1[0].astype(F32)))
           - jnp.exp(jnp.sum(diff_lambda_q2[0].astype(F32) * diff_lambda_k2[0].astype(F32)))
           + LAMBDA_INIT).reshape(1)
    slopes = jnp.exp2(-8.0 * jnp.arange(1, DIFF_HEADS + 1, dtype=F32) / DIFF_HEADS)
    att = _attention(z, lam, slopes, diff_q_gain[0], diff_k_gain[0], diff_out_gain[0], tq=384, tk=384)

    h1, xt, route, counts = _out_proj(
        ret.reshape(b * lp, -1), att.reshape(b * lp, -1), hp.reshape(b * lp, d),
        w_out[0].astype(BF16), ffn_norm_gain[0], w_router[0], b_router[0], tm=lp // 8, lp=lp, n_pad=n_pad)

    route = route.reshape(b, lp, 128)[:, n_pad:, :]
    top_idx = route[:, :, :TOP_K].astype(jnp.int32)
    gates = route[:, :, TOP_K:2 * TOP_K]
    rank = route[:, :, 2 * TOP_K:3 * TOP_K].astype(jnp.int32)

    tile_expert, tile_valid, tile_q0, n_used, sorted_rows, dest, n_tiles = _routing(
        top_idx, rank, counts.reshape(-1).astype(jnp.int32), lp, n_pad, tm=EXPERT_TILE)
    ys = _experts(tile_expert, tile_valid, tile_q0, n_used, sorted_rows, xt,
                  w_gate_up[0], b_gate_up[0], w_down[0], b_down[0], n_tiles=n_tiles)
    return _combine(dest, h1.reshape(b, lp, d), gates, ys, tt=CHUNK)
```

```python
import functools
import math

import jax
import jax.numpy as jnp
from jax import lax
from jax.experimental import pallas as pl
from jax.experimental.pallas import tpu as pltpu

N_META = 16
CHUNK = 128
RET_HEADS = 4
RET_QK_DIM = 128
RET_V_DIM = 256
DIFF_HEADS = 4
DIFF_QK_DIM = 128
DIFF_V_DIM = 256
TOP_K = 4
SWIGLU_ALPHA = 1.702
SWIGLU_LIMIT = 7.0
NORM_EPS = 1e-6
RET_DECAY_BASE = 5.0
LAMBDA_INIT = 0.8 - 0.6 * math.exp(-0.3 * 0)
NEG_BIG = -1e30

V7X_VMEM_BYTES = 64 * 1024 * 1024
VMEM_LIMIT_BYTES = V7X_VMEM_BYTES - 8 * 1024 * 1024

F32 = jnp.float32
BF16 = jnp.bfloat16


def _compiler_params(semantics):
    return pltpu.CompilerParams(dimension_semantics=semantics, vmem_limit_bytes=VMEM_LIMIT_BYTES)


def _prep_kernel(x_ref, meta_ref, g_ref, hp_ref, u_ref, *, n_pad):
    c = pl.program_id(1)

    def emit(h):
        hp_ref[0] = h
        ms = jnp.mean(h * h, axis=-1, keepdims=True)
        u_ref[0] = (h * lax.rsqrt(ms + NORM_EPS) * g_ref[...]).astype(u_ref.dtype)

    @pl.when(c == 0)
    def _():
        d = meta_ref.shape[-1]
        emit(jnp.concatenate([jnp.zeros((n_pad, d), F32), meta_ref[...]], axis=0))

    @pl.when(c > 0)
    def _():
        emit(x_ref[0])


def _prep(x, meta, gain):
    b, s, d = x.shape
    n_pad = CHUNK - N_META
    nc = s // CHUNK + 1
    lp = nc * CHUNK
    return pl.pallas_call(
        functools.partial(_prep_kernel, n_pad=n_pad),
        grid=(b, nc),
        in_specs=[
            pl.BlockSpec((1, CHUNK, d), lambda i, c: (i, jnp.maximum(c - 1, 0), 0)),
            pl.BlockSpec((N_META, d), lambda i, c: (0, 0)),
            pl.BlockSpec((1, d), lambda i, c: (0, 0)),
        ],
        out_specs=[
            pl.BlockSpec((1, CHUNK, d), lambda i, c: (i, c, 0)),
            pl.BlockSpec((1, CHUNK, d), lambda i, c: (i, c, 0)),
        ],
        out_shape=[
            jax.ShapeDtypeStruct((b, lp, d), F32),
            jax.ShapeDtypeStruct((b, lp, d), BF16),
        ],
        compiler_params=_compiler_params(("parallel", "arbitrary")),
        name="prep",
    )(x, meta, gain.reshape(1, d))


def _matmul_kernel(a_ref, b_ref, o_ref):
    o_ref[...] = jnp.dot(a_ref[...], b_ref[...], preferred_element_type=F32).astype(o_ref.dtype)


def _in_proj(u2d, w_bf16, *, tm, tn):
    m, k = u2d.shape
    n = w_bf16.shape[1]
    return pl.pallas_call(
        _matmul_kernel,
        grid=(m // tm, n // tn),
        in_specs=[
            pl.BlockSpec((tm, k), lambda i, j: (i, 0)),
            pl.BlockSpec((k, tn), lambda i, j: (0, j)),
        ],
        out_specs=pl.BlockSpec((tm, tn), lambda i, j: (i, j)),
        out_shape=jax.ShapeDtypeStruct((m, n), BF16),
        compiler_params=_compiler_params(("parallel", "arbitrary")),
        name="in_proj",
    )(u2d, w_bf16)


def _retention_kernel(lgf_ref, lgb_ref, q_ref, k_ref, v_ref, g_ref, gain_ref, o_ref,
                      ob_ref, sf_ref, sb_ref, dec_ref, *, nc, heads):
    c = CHUNK
    dk, dv = RET_QK_DIM, RET_V_DIM
    scale = dk ** -0.5
    row = lax.broadcasted_iota(jnp.int32, (c, c), 0)
    col = lax.broadcasted_iota(jnp.int32, (c, c), 1)
    rel = (row - col).astype(F32)
    pos = lax.broadcasted_iota(jnp.int32, (c, 1), 0).astype(F32)
    q_dec_f, k_dec_f, q_dec_b, k_dec_b, chunk_f, chunk_b = [], [], [], [], [], []
    for hh in range(heads):
        lgf = lgf_ref[pl.program_id(1) * heads + hh]
        lgb = lgb_ref[pl.program_id(1) * heads + hh]
        dec_ref[hh] = jnp.where(rel >= 0, jnp.exp(lgf * jnp.maximum(rel, 0.0)),
                                jnp.exp(lgb * jnp.maximum(-rel, 0.0))) * scale
        q_dec_f.append(jnp.exp(lgf * (pos + 1.0)))
        k_dec_f.append(jnp.exp(lgf * (c - 1.0 - pos)) * scale)
        q_dec_b.append(jnp.exp(lgb * (c - pos)))
        k_dec_b.append(jnp.exp(lgb * pos) * scale)
        chunk_f.append(jnp.exp(lgf * c))
        chunk_b.append(jnp.exp(lgb * c))

    def rows(n):
        return pl.ds(pl.multiple_of(n * c, c), c)

    def qk_cols(hh):
        return slice(hh * dk, (hh + 1) * dk)

    def v_cols(hh):
        return slice(hh * dv, (hh + 1) * dv)

    def state_update(s_ref, hh, kn, vn, k_dec, chunk_decay):
        kd_t = (kn.astype(F32) * k_dec).T.astype(BF16)
        s_ref[hh] = s_ref[hh] * chunk_decay + jnp.dot(kd_t, vn, preferred_element_type=F32)

    sb_ref[...] = jnp.zeros_like(sb_ref)
    sf_ref[...] = jnp.zeros_like(sf_ref)

    def bwd_body(t, carry):
        n = nc - 1 - t
        for hh in range(heads):
            qn = q_ref[0, rows(n), qk_cols(hh)]
            kn = k_ref[0, rows(n), qk_cols(hh)]
            vn = v_ref[0, rows(n), v_cols(hh)]
            qd = (qn.astype(F32) * q_dec_b[hh]).astype(BF16)
            ob_ref[rows(n), v_cols(hh)] = jnp.dot(qd, sb_ref[hh].astype(BF16), preferred_element_type=F32)
            state_update(sb_ref, hh, kn, vn, k_dec_b[hh], chunk_b[hh])
        return carry

    lax.fori_loop(0, nc, bwd_body, 0)

    def fwd_body(n, carry):
        for hh in range(heads):
            qn = q_ref[0, rows(n), qk_cols(hh)]
            kn = k_ref[0, rows(n), qk_cols(hh)]
            vn = v_ref[0, rows(n), v_cols(hh)]
            sc = lax.dot_general(qn, kn, (((1,), (1,)), ((), ())), preferred_element_type=F32)
            inner = jnp.dot((sc * dec_ref[hh]).astype(BF16), vn, preferred_element_type=F32)
            qd = (qn.astype(F32) * q_dec_f[hh]).astype(BF16)
            cross = jnp.dot(qd, sf_ref[hh].astype(BF16), preferred_element_type=F32)
            o = inner + cross + ob_ref[rows(n), v_cols(hh)]
            state_update(sf_ref, hh, kn, vn, k_dec_f[hh], chunk_f[hh])
            o = o * lax.rsqrt(jnp.mean(o * o, axis=-1, keepdims=True) + NORM_EPS) * gain_ref[:, v_cols(hh)]
            gate = g_ref[0, rows(n), v_cols(hh)].astype(F32)
            o_ref[0, rows(n), v_cols(hh)] = (o * (gate * jax.nn.sigmoid(gate))).astype(o_ref.dtype)
        return carry

    lax.fori_loop(0, nc, fwd_body, 0)


def _retention(z, lg_f, lg_b, ret_gain):
    b, lp, _ = z.shape
    nc = lp // CHUNK
    dk, dv, nh = RET_QK_DIM, RET_V_DIM, RET_HEADS
    heads = 2
    groups = nh // heads
    k_blk0 = groups
    v_blk0 = 2 * nh * dk // (heads * dv)
    g_blk0 = v_blk0 + groups
    smem = pl.BlockSpec(memory_space=pltpu.SMEM)
    return pl.pallas_call(
        functools.partial(_retention_kernel, nc=nc, heads=heads),
        grid=(b, groups),
        in_specs=[
            smem, smem,
            pl.BlockSpec((1, lp, heads * dk), lambda i, h: (i, 0, h)),
            pl.BlockSpec((1, lp, heads * dk), lambda i, h: (i, 0, k_blk0 + h)),
            pl.BlockSpec((1, lp, heads * dv), lambda i, h: (i, 0, v_blk0 + h)),
            pl.BlockSpec((1, lp, heads * dv), lambda i, h: (i, 0, g_blk0 + h)),
            pl.BlockSpec((1, heads * dv), lambda i, h: (0, h)),
        ],
        out_specs=pl.BlockSpec((1, lp, heads * dv), lambda i, h: (i, 0, h)),
        out_shape=jax.ShapeDtypeStruct((b, lp, nh * dv), BF16),
        scratch_shapes=[
            pltpu.VMEM((lp, heads * dv), F32),
            pltpu.VMEM((heads, dk, dv), F32),
            pltpu.VMEM((heads, dk, dv), F32),
            pltpu.VMEM((heads, CHUNK, CHUNK), F32),
        ],
        compiler_params=_compiler_params(("parallel", "arbitrary")),
        name="retention",
    )(lg_f, lg_b, z, z, z, z, ret_gain.reshape(1, nh * dv))


def _attn_kernel(lam_ref, slope_ref, q_ref, k_ref, v_ref, qg_ref, kg_ref, og_ref, o_ref,
                 kaug_ref, corr_ref, qaug_ref, s_ref, p_ref, *, tq, tk, n_pad):
    h = pl.program_id(1)
    qi = pl.program_id(2)
    lp = k_ref.shape[1]
    d = DIFF_QK_DIM
    n_chunks = lp // tk
    lanes = 128
    lam = lam_ref[0]
    log2e = math.log2(math.e)
    slope = slope_ref[h] * log2e
    contract_last = (((1,), (1,)), ((), ()))

    def qk_norm(x, gain):
        return x * lax.rsqrt(jnp.mean(x * x, axis=-1, keepdims=True) + NORM_EPS) * gain

    def split3(v):
        hi = v.astype(BF16).astype(F32)
        mid = (v - hi).astype(BF16).astype(F32)
        lo = (v - hi - mid).astype(BF16).astype(F32)
        return [hi, mid, lo]

    def extra_lanes(cols):
        rows = cols[0].shape[0]
        lane = lax.broadcasted_iota(jnp.int32, (rows, d), 1)
        out = jnp.zeros((rows, d), F32)
        for t, col in enumerate(cols):
            out = jnp.where(lane == t, col, out)
        return out.astype(BF16)

    @pl.when(qi == 0)
    def _():
        for c in range(n_chunks):
            r = slice(c * tk, (c + 1) * tk)
            kk = k_ref[0, r, :].astype(F32)
            k_row = c * tk + lax.broadcasted_iota(jnp.int32, (tk, 1), 0)
            real = k_row >= CHUNK
            one = jnp.where(real, 1.0, 0.0)
            first = jnp.where(k_row < n_pad, NEG_BIG, 0.0)
            sk = [jnp.where(real, t, 0.0) for t in split3(slope * k_row.astype(F32))]
            extra = extra_lanes([sk[0] + first, sk[1], sk[2], one, one, one])
            for m in range(2):
                kaug_ref[m, r, :d] = qk_norm(kk[:, m * d:(m + 1) * d], kg_ref[...]).astype(BF16)
                kaug_ref[m, r, d:] = extra
        i_loc = lax.broadcasted_iota(jnp.int32, (tq, tk), 0)
        j_loc = lax.broadcasted_iota(jnp.int32, (tq, tk), 1)
        corr_ref[...] = (2.0 * slope) * jnp.maximum(j_loc - i_loc, 0).astype(F32)

    qq = q_ref[0].astype(F32)
    qs = [(qk_norm(qq[:, m * d:(m + 1) * d], qg_ref[...]) * (d ** -0.5 * log2e)).astype(BF16)
          for m in range(2)]
    q_row = qi * tq + lax.broadcasted_iota(jnp.int32, (tq, 1), 0)
    real_q = q_row >= CHUNK
    sq = [jnp.where(real_q, t, 0.0) for t in split3(slope * q_row.astype(F32))]
    one = jnp.where(real_q, 1.0, 0.0)
    extra_before = extra_lanes([one, one, one] + [-t for t in sq])
    extra_after = extra_lanes([-one, -one, -one] + sq)
    for m in range(2):
        qaug_ref[0, m] = jnp.concatenate([qs[m], extra_before], axis=1)
        qaug_ref[1, m] = jnp.concatenate([qs[m], extra_after], axis=1)

    def lane_fold(x, op):
        out = x[:, :lanes]
        for t in range(1, x.shape[1] // lanes):
            out = op(out, x[:, t * lanes:(t + 1) * lanes])
        return out

    for c in range(n_chunks):
        r = slice(c * tk, (c + 1) * tk)
        side = jnp.where(c <= qi, 0, 1)
        for m in range(2):
            s_ref[m, c] = lax.dot_general(qaug_ref[side, m], kaug_ref[m, r, :], contract_last,
                                          preferred_element_type=F32)
    for m in range(2):
        s_ref[m, qi] = s_ref[m, qi] - corr_ref[...]

    @pl.when(qi == 0)
    def _():
        q_pos = lax.broadcasted_iota(jnp.int32, (tq, tk), 0)
        k_pos = lax.broadcasted_iota(jnp.int32, (tq, tk), 1)
        dist = jnp.abs(q_pos - k_pos).astype(F32)
        both_real = jnp.logical_and(q_pos >= CHUNK, k_pos >= CHUNK)
        bias = jnp.where(both_real, -slope * dist, 0.0)
        for m in range(2):
            s = lax.dot_general(qs[m], kaug_ref[m, :tk, :d], contract_last, preferred_element_type=F32)
            s_ref[m, 0] = jnp.where(k_pos >= n_pad, s + bias, NEG_BIG)

    outs = []
    band = lanes
    for m in range(2):
        rows_out = []
        for rb in range(tq // band):
            rows = slice(rb * band, (rb + 1) * band)
            m_part = lane_fold(s_ref[m, 0, rows, :], jnp.maximum)
            for c in range(1, n_chunks):
                m_part = jnp.maximum(m_part, lane_fold(s_ref[m, c, rows, :], jnp.maximum))
            row_max = jnp.max(m_part, axis=-1, keepdims=True)
            shift = jnp.concatenate([jnp.broadcast_to(row_max, (band, lanes))] * (tk // lanes), axis=1)
            l_part = jnp.zeros((band, lanes), F32)
            for c in range(n_chunks):
                e = jnp.exp2(s_ref[m, c, rows, :] - shift)
                l_part = l_part + lane_fold(e, jnp.add)
                p_ref[m, rows, c * tk:(c + 1) * tk] = e.astype(BF16)
            acc = jnp.dot(p_ref[m, rows, :], v_ref[0], preferred_element_type=F32)
            rows_out.append(acc / jnp.sum(l_part, axis=-1, keepdims=True))
        outs.append(jnp.concatenate(rows_out, axis=0))

    o = outs[0] - lam * outs[1]
    o = o * lax.rsqrt(jnp.mean(o * o, axis=-1, keepdims=True) + NORM_EPS) * og_ref[...]
    o_ref[0] = (o * (1.0 - LAMBDA_INIT)).astype(o_ref.dtype)


def _attention(z, lam, slopes, q_gain, k_gain, out_gain, *, tq, tk):
    b, lp, _ = z.shape
    nh, d, dv = DIFF_HEADS, DIFF_QK_DIM, DIFF_V_DIM
    assert tq == tk and tq >= CHUNK and lp % tq == 0
    ret_cols = 2 * RET_HEADS * RET_QK_DIM + 2 * RET_HEADS * RET_V_DIM
    q_blk0 = ret_cols // (2 * d)
    k_blk0 = q_blk0 + nh
    v_blk0 = k_blk0 + nh
    smem = pl.BlockSpec(memory_space=pltpu.SMEM)
    return pl.pallas_call(
        functools.partial(_attn_kernel, tq=tq, tk=tk, n_pad=CHUNK - N_META),
        grid=(b, nh, lp // tq),
        in_specs=[
            smem, smem,
            pl.BlockSpec((1, tq, 2 * d), lambda i, h, q: (i, q, q_blk0 + h)),
            pl.BlockSpec((1, lp, 2 * d), lambda i, h, q: (i, 0, k_blk0 + h)),
            pl.BlockSpec((1, lp, dv), lambda i, h, q: (i, 0, v_blk0 + h)),
            pl.BlockSpec((1, d), lambda i, h, q: (0, 0)),
            pl.BlockSpec((1, d), lambda i, h, q: (0, 0)),
            pl.BlockSpec((1, dv), lambda i, h, q: (0, h)),
        ],
        out_specs=pl.BlockSpec((1, tq, dv), lambda i, h, q: (i, q, h)),
        out_shape=jax.ShapeDtypeStruct((b, lp, nh * dv), BF16),
        scratch_shapes=[
            pltpu.VMEM((2, lp, 2 * d), BF16),
            pltpu.VMEM((tq, tk), F32),
            pltpu.VMEM((2, 2, tq, 2 * d), BF16),
            pltpu.VMEM((2, lp // tk, tq, tk), F32),
            pltpu.VMEM((2, tq, lp), BF16),
        ],
        compiler_params=_compiler_params(("parallel", "parallel", "arbitrary")),
        name="attention",
    )(lam, slopes, z, z, z, q_gain.reshape(1, d), k_gain.reshape(1, d), out_gain.reshape(1, nh * dv))


def _pack_bf16_pairs(x):
    n = x.shape[1] // 2
    xb = x.astype(BF16).astype(F32)
    lo = lax.bitcast_convert_type(xb[:, :n], jnp.uint32)
    hi = lax.bitcast_convert_type(xb[:, n:], jnp.uint32)
    return (hi & jnp.uint32(0xFFFF0000)) | (lo >> 16)


def _unpack_bf16_pairs(p):
    lo = lax.bitcast_convert_type(p << 16, F32).astype(BF16)
    hi = lax.bitcast_convert_type(p & jnp.uint32(0xFFFF0000), F32).astype(BF16)
    return lo, hi


def _out_proj_kernel(ret_ref, att_ref, hp_ref, w_ref, g_ref, wr_ref, br_ref,
                     h1_ref, xt_ref, route_ref, cnt_ref, run_ref, *, lp, n_pad):
    i = pl.program_id(0)

    @pl.when(i == 0)
    def _():
        run_ref[...] = jnp.zeros_like(run_ref)

    kr = ret_ref.shape[1]
    y = jnp.dot(ret_ref[...], w_ref[:kr, :], preferred_element_type=F32)
    y = y + jnp.dot(att_ref[...], w_ref[kr:, :], preferred_element_type=F32)
    h1 = hp_ref[...] + y
    h1_ref[...] = h1
    xt = h1 * lax.rsqrt(jnp.mean(h1 * h1, axis=-1, keepdims=True) + NORM_EPS) * g_ref[...]
    xt_ref[...] = _pack_bf16_pairs(xt)
    xt_hi = xt.astype(BF16)
    xt_lo = (xt - xt_hi.astype(F32)).astype(BF16)
    wr = wr_ref[...]
    wr_hi = wr.astype(BF16)
    wr_lo = (wr - wr_hi.astype(F32)).astype(BF16)
    logits = (jnp.dot(xt_hi, wr_hi, preferred_element_type=F32)
              + jnp.dot(xt_lo, wr_hi, preferred_element_type=F32)
              + jnp.dot(xt_hi, wr_lo, preferred_element_type=F32)) + br_ref[...]
    tm, ne = logits.shape
    lane = lax.broadcasted_iota(jnp.int32, (tm, ne), 1).astype(F32)
    out_lane = lax.broadcasted_iota(jnp.int32, route_ref.shape, 1)
    vals, idxs = [], []
    work = logits
    for _ in range(TOP_K):
        top = jnp.max(work, axis=-1, keepdims=True)
        idx = jnp.min(jnp.where(work == top, lane, float(ne)), axis=-1, keepdims=True)
        work = jnp.where(lane == idx, -jnp.inf, work)
        vals.append(top)
        idxs.append(idx)
    exps = [jnp.exp(v - vals[0]) for v in vals]
    denom = exps[0] + exps[1] + exps[2] + exps[3]

    local_row = (i * tm) % lp + lax.broadcasted_iota(jnp.int32, (tm, 1), 0)
    member = jnp.zeros((tm, ne), F32)
    for k in range(TOP_K):
        member = member + jnp.where(lane == idxs[k], 1.0, 0.0)
    member = jnp.where(local_row >= n_pad, member, 0.0)
    earlier = (lax.broadcasted_iota(jnp.int32, (tm, tm), 0)
               > lax.broadcasted_iota(jnp.int32, (tm, tm), 1))
    before = run_ref[...] + jnp.dot(jnp.where(earlier, 1.0, 0.0).astype(BF16), member.astype(BF16),
                                    preferred_element_type=F32)
    run_ref[...] = run_ref[...] + jnp.sum(member, axis=0, keepdims=True)
    cnt_ref[...] = run_ref[...]

    slab = jnp.zeros(route_ref.shape, F32)
    for k in range(TOP_K):
        rank = jnp.sum(jnp.where(lane == idxs[k], before, 0.0), axis=-1, keepdims=True)
        slab = jnp.where(out_lane == k, idxs[k], slab)
        slab = jnp.where(out_lane == TOP_K + k, exps[k] / denom, slab)
        slab = jnp.where(out_lane == 2 * TOP_K + k, rank, slab)
    route_ref[...] = slab


def _out_proj(ret2d, att2d, hp2d, w_bf16, gain, w_router, b_router, *, tm, lp, n_pad):
    m, kr = ret2d.shape
    d = hp2d.shape[1]
    ne = w_router.shape[1]
    assert lp % tm == 0
    row = lambda i: (i, 0)
    const = lambda i: (0, 0)
    return pl.pallas_call(
        functools.partial(_out_proj_kernel, lp=lp, n_pad=n_pad),
        grid=(m // tm,),
        in_specs=[
            pl.BlockSpec((tm, kr), row),
            pl.BlockSpec((tm, kr), row),
            pl.BlockSpec((tm, d), row),
            pl.BlockSpec((2 * kr, d), const),
            pl.BlockSpec((1, d), const),
            pl.BlockSpec((d, ne), const),
            pl.BlockSpec((1, ne), const),
        ],
        out_specs=[
            pl.BlockSpec((tm, d), row),
            pl.BlockSpec((tm, d // 2), row),
            pl.BlockSpec((tm, 128), row),
            pl.BlockSpec((1, ne), const),
        ],
        out_shape=[
            jax.ShapeDtypeStruct((m, d), F32),
            jax.ShapeDtypeStruct((m, d // 2), jnp.uint32),
            jax.ShapeDtypeStruct((m, 128), F32),
            jax.ShapeDtypeStruct((1, ne), F32),
        ],
        scratch_shapes=[pltpu.VMEM((1, ne), F32)],
        compiler_params=_compiler_params(("arbitrary",)),
        name="out_proj",
    )(ret2d, att2d, hp2d, w_bf16, gain.reshape(1, d), w_router, b_router.reshape(1, ne))


EXPERT_TILE = 1024
EXPERT_SUB = 256
EXPERT_UP_COLS = 256
EXPERT_DOWN_COLS = 512


def _expert_kernel(te_ref, nv_ref, q0_ref, nt_ref, rows_ref,
                   x_hbm, wg_ref, wu_ref, bg_ref, bu_ref, wd_ref, bd_ref, o_ref,
                   xbuf, xb_ref, act_ref, wgu_bf, wd_bf, sem, *, tm, sub, n_up, n_down):
    i = pl.program_id(0)
    j = pl.program_id(1)
    n_used = nt_ref[0]
    used = i < n_used
    nv = nv_ref[i]
    slot = i % 2
    half = xbuf.shape[2]
    rows_per_step = tm // n_up
    up_phase = j < n_up

    def row_copy(q, r, s):
        return pltpu.make_async_copy(x_hbm.at[rows_ref[q]], xbuf.at[s, r], sem.at[s])

    def gather_next_rows():
        r0 = pl.multiple_of(j * rows_per_step, rows_per_step)
        q = q0_ref[i + 1] + r0
        for r in range(rows_per_step):
            row_copy(q + r, r0 + r, 1 - slot).start()

    @pl.when(jnp.logical_and(i == 0, j == 0))
    def _():
        def body(r, carry):
            row_copy(q0_ref[0] + r, r, 0).start()
            return carry
        lax.fori_loop(0, tm, body, 0)

    @pl.when(jnp.logical_and(i <= n_used, j == 0))
    def _():
        pltpu.make_async_copy(xbuf.at[slot], xbuf.at[slot], sem.at[slot]).wait()

    @pl.when(jnp.logical_and(used, j == 0))
    def _():
        for s in range(tm // sub):
            r = slice(s * sub, (s + 1) * sub)
            lo, hi = _unpack_bf16_pairs(xbuf[slot, r, :])
            xb_ref[r, :half] = lo
            xb_ref[r, half:] = hi

    def activation(x, wg, wu):
        gate = jnp.dot(x, wg, preferred_element_type=F32) + bg_ref[0]
        up = jnp.dot(x, wu, preferred_element_type=F32) + bu_ref[0]
        gate = jnp.minimum(gate, SWIGLU_LIMIT)
        up = jnp.clip(up, -SWIGLU_LIMIT, SWIGLU_LIMIT)
        glu = gate * jax.nn.sigmoid(SWIGLU_ALPHA * gate)
        return ((up + 1.0) * glu).astype(BF16)

    def down(rows, wd):
        act = jnp.concatenate([act_ref[c, rows, :] for c in range(n_up)], axis=1)
        return jnp.dot(act, wd, preferred_element_type=F32) + bd_ref[0]

    whole = jnp.logical_and(used, nv > tm - sub)
    partial = jnp.logical_and(used, nv <= tm - sub)

    @pl.when(jnp.logical_and(whole, up_phase))
    def _():
        gather_next_rows()
        wg = wg_ref[0].astype(BF16)
        wu = wu_ref[0].astype(BF16)
        for part in range(2):
            rows = slice(part * tm // 2, (part + 1) * tm // 2)
            act_ref[j, rows, :] = activation(xb_ref[rows, :], wg, wu)

    @pl.when(jnp.logical_and(whole, jnp.logical_not(up_phase)))
    def _():
        o_ref[...] = down(slice(None), wd_ref[0].astype(BF16))

    @pl.when(jnp.logical_and(partial, up_phase))
    def _():
        wgu_bf[0] = wg_ref[0].astype(BF16)
        wgu_bf[1] = wu_ref[0].astype(BF16)

    @pl.when(jnp.logical_and(partial, jnp.logical_not(up_phase)))
    def _():
        wd_bf[...] = wd_ref[0].astype(BF16)

    for s in range(tm // sub):
        r = slice(s * sub, (s + 1) * sub)
        live = jnp.logical_and(partial, s * sub < nv)
        dead = jnp.logical_not(jnp.logical_or(whole, live))

        @pl.when(jnp.logical_and(live, up_phase))
        def _():
            if s == 0:
                gather_next_rows()
            act_ref[j, r, :] = activation(xb_ref[r, :], wgu_bf[0], wgu_bf[1])

        @pl.when(jnp.logical_and(live, jnp.logical_not(up_phase)))
        def _():
            o_ref[r, :] = down(r, wd_bf[...])

        @pl.when(jnp.logical_and(dead, jnp.logical_not(up_phase)))
        def _():
            o_ref[r, :] = jnp.zeros((sub, o_ref.shape[1]), F32)


def _experts(tile_expert, tile_valid, tile_q0, n_used, sorted_rows, x_packed,
             w_gate_up, b_gate_up, w_down, b_down, *, n_tiles):
    tm, sub = EXPERT_TILE, EXPERT_SUB
    half = x_packed.shape[1]
    ne, d, f2 = w_gate_up.shape
    f = f2 // 2
    assert d == 2 * half
    tf, tn = EXPERT_UP_COLS, EXPERT_DOWN_COLS
    n_up, n_down = f // tf, d // tn
    assert tm % n_up == 0 and tm % sub == 0

    def up_chunk(i, j, nt):
        return jnp.where(i < nt[0], jnp.minimum(j, n_up - 1), n_up - 1)

    def down_chunk(i, j, nt):
        return jnp.where(i < nt[0], jnp.maximum(j - n_up, 0), 0)

    b_gu = b_gate_up.reshape(ne, 1, f2)
    b_d = b_down.reshape(ne, 1, d)
    return pl.pallas_call(
        functools.partial(_expert_kernel, tm=tm, sub=sub, n_up=n_up, n_down=n_down),
        grid_spec=pltpu.PrefetchScalarGridSpec(
            num_scalar_prefetch=5,
            grid=(n_tiles, n_up + n_down),
            in_specs=[
                pl.BlockSpec(memory_space=pl.ANY),
                pl.BlockSpec((1, d, tf), lambda i, j, te, nv, q0, nt, rows: (te[i], 0, up_chunk(i, j, nt))),
                pl.BlockSpec((1, d, tf), lambda i, j, te, nv, q0, nt, rows: (te[i], 0, n_up + up_chunk(i, j, nt))),
                pl.BlockSpec((1, 1, tf), lambda i, j, te, nv, q0, nt, rows: (te[i], 0, up_chunk(i, j, nt))),
                pl.BlockSpec((1, 1, tf), lambda i, j, te, nv, q0, nt, rows: (te[i], 0, n_up + up_chunk(i, j, nt))),
                pl.BlockSpec((1, f, tn), lambda i, j, te, nv, q0, nt, rows: (te[i], 0, down_chunk(i, j, nt))),
                pl.BlockSpec((1, 1, tn), lambda i, j, te, nv, q0, nt, rows: (te[i], 0, down_chunk(i, j, nt))),
            ],
            out_specs=pl.BlockSpec((tm, tn), lambda i, j, te, nv, q0, nt, rows: (i, jnp.maximum(j - n_up, 0))),
            scratch_shapes=[
                pltpu.VMEM((2, tm, half), jnp.uint32),
                pltpu.VMEM((tm, d), BF16),
                pltpu.VMEM((n_up, tm, tf), BF16),
                pltpu.VMEM((2, d, tf), BF16),
                pltpu.VMEM((f, tn), BF16),
                pltpu.SemaphoreType.DMA((2,)),
            ],
        ),
        out_shape=jax.ShapeDtypeStruct((n_tiles * tm, d), F32),
        compiler_params=_compiler_params(("arbitrary", "arbitrary")),
        name="experts",
    )(tile_expert, tile_valid, tile_q0, n_used, sorted_rows, x_packed,
      w_gate_up, w_gate_up, b_gu, b_gu, w_down, b_d)


def _combine_kernel(dest_ref, h1_ref, gate_ref, y_hbm, o_ref, buf, sem, *, tt, tiles_per_batch,
                    tok_per_batch):
    bi = pl.program_id(0)
    ti = pl.program_id(1)
    base = (bi * tok_per_batch + N_META + ti * tt) * TOP_K

    def copy(r, k):
        return pltpu.make_async_copy(y_hbm.at[dest_ref[base + r * TOP_K + k]], buf.at[k, r], sem)

    def start(r, carry):
        for k in range(TOP_K):
            copy(r, k).start()
        return carry

    def wait(r, carry):
        for k in range(TOP_K):
            copy(r, k).wait()
        return carry

    lax.fori_loop(0, tt, start, 0)
    lax.fori_loop(0, tt, wait, 0)
    g = gate_ref[0]
    acc = h1_ref[0]
    moe = buf[0] * g[:, 0:1]
    for k in range(1, TOP_K):
        moe = moe + buf[k] * g[:, k:k + 1]
    o_ref[0] = acc + moe


def _combine(dest_flat, h1, gates, y_sorted, *, tt):
    b, lp, d = h1.shape
    s = lp - CHUNK
    tok_per_batch = s + N_META
    tiles = s // tt
    off = CHUNK // tt
    gates_tok = gates[:, N_META:, :]
    return pl.pallas_call(
        functools.partial(_combine_kernel, tt=tt, tiles_per_batch=tiles, tok_per_batch=tok_per_batch),
        grid_spec=pltpu.PrefetchScalarGridSpec(
            num_scalar_prefetch=1,
            grid=(b, tiles),
            in_specs=[
                pl.BlockSpec((1, tt, d), lambda i, t, dest: (i, off + t, 0)),
                pl.BlockSpec((1, tt, TOP_K), lambda i, t, dest: (i, t, 0)),
                pl.BlockSpec(memory_space=pl.ANY),
            ],
            out_specs=pl.BlockSpec((1, tt, d), lambda i, t, dest: (i, t, 0)),
            scratch_shapes=[pltpu.VMEM((TOP_K, tt, d), F32), pltpu.SemaphoreType.DMA],
        ),
        out_shape=jax.ShapeDtypeStruct((b, s, d), F32),
        compiler_params=_compiler_params(("arbitrary", "arbitrary")),
        name="combine",
    )(dest_flat, h1, gates_tok, y_sorted)


def _lookup(table, idx):
    n = table.shape[0]
    hit = idx[..., None] == jnp.arange(n, dtype=idx.dtype)
    return jnp.sum(jnp.where(hit, table, 0), axis=-1)


def _routing(top_idx, rank, counts, lp, n_pad, *, tm):
    b, l, k = top_idx.shape
    ne = counts.shape[0]
    p = b * l * k
    n_tiles = p // tm + ne + 1
    tiles_per = (counts + tm - 1) // tm
    tile_end = jnp.cumsum(tiles_per)
    tile_start = tile_end - tiles_per
    n_used = tile_end[-1]
    starts = jnp.cumsum(counts) - counts
    tile_id = jnp.arange(n_tiles, dtype=jnp.int32)
    tile_expert = jnp.minimum(jnp.sum(tile_end[None, :] <= tile_id[:, None], axis=-1), ne - 1).astype(jnp.int32)
    within = tile_id - _lookup(tile_start, tile_expert)
    tile_valid = jnp.where(tile_id < n_used, jnp.clip(_lookup(counts, tile_expert) - within * tm, 0, tm), 0)
    tile_q0 = jnp.where(tile_id < n_used, _lookup(starts, tile_expert) + within * tm, 0)
    tok = jnp.arange(b * l, dtype=jnp.int32)
    tok_bits = max(1, (b * l - 1).bit_length())
    keys = jnp.sort((top_idx.reshape(b * l, k) << tok_bits | tok[:, None]).reshape(p))
    sorted_tok = keys & ((1 << tok_bits) - 1)
    sorted_rows = (sorted_tok // l) * lp + n_pad + sorted_tok % l
    sorted_rows = jnp.concatenate([sorted_rows, jnp.arange(tm, dtype=jnp.int32) % (b * lp)])
    dest = _lookup(tile_start * tm, top_idx) + rank
    return (tile_expert, tile_valid.astype(jnp.int32), tile_q0.astype(jnp.int32),
            n_used.astype(jnp.int32).reshape(1), sorted_rows.astype(jnp.int32),
            dest.astype(jnp.int32).reshape(p), n_tiles)


def kernel(x, meta_tokens, attn_norm_gain, w_in, ret_decay_fwd, ret_decay_bwd, ret_out_gain,
           diff_q_gain, diff_k_gain, diff_lambda_q1, diff_lambda_k1, diff_lambda_q2, diff_lambda_k2,
           diff_out_gain, w_out, ffn_norm_gain, w_router, b_router, w_gate_up, b_gate_up,
           w_down, b_down):
    b, s, d = x.shape
    assert s % CHUNK == 0 and w_in.shape[0] == 1, "single layer, token count a chunk multiple"
    n_pad = CHUNK - N_META
    lp = s + CHUNK
    l = s + N_META

    hp, u = _prep(x, meta_tokens.astype(x.dtype), attn_norm_gain[0])
    z = _in_proj(u.reshape(b * lp, d), w_in[0].astype(BF16), tm=lp // 4, tn=1024)
    z = z.reshape(b, lp, -1)

    lg_f = jnp.log1p(-jnp.exp2(-ret_decay_fwd[0].astype(F32)))
    lg_b = jnp.log1p(-jnp.exp2(-ret_decay_bwd[0].astype(F32)))
    ret = _retention(z, lg_f, lg_b, ret_out_gain[0])

    lam = (jnp.exp(jnp.sum(diff_lambda_q1[0].astype(F32) * diff_lambda_k1[0].astype(F32)))
           - jnp.exp(jnp.sum(diff_lambda_q2[0].astype(F32) * diff_lambda_k2[0].astype(F32)))
           + LAMBDA_INIT).reshape(1)
    slopes = jnp.exp2(-8.0 * jnp.arange(1, DIFF_HEADS + 1, dtype=F32) / DIFF_HEADS)
    att = _attention(z, lam, slopes, diff_q_gain[0], diff_k_gain[0], diff_out_gain[0], tq=384, tk=384)

    h1, xt, route, counts = _out_proj(
        ret.reshape(b * lp, -1), att.reshape(b * lp, -1), hp.reshape(b * lp, d),
        w_out[0].astype(BF16), ffn_norm_gain[0], w_router[0], b_router[0], tm=lp // 8, lp=lp, n_pad=n_pad)

    route = route.reshape(b, lp, 128)[:, n_pad:, :]
    top_idx = route[:, :, :TOP_K].astype(jnp.int32)
    gates = route[:, :, TOP_K:2 * TOP_K]
    rank = route[:, :, 2 * TOP_K:3 * TOP_K].astype(jnp.int32)

    tile_expert, tile_valid, tile_q0, n_used, sorted_rows, dest, n_tiles = _routing(
        top_idx, rank, counts.reshape(-1).astype(jnp.int32), lp, n_pad, tm=EXPERT_TILE)
    ys = _experts(tile_expert, tile_valid, tile_q0, n_used, sorted_rows, xt,
                  w_gate_up[0], b_gate_up[0], w_down[0], b_down[0], n_tiles=n_tiles)
    return _combine(dest, h1.reshape(b, lp, d), gates, ys, tt=CHUNK)
```

```python
import functools
import math

import jax
import jax.numpy as jnp
from jax import lax
from jax.experimental import pallas as pl
from jax.experimental.pallas import tpu as pltpu

N_META = 16
CHUNK = 128
RET_HEADS = 4
RET_QK_DIM = 128
RET_V_DIM = 256
DIFF_HEADS = 4
DIFF_QK_DIM = 128
DIFF_V_DIM = 256
TOP_K = 4
SWIGLU_ALPHA = 1.702
SWIGLU_LIMIT = 7.0
NORM_EPS = 1e-6
RET_DECAY_BASE = 5.0
LAMBDA_INIT = 0.8 - 0.6 * math.exp(-0.3 * 0)
NEG_BIG = -1e30

V7X_VMEM_BYTES = 64 * 1024 * 1024
VMEM_LIMIT_BYTES = V7X_VMEM_BYTES - 8 * 1024 * 1024

F32 = jnp.float32
BF16 = jnp.bfloat16


def _compiler_params(semantics):
    return pltpu.CompilerParams(dimension_semantics=semantics, vmem_limit_bytes=VMEM_LIMIT_BYTES)


def _prep_kernel(x_ref, meta_ref, g_ref, hp_ref, u_ref, *, n_pad):
    c = pl.program_id(1)

    def emit(h):
        hp_ref[0] = h
        ms = jnp.mean(h * h, axis=-1, keepdims=True)
        u_ref[0] = (h * lax.rsqrt(ms + NORM_EPS) * g_ref[...]).astype(u_ref.dtype)

    @pl.when(c == 0)
    def _():
        d = meta_ref.shape[-1]
        emit(jnp.concatenate([jnp.zeros((n_pad, d), F32), meta_ref[...]], axis=0))

    @pl.when(c > 0)
    def _():
        emit(x_ref[0])


def _prep(x, meta, gain):
    b, s, d = x.shape
    n_pad = CHUNK - N_META
    nc = s // CHUNK + 1
    lp = nc * CHUNK
    return pl.pallas_call(
        functools.partial(_prep_kernel, n_pad=n_pad),
        grid=(b, nc),
        in_specs=[
            pl.BlockSpec((1, CHUNK, d), lambda i, c: (i, jnp.maximum(c - 1, 0), 0)),
            pl.BlockSpec((N_META, d), lambda i, c: (0, 0)),
            pl.BlockSpec((1, d), lambda i, c: (0, 0)),
        ],
        out_specs=[
            pl.BlockSpec((1, CHUNK, d), lambda i, c: (i, c, 0)),
            pl.BlockSpec((1, CHUNK, d), lambda i, c: (i, c, 0)),
        ],
        out_shape=[
            jax.ShapeDtypeStruct((b, lp, d), F32),
            jax.ShapeDtypeStruct((b, lp, d), BF16),
        ],
        compiler_params=_compiler_params(("parallel", "arbitrary")),
        name="prep",
    )(x, meta, gain.reshape(1, d))


def _matmul_kernel(a_ref, b_ref, o_ref):
    o_ref[...] = jnp.dot(a_ref[...], b_ref[...], preferred_element_type=F32).astype(o_ref.dtype)


def _in_proj(u2d, w_bf16, *, tm, tn):
    m, k = u2d.shape
    n = w_bf16.shape[1]
    return pl.pallas_call(
        _matmul_kernel,
        grid=(m // tm, n // tn),
        in_specs=[
            pl.BlockSpec((tm, k), lambda i, j: (i, 0)),
            pl.BlockSpec((k, tn), lambda i, j: (0, j)),
        ],
        out_specs=pl.BlockSpec((tm, tn), lambda i, j: (i, j)),
        out_shape=jax.ShapeDtypeStruct((m, n), BF16),
        compiler_params=_compiler_params(("parallel", "arbitrary")),
        name="in_proj",
    )(u2d, w_bf16)


def _retention_kernel(lgf_ref, lgb_ref, q_ref, k_ref, v_ref, g_ref, gain_ref, o_ref,
                      ob_ref, sf_ref, sb_ref, dec_ref, *, nc, heads):
    c = CHUNK
    dk, dv = RET_QK_DIM, RET_V_DIM
    scale = dk ** -0.5
    row = lax.broadcasted_iota(jnp.int32, (c, c), 0)
    col = lax.broadcasted_iota(jnp.int32, (c, c), 1)
    rel = (row - col).astype(F32)
    pos = lax.broadcasted_iota(jnp.int32, (c, 1), 0).astype(F32)
    q_dec_f, k_dec_f, q_dec_b, k_dec_b, chunk_f, chunk_b = [], [], [], [], [], []
    for hh in range(heads):
        lgf = lgf_ref[pl.program_id(1) * heads + hh]
        lgb = lgb_ref[pl.program_id(1) * heads + hh]
        dec_ref[hh] = jnp.where(rel >= 0, jnp.exp(lgf * jnp.maximum(rel, 0.0)),
                                jnp.exp(lgb * jnp.maximum(-rel, 0.0))) * scale
        q_dec_f.append(jnp.exp(lgf * (pos + 1.0)))
        k_dec_f.append(jnp.exp(lgf * (c - 1.0 - pos)) * scale)
        q_dec_b.append(jnp.exp(lgb * (c - pos)))
        k_dec_b.append(jnp.exp(lgb * pos) * scale)
        chunk_f.append(jnp.exp(lgf * c))
        chunk_b.append(jnp.exp(lgb * c))

    def rows(n):
        return pl.ds(pl.multiple_of(n * c, c), c)

    def qk_cols(hh):
        return slice(hh * dk, (hh + 1) * dk)

    def v_cols(hh):
        return slice(hh * dv, (hh + 1) * dv)

    def state_update(s_ref, hh, kn, vn, k_dec, chunk_decay):
        kd_t = (kn.astype(F32) * k_dec).T.astype(BF16)
        s_ref[hh] = s_ref[hh] * chunk_decay + jnp.dot(kd_t, vn, preferred_element_type=F32)

    sb_ref[...] = jnp.zeros_like(sb_ref)
    sf_ref[...] = jnp.zeros_like(sf_ref)

    def bwd_body(t, carry):
        n = nc - 1 - t
        for hh in range(heads):
            qn = q_ref[0, rows(n), qk_cols(hh)]
            kn = k_ref[0, rows(n), qk_cols(hh)]
            vn = v_ref[0, rows(n), v_cols(hh)]
            qd = (qn.astype(F32) * q_dec_b[hh]).astype(BF16)
            ob_ref[rows(n), v_cols(hh)] = jnp.dot(qd, sb_ref[hh].astype(BF16), preferred_element_type=F32)
            state_update(sb_ref, hh, kn, vn, k_dec_b[hh], chunk_b[hh])
        return carry

    lax.fori_loop(0, nc, bwd_body, 0)

    def fwd_body(n, carry):
        for hh in range(heads):
            qn = q_ref[0, rows(n), qk_cols(hh)]
            kn = k_ref[0, rows(n), qk_cols(hh)]
            vn = v_ref[0, rows(n), v_cols(hh)]
            sc = lax.dot_general(qn, kn, (((1,), (1,)), ((), ())), preferred_element_type=F32)
            inner = jnp.dot((sc * dec_ref[hh]).astype(BF16), vn, preferred_element_type=F32)
            qd = (qn.astype(F32) * q_dec_f[hh]).astype(BF16)
            cross = jnp.dot(qd, sf_ref[hh].astype(BF16), preferred_element_type=F32)
            o = inner + cross + ob_ref[rows(n), v_cols(hh)]
            state_update(sf_ref, hh, kn, vn, k_dec_f[hh], chunk_f[hh])
            o = o * lax.rsqrt(jnp.mean(o * o, axis=-1, keepdims=True) + NORM_EPS) * gain_ref[:, v_cols(hh)]
            gate = g_ref[0, rows(n), v_cols(hh)].astype(F32)
            o_ref[0, rows(n), v_cols(hh)] = (o * (gate * jax.nn.sigmoid(gate))).astype(o_ref.dtype)
        return carry

    lax.fori_loop(0, nc, fwd_body, 0)


def _retention(z, lg_f, lg_b, ret_gain):
    b, lp, _ = z.shape
    nc = lp // CHUNK
    dk, dv, nh = RET_QK_DIM, RET_V_DIM, RET_HEADS
    heads = 2
    groups = nh // heads
    k_blk0 = groups
    v_blk0 = 2 * nh * dk // (heads * dv)
    g_blk0 = v_blk0 + groups
    smem = pl.BlockSpec(memory_space=pltpu.SMEM)
    return pl.pallas_call(
        functools.partial(_retention_kernel, nc=nc, heads=heads),
        grid=(b, groups),
        in_specs=[
            smem, smem,
            pl.BlockSpec((1, lp, heads * dk), lambda i, h: (i, 0, h)),
            pl.BlockSpec((1, lp, heads * dk), lambda i, h: (i, 0, k_blk0 + h)),
            pl.BlockSpec((1, lp, heads * dv), lambda i, h: (i, 0, v_blk0 + h)),
            pl.BlockSpec((1, lp, heads * dv), lambda i, h: (i, 0, g_blk0 + h)),
            pl.BlockSpec((1, heads * dv), lambda i, h: (0, h)),
        ],
        out_specs=pl.BlockSpec((1, lp, heads * dv), lambda i, h: (i, 0, h)),
        out_shape=jax.ShapeDtypeStruct((b, lp, nh * dv), BF16),
        scratch_shapes=[
            pltpu.VMEM((lp, heads * dv), F32),
            pltpu.VMEM((heads, dk, dv), F32),
            pltpu.VMEM((heads, dk, dv), F32),
            pltpu.VMEM((heads, CHUNK, CHUNK), F32),
        ],
        compiler_params=_compiler_params(("parallel", "arbitrary")),
        name="retention",
    )(lg_f, lg_b, z, z, z, z, ret_gain.reshape(1, nh * dv))


def _attn_kernel(lam_ref, slope_ref, q_ref, k_ref, v_ref, qg_ref, kg_ref, og_ref, o_ref,
                 kaug_ref, corr_ref, qaug_ref, s_ref, p_ref, *, tq, tk, n_pad):
    h = pl.program_id(1)
    qi = pl.program_id(2)
    lp = k_ref.shape[1]
    d = DIFF_QK_DIM
    n_chunks = lp // tk
    lanes = 128
    lam = lam_ref[0]
    log2e = math.log2(math.e)
    slope = slope_ref[h] * log2e
    contract_last = (((1,), (1,)), ((), ()))

    def qk_norm(x, gain):
        return x * lax.rsqrt(jnp.mean(x * x, axis=-1, keepdims=True) + NORM_EPS) * gain

    def split3(v):
        hi = v.astype(BF16).astype(F32)
        mid = (v - hi).astype(BF16).astype(F32)
        lo = (v - hi - mid).astype(BF16).astype(F32)
        return [hi, mid, lo]

    def extra_lanes(cols):
        rows = cols[0].shape[0]
        lane = lax.broadcasted_iota(jnp.int32, (rows, d), 1)
        out = jnp.zeros((rows, d), F32)
        for t, col in enumerate(cols):
            out = jnp.where(lane == t, col, out)
        return out.astype(BF16)

    @pl.when(qi == 0)
    def _():
        for c in range(n_chunks):
            r = slice(c * tk, (c + 1) * tk)
            kk = k_ref[0, r, :].astype(F32)
            k_row = c * tk + lax.broadcasted_iota(jnp.int32, (tk, 1), 0)
            real = k_row >= CHUNK
            one = jnp.where(real, 1.0, 0.0)
            first = jnp.where(k_row < n_pad, NEG_BIG, 0.0)
            sk = [jnp.where(real, t, 0.0) for t in split3(slope * k_row.astype(F32))]
            extra = extra_lanes([sk[0] + first, sk[1], sk[2], one, one, one])
            for m in range(2):
                kaug_ref[m, r, :d] = qk_norm(kk[:, m * d:(m + 1) * d], kg_ref[...]).astype(BF16)
                kaug_ref[m, r, d:] = extra
        i_loc = lax.broadcasted_iota(jnp.int32, (tq, tk), 0)
        j_loc = lax.broadcasted_iota(jnp.int32, (tq, tk), 1)
        corr_ref[...] = (2.0 * slope) * jnp.maximum(j_loc - i_loc, 0).astype(F32)

    qq = q_ref[0].astype(F32)
    qs = [(qk_norm(qq[:, m * d:(m + 1) * d], qg_ref[...]) * (d ** -0.5 * log2e)).astype(BF16)
          for m in range(2)]
    q_row = qi * tq + lax.broadcasted_iota(jnp.int32, (tq, 1), 0)
    real_q = q_row >= CHUNK
    sq = [jnp.where(real_q, t, 0.0) for t in split3(slope * q_row.astype(F32))]
    one = jnp.where(real_q, 1.0, 0.0)
    extra_before = extra_lanes([one, one, one] + [-t for t in sq])
    extra_after = extra_lanes([-one, -one, -one] + sq)
    for m in range(2):
        qaug_ref[0, m] = jnp.concatenate([qs[m], extra_before], axis=1)
        qaug_ref[1, m] = jnp.concatenate([qs[m], extra_after], axis=1)

    def lane_fold(x, op):
        out = x[:, :lanes]
        for t in range(1, x.shape[1] // lanes):
            out = op(out, x[:, t * lanes:(t + 1) * lanes])
        return out

    for c in range(n_chunks):
        r = slice(c * tk, (c + 1) * tk)
        side = jnp.where(c <= qi, 0, 1)
        for m in range(2):
            s_ref[m, c] = lax.dot_general(qaug_ref[side, m], kaug_ref[m, r, :], contract_last,
                                          preferred_element_type=F32)
    for m in range(2):
        s_ref[m, qi] = s_ref[m, qi] - corr_ref[...]

    @pl.when(qi == 0)
    def _():
        q_pos = lax.broadcasted_iota(jnp.int32, (tq, tk), 0)
        k_pos = lax.broadcasted_iota(jnp.int32, (tq, tk), 1)
        dist = jnp.abs(q_pos - k_pos).astype(F32)
        both_real = jnp.logical_and(q_pos >= CHUNK, k_pos >= CHUNK)
        bias = jnp.where(both_real, -slope * dist, 0.0)
        for m in range(2):
            s = lax.dot_general(qs[m], kaug_ref[m, :tk, :d], contract_last, preferred_element_type=F32)
            s_ref[m, 0] = jnp.where(k_pos >= n_pad, s + bias, NEG_BIG)

    outs = []
    band = lanes
    for m in range(2):
        rows_out = []
        for rb in range(tq // band):
            rows = slice(rb * band, (rb + 1) * band)
            m_part = lane_fold(s_ref[m, 0, rows, :], jnp.maximum)
            for c in range(1, n_chunks):
                m_part = jnp.maximum(m_part, lane_fold(s_ref[m, c, rows, :], jnp.maximum))
            row_max = jnp.max(m_part, axis=-1, keepdims=True)
            shift = jnp.concatenate([jnp.broadcast_to(row_max, (band, lanes))] * (tk // lanes), axis=1)
            l_part = jnp.zeros((band, lanes), F32)
            for c in range(n_chunks):
                e = jnp.exp2(s_ref[m, c, rows, :] - shift)
                l_part = l_part + lane_fold(e, jnp.add)
                p_ref[m, rows, c * tk:(c + 1) * tk] = e.astype(BF16)
            acc = jnp.dot(p_ref[m, rows, :], v_ref[0], preferred_element_type=F32)
            rows_out.append(acc / jnp.sum(l_part, axis=-1, keepdims=True))
        outs.append(jnp.concatenate(rows_out, axis=0))

    o = outs[0] - lam * outs[1]
    o = o * lax.rsqrt(jnp.mean(o * o, axis=-1, keepdims=True) + NORM_EPS) * og_ref[...]
    o_ref[0] = (o * (1.0 - LAMBDA_INIT)).astype(o_ref.dtype)


def _attention(z, lam, slopes, q_gain, k_gain, out_gain, *, tq, tk):
    b, lp, _ = z.shape
    nh, d, dv = DIFF_HEADS, DIFF_QK_DIM, DIFF_V_DIM
    assert tq == tk and tq >= CHUNK and lp % tq == 0
    ret_cols = 2 * RET_HEADS * RET_QK_DIM + 2 * RET_HEADS * RET_V_DIM
    q_blk0 = ret_cols // (2 * d)
    k_blk0 = q_blk0 + nh
    v_blk0 = k_blk0 + nh
    smem = pl.BlockSpec(memory_space=pltpu.SMEM)
    return pl.pallas_call(
        functools.partial(_attn_kernel, tq=tq, tk=tk, n_pad=CHUNK - N_META),
        grid=(b, nh, lp // tq),
        in_specs=[
            smem, smem,
            pl.BlockSpec((1, tq, 2 * d), lambda i, h, q: (i, q, q_blk0 + h)),
            pl.BlockSpec((1, lp, 2 * d), lambda i, h, q: (i, 0, k_blk0 + h)),
            pl.BlockSpec((1, lp, dv), lambda i, h, q: (i, 0, v_blk0 + h)),
            pl.BlockSpec((1, d), lambda i, h, q: (0, 0)),
            pl.BlockSpec((1, d), lambda i, h, q: (0, 0)),
            pl.BlockSpec((1, dv), lambda i, h, q: (0, h)),
        ],
        out_specs=pl.BlockSpec((1, tq, dv), lambda i, h, q: (i, q, h)),
        out_shape=jax.ShapeDtypeStruct((b, lp, nh * dv), BF16),
        scratch_shapes=[
            pltpu.VMEM((2, lp, 2 * d), BF16),
            pltpu.VMEM((tq, tk), F32),
            pltpu.VMEM((2, 2, tq, 2 * d), BF16),
            pltpu.VMEM((2, lp // tk, tq, tk), F32),
            pltpu.VMEM((2, tq, lp), BF16),
        ],
        compiler_params=_compiler_params(("parallel", "parallel", "arbitrary")),
        name="attention",
    )(lam, slopes, z, z, z, q_gain.reshape(1, d), k_gain.reshape(1, d), out_gain.reshape(1, nh * dv))


def _pack_bf16_pairs(x):
    n = x.shape[1] // 2
    xb = x.astype(BF16).astype(F32)
    lo = lax.bitcast_convert_type(xb[:, :n], jnp.uint32)
    hi = lax.bitcast_convert_type(xb[:, n:], jnp.uint32)
    return (hi & jnp.uint32(0xFFFF0000)) | (lo >> 16)


def _unpack_bf16_pairs(p):
    lo = lax.bitcast_convert_type(p << 16, F32).astype(BF16)
    hi = lax.bitcast_convert_type(p & jnp.uint32(0xFFFF0000), F32).astype(BF16)
    return lo, hi


def _out_proj_kernel(ret_ref, att_ref, hp_ref, w_ref, g_ref, wr_ref, br_ref,
                     h1_ref, xt_ref, route_ref, cnt_ref, run_ref, *, lp, n_pad):
    i = pl.program_id(0)

    @pl.when(i == 0)
    def _():
        run_ref[...] = jnp.zeros_like(run_ref)

    kr = ret_ref.shape[1]
    y = jnp.dot(ret_ref[...], w_ref[:kr, :], preferred_element_type=F32)
    y = y + jnp.dot(att_ref[...], w_ref[kr:, :], preferred_element_type=F32)
    h1 = hp_ref[...] + y
    h1_ref[...] = h1
    xt = h1 * lax.rsqrt(jnp.mean(h1 * h1, axis=-1, keepdims=True) + NORM_EPS) * g_ref[...]
    xt_ref[...] = _pack_bf16_pairs(xt)
    xt_hi = xt.astype(BF16)
    xt_lo = (xt - xt_hi.astype(F32)).astype(BF16)
    wr = wr_ref[...]
    wr_hi = wr.astype(BF16)
    wr_lo = (wr - wr_hi.astype(F32)).astype(BF16)
    logits = (jnp.dot(xt_hi, wr_hi, preferred_element_type=F32)
              + jnp.dot(xt_lo, wr_hi, preferred_element_type=F32)
              + jnp.dot(xt_hi, wr_lo, preferred_element_type=F32)) + br_ref[...]
    tm, ne = logits.shape
    lane = lax.broadcasted_iota(jnp.int32, (tm, ne), 1).astype(F32)
    out_lane = lax.broadcasted_iota(jnp.int32, route_ref.shape, 1)
    vals, idxs = [], []
    work = logits
    for _ in range(TOP_K):
        top = jnp.max(work, axis=-1, keepdims=True)
        idx = jnp.min(jnp.where(work == top, lane, float(ne)), axis=-1, keepdims=True)
        work = jnp.where(lane == idx, -jnp.inf, work)
        vals.append(top)
        idxs.append(idx)
    exps = [jnp.exp(v - vals[0]) for v in vals]
    denom = exps[0] + exps[1] + exps[2] + exps[3]

    local_row = (i * tm) % lp + lax.broadcasted_iota(jnp.int32, (tm, 1), 0)
    member = jnp.zeros((tm, ne), F32)
    for k in range(TOP_K):
        member = member + jnp.where(lane == idxs[k], 1.0, 0.0)
    member = jnp.where(local_row >= n_pad, member, 0.0)
    earlier = (lax.broadcasted_iota(jnp.int32, (tm, tm), 0)
               > lax.broadcasted_iota(jnp.int32, (tm, tm), 1))
    before = run_ref[...] + jnp.dot(jnp.where(earlier, 1.0, 0.0).astype(BF16), member.astype(BF16),
                                    preferred_element_type=F32)
    run_ref[...] = run_ref[...] + jnp.sum(member, axis=0, keepdims=True)
    cnt_ref[...] = run_ref[...]

    slab = jnp.zeros(route_ref.shape, F32)
    for k in range(TOP_K):
        rank = jnp.sum(jnp.where(lane == idxs[k], before, 0.0), axis=-1, keepdims=True)
        slab = jnp.where(out_lane == k, idxs[k], slab)
        slab = jnp.where(out_lane == TOP_K + k, exps[k] / denom, slab)
        slab = jnp.where(out_lane == 2 * TOP_K + k, rank, slab)
    route_ref[...] = slab


def _out_proj(ret2d, att2d, hp2d, w_bf16, gain, w_router, b_router, *, tm, lp, n_pad):
    m, kr = ret2d.shape
    d = hp2d.shape[1]
    ne = w_router.shape[1]
    assert lp % tm == 0
    row = lambda i: (i, 0)
    const = lambda i: (0, 0)
    return pl.pallas_call(
        functools.partial(_out_proj_kernel, lp=lp, n_pad=n_pad),
        grid=(m // tm,),
        in_specs=[
            pl.BlockSpec((tm, kr), row),
            pl.BlockSpec((tm, kr), row),
            pl.BlockSpec((tm, d), row),
            pl.BlockSpec((2 * kr, d), const),
            pl.BlockSpec((1, d), const),
            pl.BlockSpec((d, ne), const),
            pl.BlockSpec((1, ne), const),
        ],
        out_specs=[
            pl.BlockSpec((tm, d), row),
            pl.BlockSpec((tm, d // 2), row),
            pl.BlockSpec((tm, 128), row),
            pl.BlockSpec((1, ne), const),
        ],
        out_shape=[
            jax.ShapeDtypeStruct((m, d), F32),
            jax.ShapeDtypeStruct((m, d // 2), jnp.uint32),
            jax.ShapeDtypeStruct((m, 128), F32),
            jax.ShapeDtypeStruct((1, ne), F32),
        ],
        scratch_shapes=[pltpu.VMEM((1, ne), F32)],
        compiler_params=_compiler_params(("arbitrary",)),
        name="out_proj",
    )(ret2d, att2d, hp2d, w_bf16, gain.reshape(1, d), w_router, b_router.reshape(1, ne))


EXPERT_TILE = 1024
EXPERT_SUB = 256
EXPERT_COLS = 256
EXPERT_WEIGHT_SLOTS = 3


def _expert_kernel(te_ref, nv_ref, q0_ref, nt_ref, rows_ref,
                   x_hbm, wgu_hbm, wd_hbm, bg_ref, bu_ref, bd_ref, o_ref,
                   xbuf, xb_ref, act_ref, wring, wgu_bf, wd_bf, sem, wsem, *, tm, sub, n_up, n_down):
    i = pl.program_id(0)
    j = pl.program_id(1)
    n_used = nt_ref[0]
    used = i < n_used
    nv = nv_ref[i]
    slot = i % 2
    half = xbuf.shape[2]
    tf = wring.shape[-1]
    f = wgu_hbm.shape[2] // 2
    steps = n_up + n_down
    n_slots = wring.shape[0]
    rows_per_step = tm // n_up
    up_phase = j < n_up
    wslot = (i * steps + j) % n_slots

    def weight_copy(src, s, piece):
        return pltpu.make_async_copy(src, wring.at[s, piece], wsem.at[s])

    def request_weights(ahead):
        wrap = j + ahead >= steps
        tile = jnp.where(wrap, i + 1, i)
        step = jnp.where(wrap, j + ahead - steps, j + ahead)
        s = (i * steps + j + ahead) % n_slots
        e = te_ref[tile]
        wanted = tile < n_used

        @pl.when(jnp.logical_and(wanted, step < n_up))
        def _():
            col = pl.multiple_of(step * tf, tf)
            weight_copy(wgu_hbm.at[e, :, pl.ds(col, tf)], s, 0).start()
            weight_copy(wgu_hbm.at[e, :, pl.ds(f + col, tf)], s, 1).start()

        @pl.when(jnp.logical_and(wanted, step >= n_up))
        def _():
            col = pl.multiple_of((step - n_up) * 2 * tf, 2 * tf)
            weight_copy(wd_hbm.at[e, :, pl.ds(col, tf)], s, 0).start()
            weight_copy(wd_hbm.at[e, :, pl.ds(col + tf, tf)], s, 1).start()

    @pl.when(jnp.logical_and(i == 0, j == 0))
    def _():
        for ahead in range(n_slots - 1):
            request_weights(ahead)

    @pl.when(used)
    def _():
        request_weights(n_slots - 1)
        for piece in range(2):
            weight_copy(wd_hbm.at[0, :, pl.ds(0, tf)], wslot, piece).wait()

    def row_copy(q, r, s):
        return pltpu.make_async_copy(x_hbm.at[rows_ref[q]], xbuf.at[s, r], sem.at[s])

    def gather_next_rows():
        r0 = pl.multiple_of(j * rows_per_step, rows_per_step)
        q = q0_ref[i + 1] + r0
        for r in range(rows_per_step):
            row_copy(q + r, r0 + r, 1 - slot).start()

    @pl.when(jnp.logical_and(i == 0, j == 0))
    def _():
        def body(r, carry):
            row_copy(q0_ref[0] + r, r, 0).start()
            return carry
        lax.fori_loop(0, tm, body, 0)

    @pl.when(jnp.logical_and(i <= n_used, j == 0))
    def _():
        pltpu.make_async_copy(xbuf.at[slot], xbuf.at[slot], sem.at[slot]).wait()

    @pl.when(jnp.logical_and(used, j == 0))
    def _():
        for s in range(tm // sub):
            r = slice(s * sub, (s + 1) * sub)
            lo, hi = _unpack_bf16_pairs(xbuf[slot, r, :])
            xb_ref[r, :half] = lo
            xb_ref[r, half:] = hi

    def activation(x, wg, wu):
        gate = jnp.dot(x, wg, preferred_element_type=F32) + bg_ref[0]
        up = jnp.dot(x, wu, preferred_element_type=F32) + bu_ref[0]
        gate = jnp.minimum(gate, SWIGLU_LIMIT)
        up = jnp.clip(up, -SWIGLU_LIMIT, SWIGLU_LIMIT)
        glu = gate * jax.nn.sigmoid(SWIGLU_ALPHA * gate)
        return ((up + 1.0) * glu).astype(BF16)

    def down(rows, wd, cols):
        act = jnp.concatenate([act_ref[c, rows, :] for c in range(n_up)], axis=1)
        return jnp.dot(act, wd, preferred_element_type=F32) + bd_ref[0, :, cols]

    whole = jnp.logical_and(used, nv > tm - sub)
    partial = jnp.logical_and(used, nv <= tm - sub)

    @pl.when(jnp.logical_and(whole, up_phase))
    def _():
        gather_next_rows()
        wg = wring[wslot, 0].astype(BF16)
        wu = wring[wslot, 1].astype(BF16)
        for part in range(2):
            rows = slice(part * tm // 2, (part + 1) * tm // 2)
            act_ref[j, rows, :] = activation(xb_ref[rows, :], wg, wu)

    @pl.when(jnp.logical_and(whole, jnp.logical_not(up_phase)))
    def _():
        for piece in range(2):
            cols = slice(piece * tf, (piece + 1) * tf)
            o_ref[:, cols] = down(slice(None), wring[wslot, piece].astype(BF16), cols)

    @pl.when(jnp.logical_and(partial, up_phase))
    def _():
        wgu_bf[0] = wring[wslot, 0].astype(BF16)
        wgu_bf[1] = wring[wslot, 1].astype(BF16)

    @pl.when(jnp.logical_and(partial, jnp.logical_not(up_phase)))
    def _():
        wd_bf[0] = wring[wslot, 0].astype(BF16)
        wd_bf[1] = wring[wslot, 1].astype(BF16)

    for s in range(tm // sub):
        r = slice(s * sub, (s + 1) * sub)
        live = jnp.logical_and(partial, s * sub < nv)
        dead = jnp.logical_not(jnp.logical_or(whole, live))

        @pl.when(jnp.logical_and(live, up_phase))
        def _():
            if s == 0:
                gather_next_rows()
            act_ref[j, r, :] = activation(xb_ref[r, :], wgu_bf[0], wgu_bf[1])

        @pl.when(jnp.logical_and(live, jnp.logical_not(up_phase)))
        def _():
            for piece in range(2):
                cols = slice(piece * tf, (piece + 1) * tf)
                o_ref[r, cols] = down(r, wd_bf[piece], cols)

        @pl.when(jnp.logical_and(dead, jnp.logical_not(up_phase)))
        def _():
            o_ref[r, :] = jnp.zeros((sub, o_ref.shape[1]), F32)


def _experts(tile_expert, tile_valid, tile_q0, n_used, sorted_rows, x_packed,
             w_gate_up, b_gate_up, w_down, b_down, *, n_tiles):
    tm, sub = EXPERT_TILE, EXPERT_SUB
    half = x_packed.shape[1]
    ne, d, f2 = w_gate_up.shape
    f = f2 // 2
    assert d == 2 * half
    tf = EXPERT_COLS
    tn = 2 * tf
    n_up, n_down = f // tf, d // tn
    assert tm % n_up == 0 and tm % sub == 0
    assert f == d

    def up_chunk(i, j, nt):
        return jnp.where(i < nt[0], jnp.minimum(j, n_up - 1), n_up - 1)

    def down_chunk(i, j, nt):
        return jnp.where(i < nt[0], jnp.maximum(j - n_up, 0), 0)

    b_gu = b_gate_up.reshape(ne, 1, f2)
    b_d = b_down.reshape(ne, 1, d)
    return pl.pallas_call(
        functools.partial(_expert_kernel, tm=tm, sub=sub, n_up=n_up, n_down=n_down),
        grid_spec=pltpu.PrefetchScalarGridSpec(
            num_scalar_prefetch=5,
            grid=(n_tiles, n_up + n_down),
            in_specs=[
                pl.BlockSpec(memory_space=pl.ANY),
                pl.BlockSpec(memory_space=pl.ANY),
                pl.BlockSpec(memory_space=pl.ANY),
                pl.BlockSpec((1, 1, tf), lambda i, j, te, nv, q0, nt, rows: (te[i], 0, up_chunk(i, j, nt))),
                pl.BlockSpec((1, 1, tf), lambda i, j, te, nv, q0, nt, rows: (te[i], 0, n_up + up_chunk(i, j, nt))),
                pl.BlockSpec((1, 1, tn), lambda i, j, te, nv, q0, nt, rows: (te[i], 0, down_chunk(i, j, nt))),
            ],
            out_specs=pl.BlockSpec((tm, tn), lambda i, j, te, nv, q0, nt, rows: (i, jnp.maximum(j - n_up, 0))),
            scratch_shapes=[
                pltpu.VMEM((2, tm, half), jnp.uint32),
                pltpu.VMEM((tm, d), BF16),
                pltpu.VMEM((n_up, tm, tf), BF16),
                pltpu.VMEM((EXPERT_WEIGHT_SLOTS, 2, d, tf), F32),
                pltpu.VMEM((2, d, tf), BF16),
                pltpu.VMEM((2, f, tf), BF16),
                pltpu.SemaphoreType.DMA((2,)),
                pltpu.SemaphoreType.DMA((EXPERT_WEIGHT_SLOTS,)),
            ],
        ),
        out_shape=jax.ShapeDtypeStruct((n_tiles * tm, d), F32),
        compiler_params=_compiler_params(("arbitrary", "arbitrary")),
        name="experts",
    )(tile_expert, tile_valid, tile_q0, n_used, sorted_rows, x_packed,
      w_gate_up, w_down, b_gu, b_gu, b_d)


def _combine_kernel(dest_ref, h1_ref, gate_ref, y_hbm, o_ref, buf, sem, *, tt, tiles_per_batch,
                    tok_per_batch, n_steps):
    t = pl.program_id(0)
    slot = t % 2

    def request_rows(step, s):
        base = ((step // tiles_per_batch) * tok_per_batch + N_META + (step % tiles_per_batch) * tt) * TOP_K
        for r in range(tt):
            for k in range(TOP_K):
                pltpu.make_async_copy(y_hbm.at[dest_ref[base + r * TOP_K + k]], buf.at[s, k, r],
                                      sem.at[s]).start()

    def wait_rows(s):
        pltpu.make_async_copy(buf.at[s], buf.at[s], sem.at[s]).wait()

    @pl.when(t == 0)
    def _():
        request_rows(0, 0)

    wait_rows(slot)
    request_rows(jnp.minimum(t + 1, n_steps - 1), 1 - slot)
    g = gate_ref[0]
    moe = buf[slot, 0] * g[:, 0:1]
    for k in range(1, TOP_K):
        moe = moe + buf[slot, k] * g[:, k:k + 1]
    o_ref[0] = h1_ref[0] + moe

    @pl.when(t == n_steps - 1)
    def _():
        wait_rows(1 - slot)


def _combine(dest_flat, h1, gates, y_sorted, *, tt):
    b, lp, d = h1.shape
    s = lp - CHUNK
    tok_per_batch = s + N_META
    tiles = s // tt
    off = CHUNK // tt
    gates_tok = gates[:, N_META:, :]
    n_steps = b * tiles
    return pl.pallas_call(
        functools.partial(_combine_kernel, tt=tt, tiles_per_batch=tiles, tok_per_batch=tok_per_batch,
                          n_steps=n_steps),
        grid_spec=pltpu.PrefetchScalarGridSpec(
            num_scalar_prefetch=1,
            grid=(n_steps,),
            in_specs=[
                pl.BlockSpec((1, tt, d), lambda t, dest: (t // tiles, off + t % tiles, 0)),
                pl.BlockSpec((1, tt, TOP_K), lambda t, dest: (t // tiles, t % tiles, 0)),
                pl.BlockSpec(memory_space=pl.ANY),
            ],
            out_specs=pl.BlockSpec((1, tt, d), lambda t, dest: (t // tiles, t % tiles, 0)),
            scratch_shapes=[pltpu.VMEM((2, TOP_K, tt, d), F32), pltpu.SemaphoreType.DMA((2,))],
        ),
        out_shape=jax.ShapeDtypeStruct((b, s, d), F32),
        compiler_params=_compiler_params(("arbitrary",)),
        name="combine",
    )(dest_flat, h1, gates_tok, y_sorted)


def _lookup(table, idx):
    n = table.shape[0]
    hit = idx[..., None] == jnp.arange(n, dtype=idx.dtype)
    return jnp.sum(jnp.where(hit, table, 0), axis=-1)


def _routing(top_idx, rank, counts, lp, n_pad, *, tm):
    b, l, k = top_idx.shape
    ne = counts.shape[0]
    p = b * l * k
    n_tiles = p // tm + ne + 1
    tiles_per = (counts + tm - 1) // tm
    tile_end = jnp.cumsum(tiles_per)
    tile_start = tile_end - tiles_per
    n_used = tile_end[-1]
    starts = jnp.cumsum(counts) - counts
    tile_id = jnp.arange(n_tiles, dtype=jnp.int32)
    tile_expert = jnp.minimum(jnp.sum(tile_end[None, :] <= tile_id[:, None], axis=-1), ne - 1).astype(jnp.int32)
    within = tile_id - _lookup(tile_start, tile_expert)
    tile_valid = jnp.where(tile_id < n_used, jnp.clip(_lookup(counts, tile_expert) - within * tm, 0, tm), 0)
    tile_q0 = jnp.where(tile_id < n_used, _lookup(starts, tile_expert) + within * tm, 0)
    tok = jnp.arange(b * l, dtype=jnp.int32)
    tok_bits = max(1, (b * l - 1).bit_length())
    keys = jnp.sort((top_idx.reshape(b * l, k) << tok_bits | tok[:, None]).reshape(p))
    sorted_tok = keys & ((1 << tok_bits) - 1)
    sorted_rows = (sorted_tok // l) * lp + n_pad + sorted_tok % l
    sorted_rows = jnp.concatenate([sorted_rows, jnp.arange(tm, dtype=jnp.int32) % (b * lp)])
    dest = _lookup(tile_start * tm, top_idx) + rank
    return (tile_expert, tile_valid.astype(jnp.int32), tile_q0.astype(jnp.int32),
            n_used.astype(jnp.int32).reshape(1), sorted_rows.astype(jnp.int32),
            dest.astype(jnp.int32).reshape(p), n_tiles)


def kernel(x, meta_tokens, attn_norm_gain, w_in, ret_decay_fwd, ret_decay_bwd, ret_out_gain,
           diff_q_gain, diff_k_gain, diff_lambda_q1, diff_lambda_k1, diff_lambda_q2, diff_lambda_k2,
           diff_out_gain, w_out, ffn_norm_gain, w_router, b_router, w_gate_up, b_gate_up,
           w_down, b_down):
    b, s, d = x.shape
    assert s % CHUNK == 0 and w_in.shape[0] == 1, "single layer, token count a chunk multiple"
    n_pad = CHUNK - N_META
    lp = s + CHUNK
    l = s + N_META

    hp, u = _prep(x, meta_tokens.astype(x.dtype), attn_norm_gain[0])
    z = _in_proj(u.reshape(b * lp, d), w_in[0].astype(BF16), tm=lp // 4, tn=1024)
    z = z.reshape(b, lp, -1)

    lg_f = jnp.log1p(-jnp.exp2(-ret_decay_fwd[0].astype(F32)))
    lg_b = jnp.log1p(-jnp.exp2(-ret_decay_bwd[0].astype(F32)))
    ret = _retention(z, lg_f, lg_b, ret_out_gain[0])

    lam = (jnp.exp(jnp.sum(diff_lambda_q1[0].astype(F32) * diff_lambda_k1[0].astype(F32)))
           - jnp.exp(jnp.sum(diff_lambda_q2[0].astype(F32) * diff_lambda_k2[0].astype(F32)))
           + LAMBDA_INIT).reshape(1)
    slopes = jnp.exp2(-8.0 * jnp.arange(1, DIFF_HEADS + 1, dtype=F32) / DIFF_HEADS)
    att = _attention(z, lam, slopes, diff_q_gain[0], diff_k_gain[0], diff_out_gain[0], tq=384, tk=384)

    h1, xt, route, counts = _out_proj(
        ret.reshape(b * lp, -1), att.reshape(b * lp, -1), hp.reshape(b * lp, d),
        w_out[0].astype(BF16), ffn_norm_gain[0], w_router[0], b_router[0], tm=lp // 8, lp=lp, n_pad=n_pad)

    route = route.reshape(b, lp, 128)[:, n_pad:, :]
    top_idx = route[:, :, :TOP_K].astype(jnp.int32)
    gates = route[:, :, TOP_K:2 * TOP_K]
    rank = route[:, :, 2 * TOP_K:3 * TOP_K].astype(jnp.int32)

    tile_expert, tile_valid, tile_q0, n_used, sorted_rows, dest, n_tiles = _routing(
        top_idx, rank, counts.reshape(-1).astype(jnp.int32), lp, n_pad, tm=EXPERT_TILE)
    ys = _experts(tile_expert, tile_valid, tile_q0, n_used, sorted_rows, xt,
                  w_gate_up[0], b_gate_up[0], w_down[0], b_down[0], n_tiles=n_tiles)
    return _combine(dest, h1.reshape(b, lp, d), gates, ys, tt=CHUNK)
```

```python
import functools
import math

import jax
import jax.numpy as jnp
from jax import lax
from jax.experimental import pallas as pl
from jax.experimental.pallas import tpu as pltpu

N_META = 16
CHUNK = 128
RET_HEADS = 4
RET_QK_DIM = 128
RET_V_DIM = 256
DIFF_HEADS = 4
DIFF_QK_DIM = 128
DIFF_V_DIM = 256
TOP_K = 4
SWIGLU_ALPHA = 1.702
SWIGLU_LIMIT = 7.0
NORM_EPS = 1e-6
RET_DECAY_BASE = 5.0
LAMBDA_INIT = 0.8 - 0.6 * math.exp(-0.3 * 0)
NEG_BIG = -1e30

V7X_VMEM_BYTES = 64 * 1024 * 1024
VMEM_LIMIT_BYTES = V7X_VMEM_BYTES - 8 * 1024 * 1024

F32 = jnp.float32
BF16 = jnp.bfloat16


def _compiler_params(semantics):
    return pltpu.CompilerParams(dimension_semantics=semantics, vmem_limit_bytes=VMEM_LIMIT_BYTES)


def _prep_kernel(x_ref, meta_ref, g_ref, hp_ref, u_ref, *, n_pad):
    c = pl.program_id(1)

    def emit(h):
        hp_ref[0] = h
        ms = jnp.mean(h * h, axis=-1, keepdims=True)
        u_ref[0] = (h * lax.rsqrt(ms + NORM_EPS) * g_ref[...]).astype(u_ref.dtype)

    @pl.when(c == 0)
    def _():
        d = meta_ref.shape[-1]
        emit(jnp.concatenate([jnp.zeros((n_pad, d), F32), meta_ref[...]], axis=0))

    @pl.when(c > 0)
    def _():
        emit(x_ref[0])


def _prep(x, meta, gain):
    b, s, d = x.shape
    n_pad = CHUNK - N_META
    nc = s // CHUNK + 1
    lp = nc * CHUNK
    return pl.pallas_call(
        functools.partial(_prep_kernel, n_pad=n_pad),
        grid=(b, nc),
        in_specs=[
            pl.BlockSpec((1, CHUNK, d), lambda i, c: (i, jnp.maximum(c - 1, 0), 0)),
            pl.BlockSpec((N_META, d), lambda i, c: (0, 0)),
            pl.BlockSpec((1, d), lambda i, c: (0, 0)),
        ],
        out_specs=[
            pl.BlockSpec((1, CHUNK, d), lambda i, c: (i, c, 0)),
            pl.BlockSpec((1, CHUNK, d), lambda i, c: (i, c, 0)),
        ],
        out_shape=[
            jax.ShapeDtypeStruct((b, lp, d), F32),
            jax.ShapeDtypeStruct((b, lp, d), BF16),
        ],
        compiler_params=_compiler_params(("parallel", "arbitrary")),
        name="prep",
    )(x, meta, gain.reshape(1, d))


def _matmul_kernel(a_ref, b_ref, o_ref):
    o_ref[...] = jnp.dot(a_ref[...], b_ref[...], preferred_element_type=F32).astype(o_ref.dtype)


def _in_proj(u2d, w_bf16, *, tm, tn):
    m, k = u2d.shape
    n = w_bf16.shape[1]
    return pl.pallas_call(
        _matmul_kernel,
        grid=(m // tm, n // tn),
        in_specs=[
            pl.BlockSpec((tm, k), lambda i, j: (i, 0)),
            pl.BlockSpec((k, tn), lambda i, j: (0, j)),
        ],
        out_specs=pl.BlockSpec((tm, tn), lambda i, j: (i, j)),
        out_shape=jax.ShapeDtypeStruct((m, n), BF16),
        compiler_params=_compiler_params(("parallel", "arbitrary")),
        name="in_proj",
    )(u2d, w_bf16)


def _retention_kernel(lgf_ref, lgb_ref, q_ref, k_ref, v_ref, g_ref, gain_ref, o_ref,
                      ob_ref, sf_ref, sb_ref, dec_ref, *, nc, heads):
    c = CHUNK
    dk, dv = RET_QK_DIM, RET_V_DIM
    scale = dk ** -0.5
    row = lax.broadcasted_iota(jnp.int32, (c, c), 0)
    col = lax.broadcasted_iota(jnp.int32, (c, c), 1)
    rel = (row - col).astype(F32)
    pos = lax.broadcasted_iota(jnp.int32, (c, 1), 0).astype(F32)
    q_dec_f, k_dec_f, q_dec_b, k_dec_b, chunk_f, chunk_b = [], [], [], [], [], []
    for hh in range(heads):
        lgf = lgf_ref[pl.program_id(1) * heads + hh]
        lgb = lgb_ref[pl.program_id(1) * heads + hh]
        dec_ref[hh] = jnp.where(rel >= 0, jnp.exp(lgf * jnp.maximum(rel, 0.0)),
                                jnp.exp(lgb * jnp.maximum(-rel, 0.0))) * scale
        q_dec_f.append(jnp.exp(lgf * (pos + 1.0)))
        k_dec_f.append(jnp.exp(lgf * (c - 1.0 - pos)) * scale)
        q_dec_b.append(jnp.exp(lgb * (c - pos)))
        k_dec_b.append(jnp.exp(lgb * pos) * scale)
        chunk_f.append(jnp.exp(lgf * c))
        chunk_b.append(jnp.exp(lgb * c))

    def rows(n):
        return pl.ds(pl.multiple_of(n * c, c), c)

    def qk_cols(hh):
        return slice(hh * dk, (hh + 1) * dk)

    def v_cols(hh):
        return slice(hh * dv, (hh + 1) * dv)

    def state_update(s_ref, hh, kn, vn, k_dec, chunk_decay):
        kd_t = (kn.astype(F32) * k_dec).T.astype(BF16)
        s_ref[hh] = s_ref[hh] * chunk_decay + jnp.dot(kd_t, vn, preferred_element_type=F32)

    sb_ref[...] = jnp.zeros_like(sb_ref)
    sf_ref[...] = jnp.zeros_like(sf_ref)

    def bwd_body(t, carry):
        n = nc - 1 - t
        for hh in range(heads):
            qn = q_ref[0, rows(n), qk_cols(hh)]
            kn = k_ref[0, rows(n), qk_cols(hh)]
            vn = v_ref[0, rows(n), v_cols(hh)]
            qd = (qn.astype(F32) * q_dec_b[hh]).astype(BF16)
            ob_ref[rows(n), v_cols(hh)] = jnp.dot(qd, sb_ref[hh].astype(BF16), preferred_element_type=F32)
            state_update(sb_ref, hh, kn, vn, k_dec_b[hh], chunk_b[hh])
        return carry

    lax.fori_loop(0, nc, bwd_body, 0)

    def fwd_body(n, carry):
        for hh in range(heads):
            qn = q_ref[0, rows(n), qk_cols(hh)]
            kn = k_ref[0, rows(n), qk_cols(hh)]
            vn = v_ref[0, rows(n), v_cols(hh)]
            sc = lax.dot_general(qn, kn, (((1,), (1,)), ((), ())), preferred_element_type=F32)
            inner = jnp.dot((sc * dec_ref[hh]).astype(BF16), vn, preferred_element_type=F32)
            qd = (qn.astype(F32) * q_dec_f[hh]).astype(BF16)
            cross = jnp.dot(qd, sf_ref[hh].astype(BF16), preferred_element_type=F32)
            o = inner + cross + ob_ref[rows(n), v_cols(hh)]
            state_update(sf_ref, hh, kn, vn, k_dec_f[hh], chunk_f[hh])
            o = o * lax.rsqrt(jnp.mean(o * o, axis=-1, keepdims=True) + NORM_EPS) * gain_ref[:, v_cols(hh)]
            gate = g_ref[0, rows(n), v_cols(hh)].astype(F32)
            o_ref[0, rows(n), v_cols(hh)] = (o * (gate * jax.nn.sigmoid(gate))).astype(o_ref.dtype)
        return carry

    lax.fori_loop(0, nc, fwd_body, 0)


def _retention(z, lg_f, lg_b, ret_gain):
    b, lp, _ = z.shape
    nc = lp // CHUNK
    dk, dv, nh = RET_QK_DIM, RET_V_DIM, RET_HEADS
    heads = 2
    groups = nh // heads
    k_blk0 = groups
    v_blk0 = 2 * nh * dk // (heads * dv)
    g_blk0 = v_blk0 + groups
    smem = pl.BlockSpec(memory_space=pltpu.SMEM)
    return pl.pallas_call(
        functools.partial(_retention_kernel, nc=nc, heads=heads),
        grid=(b, groups),
        in_specs=[
            smem, smem,
            pl.BlockSpec((1, lp, heads * dk), lambda i, h: (i, 0, h)),
            pl.BlockSpec((1, lp, heads * dk), lambda i, h: (i, 0, k_blk0 + h)),
            pl.BlockSpec((1, lp, heads * dv), lambda i, h: (i, 0, v_blk0 + h)),
            pl.BlockSpec((1, lp, heads * dv), lambda i, h: (i, 0, g_blk0 + h)),
            pl.BlockSpec((1, heads * dv), lambda i, h: (0, h)),
        ],
        out_specs=pl.BlockSpec((1, lp, heads * dv), lambda i, h: (i, 0, h)),
        out_shape=jax.ShapeDtypeStruct((b, lp, nh * dv), BF16),
        scratch_shapes=[
            pltpu.VMEM((lp, heads * dv), F32),
            pltpu.VMEM((heads, dk, dv), F32),
            pltpu.VMEM((heads, dk, dv), F32),
            pltpu.VMEM((heads, CHUNK, CHUNK), F32),
        ],
        compiler_params=_compiler_params(("parallel", "arbitrary")),
        name="retention",
    )(lg_f, lg_b, z, z, z, z, ret_gain.reshape(1, nh * dv))


def _attn_kernel(lam_ref, slope_ref, q_ref, k_ref, v_ref, qg_ref, kg_ref, og_ref, o_ref,
                 kaug_ref, corr_ref, qaug_ref, s_ref, p_ref, m_ref, *, tq, tk, n_pad):
    h = pl.program_id(1)
    qi = pl.program_id(2)
    lp = k_ref.shape[1]
    d = DIFF_QK_DIM
    n_chunks = lp // tk
    lanes = 128
    lam = lam_ref[0]
    log2e = math.log2(math.e)
    slope = slope_ref[h] * log2e
    contract_last = (((1,), (1,)), ((), ()))

    def qk_norm(x, gain):
        return x * lax.rsqrt(jnp.mean(x * x, axis=-1, keepdims=True) + NORM_EPS) * gain

    def split3(v):
        hi = v.astype(BF16).astype(F32)
        mid = (v - hi).astype(BF16).astype(F32)
        lo = (v - hi - mid).astype(BF16).astype(F32)
        return [hi, mid, lo]

    def extra_lanes(cols):
        rows = cols[0].shape[0]
        lane = lax.broadcasted_iota(jnp.int32, (rows, d), 1)
        out = jnp.zeros((rows, d), F32)
        for t, col in enumerate(cols):
            out = jnp.where(lane == t, col, out)
        return out.astype(BF16)

    @pl.when(qi == 0)
    def _():
        for c in range(n_chunks):
            r = slice(c * tk, (c + 1) * tk)
            kk = k_ref[0, r, :].astype(F32)
            k_row = c * tk + lax.broadcasted_iota(jnp.int32, (tk, 1), 0)
            real = k_row >= CHUNK
            one = jnp.where(real, 1.0, 0.0)
            first = jnp.where(k_row < n_pad, NEG_BIG, 0.0)
            sk = [jnp.where(real, t, 0.0) for t in split3(slope * k_row.astype(F32))]
            extra = extra_lanes([sk[0] + first, sk[1], sk[2], one, one, one])
            for m in range(2):
                kaug_ref[m, r, :d] = qk_norm(kk[:, m * d:(m + 1) * d], kg_ref[...]).astype(BF16)
                kaug_ref[m, r, d:] = extra
        i_loc = lax.broadcasted_iota(jnp.int32, (tq, tk), 0)
        j_loc = lax.broadcasted_iota(jnp.int32, (tq, tk), 1)
        corr_ref[...] = (2.0 * slope) * jnp.maximum(j_loc - i_loc, 0).astype(F32)

    qq = q_ref[0].astype(F32)
    qs = [(qk_norm(qq[:, m * d:(m + 1) * d], qg_ref[...]) * (d ** -0.5 * log2e)).astype(BF16)
          for m in range(2)]
    q_row = qi * tq + lax.broadcasted_iota(jnp.int32, (tq, 1), 0)
    real_q = q_row >= CHUNK
    sq = [jnp.where(real_q, t, 0.0) for t in split3(slope * q_row.astype(F32))]
    one = jnp.where(real_q, 1.0, 0.0)
    extra_before = extra_lanes([one, one, one] + [-t for t in sq])
    extra_after = extra_lanes([-one, -one, -one] + sq)
    for m in range(2):
        qaug_ref[0, m] = jnp.concatenate([qs[m], extra_before], axis=1)
        qaug_ref[1, m] = jnp.concatenate([qs[m], extra_after], axis=1)

    def lane_fold(x, op):
        out = x[:, :lanes]
        for t in range(1, x.shape[1] // lanes):
            out = op(out, x[:, t * lanes:(t + 1) * lanes])
        return out

    m_ref[...] = jnp.full_like(m_ref, NEG_BIG)
    for c in range(n_chunks):
        r = slice(c * tk, (c + 1) * tk)
        side = jnp.where(c <= qi, 0, 1)
        for m in range(2):
            s = lax.dot_general(qaug_ref[side, m], kaug_ref[m, r, :], contract_last,
                                preferred_element_type=F32)
            s_ref[m, c] = s
            m_ref[m] = jnp.maximum(m_ref[m], jnp.where(c == qi, NEG_BIG, lane_fold(s, jnp.maximum)))
    for m in range(2):
        s = s_ref[m, qi] - corr_ref[...]
        s_ref[m, qi] = s
        m_ref[m] = jnp.maximum(m_ref[m], lane_fold(s, jnp.maximum))

    @pl.when(qi == 0)
    def _():
        q_pos = lax.broadcasted_iota(jnp.int32, (tq, tk), 0)
        k_pos = lax.broadcasted_iota(jnp.int32, (tq, tk), 1)
        dist = jnp.abs(q_pos - k_pos).astype(F32)
        both_real = jnp.logical_and(q_pos >= CHUNK, k_pos >= CHUNK)
        bias = jnp.where(both_real, -slope * dist, 0.0)
        for m in range(2):
            s = lax.dot_general(qs[m], kaug_ref[m, :tk, :d], contract_last, preferred_element_type=F32)
            s = jnp.where(k_pos >= n_pad, s + bias, NEG_BIG)
            s_ref[m, 0] = s
            m_ref[m] = jnp.maximum(m_ref[m], lane_fold(s, jnp.maximum))

    band = lanes
    inv_l = []
    for m in range(2):
        for rb in range(tq // band):
            rows = slice(rb * band, (rb + 1) * band)
            row_max = jnp.max(m_ref[m, rows, :], axis=-1, keepdims=True)
            shift = jnp.concatenate([jnp.broadcast_to(row_max, (band, lanes))] * (tk // lanes), axis=1)
            l_part = jnp.zeros((band, lanes), F32)
            for c in range(n_chunks):
                e = jnp.exp2(s_ref[m, c, rows, :] - shift)
                l_part = l_part + lane_fold(e, jnp.add)
                p_ref[m * tq + rb * band:m * tq + (rb + 1) * band, c * tk:(c + 1) * tk] = e.astype(BF16)
            inv_l.append(1.0 / jnp.sum(l_part, axis=-1, keepdims=True))
    acc = jnp.dot(p_ref[...], v_ref[0], preferred_element_type=F32) * jnp.concatenate(inv_l, axis=0)

    o = acc[:tq] - lam * acc[tq:]
    o = o * lax.rsqrt(jnp.mean(o * o, axis=-1, keepdims=True) + NORM_EPS) * og_ref[...]
    o_ref[0] = (o * (1.0 - LAMBDA_INIT)).astype(o_ref.dtype)


def _attention(z, lam, slopes, q_gain, k_gain, out_gain, *, tq, tk):
    b, lp, _ = z.shape
    nh, d, dv = DIFF_HEADS, DIFF_QK_DIM, DIFF_V_DIM
    assert tq == tk and tq >= CHUNK and lp % tq == 0
    ret_cols = 2 * RET_HEADS * RET_QK_DIM + 2 * RET_HEADS * RET_V_DIM
    q_blk0 = ret_cols // (2 * d)
    k_blk0 = q_blk0 + nh
    v_blk0 = k_blk0 + nh
    smem = pl.BlockSpec(memory_space=pltpu.SMEM)
    return pl.pallas_call(
        functools.partial(_attn_kernel, tq=tq, tk=tk, n_pad=CHUNK - N_META),
        grid=(b, nh, lp // tq),
        in_specs=[
            smem, smem,
            pl.BlockSpec((1, tq, 2 * d), lambda i, h, q: (i, q, q_blk0 + h)),
            pl.BlockSpec((1, lp, 2 * d), lambda i, h, q: (i, 0, k_blk0 + h)),
            pl.BlockSpec((1, lp, dv), lambda i, h, q: (i, 0, v_blk0 + h)),
            pl.BlockSpec((1, d), lambda i, h, q: (0, 0)),
            pl.BlockSpec((1, d), lambda i, h, q: (0, 0)),
            pl.BlockSpec((1, dv), lambda i, h, q: (0, h)),
        ],
        out_specs=pl.BlockSpec((1, tq, dv), lambda i, h, q: (i, q, h)),
        out_shape=jax.ShapeDtypeStruct((b, lp, nh * dv), BF16),
        scratch_shapes=[
            pltpu.VMEM((2, lp, 2 * d), BF16),
            pltpu.VMEM((tq, tk), F32),
            pltpu.VMEM((2, 2, tq, 2 * d), BF16),
            pltpu.VMEM((2, lp // tk, tq, tk), F32),
            pltpu.VMEM((2 * tq, lp), BF16),
            pltpu.VMEM((2, tq, 128), F32),
        ],
        compiler_params=_compiler_params(("parallel", "parallel", "arbitrary")),
        name="attention",
    )(lam, slopes, z, z, z, q_gain.reshape(1, d), k_gain.reshape(1, d), out_gain.reshape(1, nh * dv))


def _pack_bf16_pairs(x):
    n = x.shape[1] // 2
    xb = x.astype(BF16).astype(F32)
    lo = lax.bitcast_convert_type(xb[:, :n], jnp.uint32)
    hi = lax.bitcast_convert_type(xb[:, n:], jnp.uint32)
    return (hi & jnp.uint32(0xFFFF0000)) | (lo >> 16)


def _unpack_bf16_pairs(p):
    lo = lax.bitcast_convert_type(p << 16, F32).astype(BF16)
    hi = lax.bitcast_convert_type(p & jnp.uint32(0xFFFF0000), F32).astype(BF16)
    return lo, hi


def _out_proj_kernel(ret_ref, att_ref, hp_ref, w_ref, g_ref, wr_ref, br_ref,
                     h1_ref, xt_ref, route_ref, cnt_ref, run_ref, *, lp, n_pad):
    i = pl.program_id(0)

    @pl.when(i == 0)
    def _():
        run_ref[...] = jnp.zeros_like(run_ref)

    kr = ret_ref.shape[1]
    y = jnp.dot(ret_ref[...], w_ref[:kr, :], preferred_element_type=F32)
    y = y + jnp.dot(att_ref[...], w_ref[kr:, :], preferred_element_type=F32)
    h1 = hp_ref[...] + y
    h1_ref[...] = h1
    xt = h1 * lax.rsqrt(jnp.mean(h1 * h1, axis=-1, keepdims=True) + NORM_EPS) * g_ref[...]
    xt_ref[...] = _pack_bf16_pairs(xt)
    xt_hi = xt.astype(BF16)
    xt_lo = (xt - xt_hi.astype(F32)).astype(BF16)
    wr = wr_ref[...]
    wr_hi = wr.astype(BF16)
    wr_lo = (wr - wr_hi.astype(F32)).astype(BF16)
    logits = (jnp.dot(xt_hi, wr_hi, preferred_element_type=F32)
              + jnp.dot(xt_lo, wr_hi, preferred_element_type=F32)
              + jnp.dot(xt_hi, wr_lo, preferred_element_type=F32)) + br_ref[...]
    tm, ne = logits.shape
    lane = lax.broadcasted_iota(jnp.int32, (tm, ne), 1).astype(F32)
    out_lane = lax.broadcasted_iota(jnp.int32, route_ref.shape, 1)
    vals, idxs = [], []
    work = logits
    for _ in range(TOP_K):
        top = jnp.max(work, axis=-1, keepdims=True)
        idx = jnp.min(jnp.where(work == top, lane, float(ne)), axis=-1, keepdims=True)
        work = jnp.where(lane == idx, -jnp.inf, work)
        vals.append(top)
        idxs.append(idx)
    exps = [jnp.exp(v - vals[0]) for v in vals]
    denom = exps[0] + exps[1] + exps[2] + exps[3]

    local_row = (i * tm) % lp + lax.broadcasted_iota(jnp.int32, (tm, 1), 0)
    member = jnp.zeros((tm, ne), F32)
    for k in range(TOP_K):
        member = member + jnp.where(lane == idxs[k], 1.0, 0.0)
    member = jnp.where(local_row >= n_pad, member, 0.0)
    earlier = (lax.broadcasted_iota(jnp.int32, (tm, tm), 0)
               > lax.broadcasted_iota(jnp.int32, (tm, tm), 1))
    before = run_ref[...] + jnp.dot(jnp.where(earlier, 1.0, 0.0).astype(BF16), member.astype(BF16),
                                    preferred_element_type=F32)
    run_ref[...] = run_ref[...] + jnp.sum(member, axis=0, keepdims=True)
    cnt_ref[...] = run_ref[...]

    slab = jnp.zeros(route_ref.shape, F32)
    for k in range(TOP_K):
        rank = jnp.sum(jnp.where(lane == idxs[k], before, 0.0), axis=-1, keepdims=True)
        slab = jnp.where(out_lane == k, idxs[k], slab)
        slab = jnp.where(out_lane == TOP_K + k, exps[k] / denom, slab)
        slab = jnp.where(out_lane == 2 * TOP_K + k, rank, slab)
    route_ref[...] = slab


def _out_proj(ret2d, att2d, hp2d, w_bf16, gain, w_router, b_router, *, tm, lp, n_pad):
    m, kr = ret2d.shape
    d = hp2d.shape[1]
    ne = w_router.shape[1]
    assert lp % tm == 0
    row = lambda i: (i, 0)
    const = lambda i: (0, 0)
    return pl.pallas_call(
        functools.partial(_out_proj_kernel, lp=lp, n_pad=n_pad),
        grid=(m // tm,),
        in_specs=[
            pl.BlockSpec((tm, kr), row),
            pl.BlockSpec((tm, kr), row),
            pl.BlockSpec((tm, d), row),
            pl.BlockSpec((2 * kr, d), const),
            pl.BlockSpec((1, d), const),
            pl.BlockSpec((d, ne), const),
            pl.BlockSpec((1, ne), const),
        ],
        out_specs=[
            pl.BlockSpec((tm, d), row),
            pl.BlockSpec((tm, d // 2), row),
            pl.BlockSpec((tm, 128), row),
            pl.BlockSpec((1, ne), const),
        ],
        out_shape=[
            jax.ShapeDtypeStruct((m, d), F32),
            jax.ShapeDtypeStruct((m, d // 2), jnp.uint32),
            jax.ShapeDtypeStruct((m, 128), F32),
            jax.ShapeDtypeStruct((1, ne), F32),
        ],
        scratch_shapes=[pltpu.VMEM((1, ne), F32)],
        compiler_params=_compiler_params(("arbitrary",)),
        name="out_proj",
    )(ret2d, att2d, hp2d, w_bf16, gain.reshape(1, d), w_router, b_router.reshape(1, ne))


EXPERT_TILE = 2176
EXPERT_SUB = 272
EXPERT_COLS = 256
EXPERT_WEIGHT_SLOTS = 3


def _expert_kernel(te_ref, nv_ref, q0_ref, nt_ref, rows_ref,
                   x_hbm, wgu_hbm, wd_hbm, bg_ref, bu_ref, bd_ref, o_ref,
                   xbuf, xb_ref, act_ref, wring, sem, wsem, *, tm, sub, n_up, n_down):
    i = pl.program_id(0)
    j = pl.program_id(1)
    n_used = nt_ref[0]
    used = i < n_used
    nv = nv_ref[i]
    half = xbuf.shape[1]
    tf = wring.shape[-1]
    f = wgu_hbm.shape[2] // 2
    steps = n_up + n_down
    n_slots = wring.shape[0]
    rows_per_step = tm // n_up
    up_phase = j < n_up
    wslot = (i * steps + j) % n_slots

    def weight_copy(src, s, piece):
        return pltpu.make_async_copy(src, wring.at[s, piece], wsem.at[s])

    def request_weights(ahead):
        wrap = j + ahead >= steps
        tile = jnp.where(wrap, i + 1, i)
        step = jnp.where(wrap, j + ahead - steps, j + ahead)
        s = (i * steps + j + ahead) % n_slots
        e = te_ref[tile]
        wanted = tile < n_used

        @pl.when(jnp.logical_and(wanted, step < n_up))
        def _():
            col = pl.multiple_of(step * tf, tf)
            weight_copy(wgu_hbm.at[e, :, pl.ds(col, tf)], s, 0).start()
            weight_copy(wgu_hbm.at[e, :, pl.ds(f + col, tf)], s, 1).start()

        @pl.when(jnp.logical_and(wanted, step >= n_up))
        def _():
            col = pl.multiple_of((step - n_up) * 2 * tf, 2 * tf)
            weight_copy(wd_hbm.at[e, :, pl.ds(col, tf)], s, 0).start()
            weight_copy(wd_hbm.at[e, :, pl.ds(col + tf, tf)], s, 1).start()

    @pl.when(jnp.logical_and(i == 0, j == 0))
    def _():
        for ahead in range(n_slots - 1):
            request_weights(ahead)

    @pl.when(used)
    def _():
        request_weights(n_slots - 1)
        for piece in range(2):
            weight_copy(wd_hbm.at[0, :, pl.ds(0, tf)], wslot, piece).wait()

    def row_copy(q, r):
        return pltpu.make_async_copy(x_hbm.at[rows_ref[q]], xbuf.at[r], sem)

    def gather_next_rows():
        r0 = pl.multiple_of(j * rows_per_step, rows_per_step)
        q = q0_ref[i + 1] + r0
        for r in range(rows_per_step):
            row_copy(q + r, r0 + r).start()

    @pl.when(jnp.logical_and(i == 0, j == 0))
    def _():
        def body(r, carry):
            row_copy(q0_ref[0] + r, r).start()
            return carry
        lax.fori_loop(0, tm, body, 0)

    @pl.when(jnp.logical_and(i <= n_used, j == 0))
    def _():
        pltpu.make_async_copy(xbuf, xbuf, sem).wait()

    @pl.when(jnp.logical_and(used, j == 0))
    def _():
        for s in range(tm // sub):
            r = slice(s * sub, (s + 1) * sub)
            lo, hi = _unpack_bf16_pairs(xbuf[r, :])
            xb_ref[r, :half] = lo
            xb_ref[r, half:] = hi

    def activation(x, wg, wu):
        gate = jnp.dot(x, wg, preferred_element_type=F32) + bg_ref[0]
        up = jnp.dot(x, wu, preferred_element_type=F32) + bu_ref[0]
        gate = jnp.minimum(gate, SWIGLU_LIMIT)
        up = jnp.clip(up, -SWIGLU_LIMIT, SWIGLU_LIMIT)
        glu = gate * jax.nn.sigmoid(SWIGLU_ALPHA * gate)
        return ((up + 1.0) * glu).astype(BF16)

    def down(rows, wd, cols):
        act = jnp.concatenate([act_ref[c, rows, :] for c in range(n_up)], axis=1)
        return jnp.dot(act, wd, preferred_element_type=F32) + bd_ref[0, :, cols]

    whole = jnp.logical_and(used, nv > tm - sub)
    partial = jnp.logical_and(used, nv <= tm - sub)

    @pl.when(jnp.logical_and(whole, up_phase))
    def _():
        gather_next_rows()
        wg = wring[wslot, 0].astype(BF16)
        wu = wring[wslot, 1].astype(BF16)
        for part in range(2):
            rows = slice(part * tm // 2, (part + 1) * tm // 2)
            act_ref[j, rows, :] = activation(xb_ref[rows, :], wg, wu)

    @pl.when(jnp.logical_and(whole, jnp.logical_not(up_phase)))
    def _():
        for piece in range(2):
            cols = slice(piece * tf, (piece + 1) * tf)
            o_ref[:, cols] = down(slice(None), wring[wslot, piece].astype(BF16), cols)

    for s in range(tm // sub):
        r = slice(s * sub, (s + 1) * sub)
        live = jnp.logical_and(partial, s * sub < nv)
        dead = jnp.logical_not(jnp.logical_or(whole, live))

        @pl.when(jnp.logical_and(live, up_phase))
        def _():
            if s == 0:
                gather_next_rows()
            act_ref[j, r, :] = activation(xb_ref[r, :], wring[wslot, 0].astype(BF16),
                                          wring[wslot, 1].astype(BF16))

        @pl.when(jnp.logical_and(live, jnp.logical_not(up_phase)))
        def _():
            for piece in range(2):
                cols = slice(piece * tf, (piece + 1) * tf)
                o_ref[r, cols] = down(r, wring[wslot, piece].astype(BF16), cols)

        @pl.when(jnp.logical_and(dead, jnp.logical_not(up_phase)))
        def _():
            o_ref[r, :] = jnp.zeros((sub, o_ref.shape[1]), F32)


def _experts(tile_expert, tile_valid, tile_q0, n_used, sorted_rows, x_packed,
             w_gate_up, b_gate_up, w_down, b_down, *, n_tiles):
    tm, sub = EXPERT_TILE, EXPERT_SUB
    half = x_packed.shape[1]
    ne, d, f2 = w_gate_up.shape
    f = f2 // 2
    assert d == 2 * half
    tf = EXPERT_COLS
    tn = 2 * tf
    n_up, n_down = f // tf, d // tn
    assert tm % n_up == 0 and tm % sub == 0
    assert f == d

    def up_chunk(i, j, nt):
        return jnp.where(i < nt[0], jnp.minimum(j, n_up - 1), n_up - 1)

    def down_chunk(i, j, nt):
        return jnp.where(i < nt[0], jnp.maximum(j - n_up, 0), 0)

    b_gu = b_gate_up.reshape(ne, 1, f2)
    b_d = b_down.reshape(ne, 1, d)
    return pl.pallas_call(
        functools.partial(_expert_kernel, tm=tm, sub=sub, n_up=n_up, n_down=n_down),
        grid_spec=pltpu.PrefetchScalarGridSpec(
            num_scalar_prefetch=5,
            grid=(n_tiles, n_up + n_down),
            in_specs=[
                pl.BlockSpec(memory_space=pl.ANY),
                pl.BlockSpec(memory_space=pl.ANY),
                pl.BlockSpec(memory_space=pl.ANY),
                pl.BlockSpec((1, 1, tf), lambda i, j, te, nv, q0, nt, rows: (te[i], 0, up_chunk(i, j, nt))),
                pl.BlockSpec((1, 1, tf), lambda i, j, te, nv, q0, nt, rows: (te[i], 0, n_up + up_chunk(i, j, nt))),
                pl.BlockSpec((1, 1, tn), lambda i, j, te, nv, q0, nt, rows: (te[i], 0, down_chunk(i, j, nt))),
            ],
            out_specs=pl.BlockSpec((tm, tn), lambda i, j, te, nv, q0, nt, rows: (i, jnp.maximum(j - n_up, 0))),
            scratch_shapes=[
                pltpu.VMEM((tm, half), jnp.uint32),
                pltpu.VMEM((tm, d), BF16),
                pltpu.VMEM((n_up, tm, tf), BF16),
                pltpu.VMEM((EXPERT_WEIGHT_SLOTS, 2, d, tf), F32),
                pltpu.SemaphoreType.DMA,
                pltpu.SemaphoreType.DMA((EXPERT_WEIGHT_SLOTS,)),
            ],
        ),
        out_shape=jax.ShapeDtypeStruct((n_tiles * tm, d), F32),
        compiler_params=_compiler_params(("arbitrary", "arbitrary")),
        name="experts",
    )(tile_expert, tile_valid, tile_q0, n_used, sorted_rows, x_packed,
      w_gate_up, w_down, b_gu, b_gu, b_d)


def _combine_kernel(dest_ref, h1_ref, gate_ref, y_hbm, o_ref, buf, sem, *, tt, tiles_per_batch,
                    tok_per_batch, n_steps):
    t = pl.program_id(0)
    slot = t % 2

    def request_rows(step, s):
        base = ((step // tiles_per_batch) * tok_per_batch + N_META + (step % tiles_per_batch) * tt) * TOP_K
        for r in range(tt):
            for k in range(TOP_K):
                pltpu.make_async_copy(y_hbm.at[dest_ref[base + r * TOP_K + k]], buf.at[s, k, r],
                                      sem.at[s]).start()

    def wait_rows(s):
        pltpu.make_async_copy(buf.at[s], buf.at[s], sem.at[s]).wait()

    @pl.when(t == 0)
    def _():
        request_rows(0, 0)

    wait_rows(slot)
    request_rows(jnp.minimum(t + 1, n_steps - 1), 1 - slot)
    g = gate_ref[0]
    moe = buf[slot, 0] * g[:, 0:1]
    for k in range(1, TOP_K):
        moe = moe + buf[slot, k] * g[:, k:k + 1]
    o_ref[0] = h1_ref[0] + moe

    @pl.when(t == n_steps - 1)
    def _():
        wait_rows(1 - slot)


def _combine(dest_flat, h1, gates, y_sorted, *, tt):
    b, lp, d = h1.shape
    s = lp - CHUNK
    tok_per_batch = s + N_META
    tiles = s // tt
    off = CHUNK // tt
    gates_tok = gates[:, N_META:, :]
    n_steps = b * tiles
    return pl.pallas_call(
        functools.partial(_combine_kernel, tt=tt, tiles_per_batch=tiles, tok_per_batch=tok_per_batch,
                          n_steps=n_steps),
        grid_spec=pltpu.PrefetchScalarGridSpec(
            num_scalar_prefetch=1,
            grid=(n_steps,),
            in_specs=[
                pl.BlockSpec((1, tt, d), lambda t, dest: (t // tiles, off + t % tiles, 0)),
                pl.BlockSpec((1, tt, TOP_K), lambda t, dest: (t // tiles, t % tiles, 0)),
                pl.BlockSpec(memory_space=pl.ANY),
            ],
            out_specs=pl.BlockSpec((1, tt, d), lambda t, dest: (t // tiles, t % tiles, 0)),
            scratch_shapes=[pltpu.VMEM((2, TOP_K, tt, d), F32), pltpu.SemaphoreType.DMA((2,))],
        ),
        out_shape=jax.ShapeDtypeStruct((b, s, d), F32),
        compiler_params=_compiler_params(("arbitrary",)),
        name="combine",
    )(dest_flat, h1, gates_tok, y_sorted)


def _lookup(table, idx):
    n = table.shape[0]
    hit = idx[..., None] == jnp.arange(n, dtype=idx.dtype)
    return jnp.sum(jnp.where(hit, table, 0), axis=-1)


def _routing(top_idx, rank, counts, lp, n_pad, *, tm):
    b, l, k = top_idx.shape
    ne = counts.shape[0]
    p = b * l * k
    n_tiles = p // tm + ne + 1
    tiles_per = (counts + tm - 1) // tm
    tile_end = jnp.cumsum(tiles_per)
    tile_start = tile_end - tiles_per
    n_used = tile_end[-1]
    starts = jnp.cumsum(counts) - counts
    tile_id = jnp.arange(n_tiles, dtype=jnp.int32)
    tile_expert = jnp.minimum(jnp.sum(tile_end[None, :] <= tile_id[:, None], axis=-1), ne - 1).astype(jnp.int32)
    within = tile_id - _lookup(tile_start, tile_expert)
    tile_valid = jnp.where(tile_id < n_used, jnp.clip(_lookup(counts, tile_expert) - within * tm, 0, tm), 0)
    tile_q0 = jnp.where(tile_id < n_used, _lookup(starts, tile_expert) + within * tm, 0)
    tok = jnp.arange(b * l, dtype=jnp.int32)
    tok_bits = max(1, (b * l - 1).bit_length())
    keys = jnp.sort((top_idx.reshape(b * l, k) << tok_bits | tok[:, None]).reshape(p))
    sorted_tok = keys & ((1 << tok_bits) - 1)
    sorted_rows = (sorted_tok // l) * lp + n_pad + sorted_tok % l
    sorted_rows = jnp.concatenate([sorted_rows, jnp.arange(tm, dtype=jnp.int32) % (b * lp)])
    dest = _lookup(tile_start * tm, top_idx) + rank
    return (tile_expert, tile_valid.astype(jnp.int32), tile_q0.astype(jnp.int32),
            n_used.astype(jnp.int32).reshape(1), sorted_rows.astype(jnp.int32),
            dest.astype(jnp.int32).reshape(p), n_tiles)


def kernel(x, meta_tokens, attn_norm_gain, w_in, ret_decay_fwd, ret_decay_bwd, ret_out_gain,
           diff_q_gain, diff_k_gain, diff_lambda_q1, diff_lambda_k1, diff_lambda_q2, diff_lambda_k2,
           diff_out_gain, w_out, ffn_norm_gain, w_router, b_router, w_gate_up, b_gate_up,
           w_down, b_down):
    b, s, d = x.shape
    assert s % CHUNK == 0 and w_in.shape[0] == 1, "single layer, token count a chunk multiple"
    n_pad = CHUNK - N_META
    lp = s + CHUNK
    l = s + N_META

    hp, u = _prep(x, meta_tokens.astype(x.dtype), attn_norm_gain[0])
    z = _in_proj(u.reshape(b * lp, d), w_in[0].astype(BF16), tm=lp // 4, tn=1024)
    z = z.reshape(b, lp, -1)

    lg_f = jnp.log1p(-jnp.exp2(-ret_decay_fwd[0].astype(F32)))
    lg_b = jnp.log1p(-jnp.exp2(-ret_decay_bwd[0].astype(F32)))
    ret = _retention(z, lg_f, lg_b, ret_out_gain[0])

    lam = (jnp.exp(jnp.sum(diff_lambda_q1[0].astype(F32) * diff_lambda_k1[0].astype(F32)))
           - jnp.exp(jnp.sum(diff_lambda_q2[0].astype(F32) * diff_lambda_k2[0].astype(F32)))
           + LAMBDA_INIT).reshape(1)
    slopes = jnp.exp2(-8.0 * jnp.arange(1, DIFF_HEADS + 1, dtype=F32) / DIFF_HEADS)
    att = _attention(z, lam, slopes, diff_q_gain[0], diff_k_gain[0], diff_out_gain[0], tq=384, tk=384)

    h1, xt, route, counts = _out_proj(
        ret.reshape(b * lp, -1), att.reshape(b * lp, -1), hp.reshape(b * lp, d),
        w_out[0].astype(BF16), ffn_norm_gain[0], w_router[0], b_router[0], tm=lp // 8, lp=lp, n_pad=n_pad)

    route = route.reshape(b, lp, 128)[:, n_pad:, :]
    top_idx = route[:, :, :TOP_K].astype(jnp.int32)
    gates = route[:, :, TOP_K:2 * TOP_K]
    rank = route[:, :, 2 * TOP_K:3 * TOP_K].astype(jnp.int32)

    tile_expert, tile_valid, tile_q0, n_used, sorted_rows, dest, n_tiles = _routing(
        top_idx, rank, counts.reshape(-1).astype(jnp.int32), lp, n_pad, tm=EXPERT_TILE)
    ys = _experts(tile_expert, tile_valid, tile_q0, n_used, sorted_rows, xt,
                  w_gate_up[0], b_gate_up[0], w_down[0], b_down[0], n_tiles=n_tiles)
    return _combine(dest, h1.reshape(b, lp, d), gates, ys, tt=CHUNK)
```

```python
import functools
import math

import jax
import jax.numpy as jnp
from jax import lax
from jax.experimental import pallas as pl
from jax.experimental.pallas import tpu as pltpu

N_META = 16
CHUNK = 128
RET_HEADS = 4
RET_QK_DIM = 128
RET_V_DIM = 256
DIFF_HEADS = 4
DIFF_QK_DIM = 128
DIFF_V_DIM = 256
TOP_K = 4
SWIGLU_ALPHA = 1.702
SWIGLU_LIMIT = 7.0
NORM_EPS = 1e-6
RET_DECAY_BASE = 5.0
LAMBDA_INIT = 0.8 - 0.6 * math.exp(-0.3 * 0)
NEG_BIG = -1e30

V7X_VMEM_BYTES = 64 * 1024 * 1024
VMEM_LIMIT_BYTES = V7X_VMEM_BYTES - 8 * 1024 * 1024

F32 = jnp.float32
BF16 = jnp.bfloat16

ROW_DMA_QUEUE = 1


def _compiler_params(semantics):
    return pltpu.CompilerParams(dimension_semantics=semantics, vmem_limit_bytes=VMEM_LIMIT_BYTES)


def _prep_kernel(x_ref, meta_ref, g_ref, hp_ref, u_ref, *, n_pad):
    c = pl.program_id(1)

    def emit(h):
        hp_ref[0] = h
        ms = jnp.mean(h * h, axis=-1, keepdims=True)
        u_ref[0] = (h * lax.rsqrt(ms + NORM_EPS) * g_ref[...]).astype(u_ref.dtype)

    @pl.when(c == 0)
    def _():
        d = meta_ref.shape[-1]
        emit(jnp.concatenate([jnp.zeros((n_pad, d), F32), meta_ref[...]], axis=0))

    @pl.when(c > 0)
    def _():
        emit(x_ref[0])


def _prep(x, meta, gain):
    b, s, d = x.shape
    n_pad = CHUNK - N_META
    nc = s // CHUNK + 1
    lp = nc * CHUNK
    return pl.pallas_call(
        functools.partial(_prep_kernel, n_pad=n_pad),
        grid=(b, nc),
        in_specs=[
            pl.BlockSpec((1, CHUNK, d), lambda i, c: (i, jnp.maximum(c - 1, 0), 0)),
            pl.BlockSpec((N_META, d), lambda i, c: (0, 0)),
            pl.BlockSpec((1, d), lambda i, c: (0, 0)),
        ],
        out_specs=[
            pl.BlockSpec((1, CHUNK, d), lambda i, c: (i, c, 0)),
            pl.BlockSpec((1, CHUNK, d), lambda i, c: (i, c, 0)),
        ],
        out_shape=[
            jax.ShapeDtypeStruct((b, lp, d), F32),
            jax.ShapeDtypeStruct((b, lp, d), BF16),
        ],
        compiler_params=_compiler_params(("parallel", "arbitrary")),
        name="prep",
    )(x, meta, gain.reshape(1, d))


def _matmul_kernel(a_ref, b_ref, o_ref):
    o_ref[...] = jnp.dot(a_ref[...], b_ref[...], preferred_element_type=F32).astype(o_ref.dtype)


def _in_proj(u2d, w_bf16, *, tm, tn):
    m, k = u2d.shape
    n = w_bf16.shape[1]
    return pl.pallas_call(
        _matmul_kernel,
        grid=(m // tm, n // tn),
        in_specs=[
            pl.BlockSpec((tm, k), lambda i, j: (i, 0)),
            pl.BlockSpec((k, tn), lambda i, j: (0, j)),
        ],
        out_specs=pl.BlockSpec((tm, tn), lambda i, j: (i, j)),
        out_shape=jax.ShapeDtypeStruct((m, n), BF16),
        compiler_params=_compiler_params(("parallel", "arbitrary")),
        name="in_proj",
    )(u2d, w_bf16)


def _retention_kernel(lgf_ref, lgb_ref, q_ref, k_ref, v_ref, g_ref, gain_ref, o_ref,
                      ob_ref, sf_ref, sb_ref, dec_ref, *, nc, heads):
    c = CHUNK
    dk, dv = RET_QK_DIM, RET_V_DIM
    scale = dk ** -0.5
    row = lax.broadcasted_iota(jnp.int32, (c, c), 0)
    col = lax.broadcasted_iota(jnp.int32, (c, c), 1)
    rel = (row - col).astype(F32)
    pos = lax.broadcasted_iota(jnp.int32, (c, 1), 0).astype(F32)
    q_dec_f, k_dec_f, q_dec_b, k_dec_b, chunk_f, chunk_b = [], [], [], [], [], []
    for hh in range(heads):
        lgf = lgf_ref[pl.program_id(1) * heads + hh]
        lgb = lgb_ref[pl.program_id(1) * heads + hh]
        dec_ref[hh] = jnp.where(rel >= 0, jnp.exp(lgf * jnp.maximum(rel, 0.0)),
                                jnp.exp(lgb * jnp.maximum(-rel, 0.0))) * scale
        q_dec_f.append(jnp.exp(lgf * (pos + 1.0)))
        k_dec_f.append(jnp.exp(lgf * (c - 1.0 - pos)) * scale)
        q_dec_b.append(jnp.exp(lgb * (c - pos)))
        k_dec_b.append(jnp.exp(lgb * pos) * scale)
        chunk_f.append(jnp.exp(lgf * c))
        chunk_b.append(jnp.exp(lgb * c))

    def rows(n):
        return pl.ds(pl.multiple_of(n * c, c), c)

    def qk_cols(hh):
        return slice(hh * dk, (hh + 1) * dk)

    def v_cols(hh):
        return slice(hh * dv, (hh + 1) * dv)

    def state_update(s_ref, hh, kn, vn, k_dec, chunk_decay):
        kd_t = (kn.astype(F32) * k_dec).T.astype(BF16)
        s_ref[hh] = s_ref[hh] * chunk_decay + jnp.dot(kd_t, vn, preferred_element_type=F32)

    sb_ref[...] = jnp.zeros_like(sb_ref)
    sf_ref[...] = jnp.zeros_like(sf_ref)

    def bwd_body(t, carry):
        n = nc - 1 - t
        for hh in range(heads):
            qn = q_ref[0, rows(n), qk_cols(hh)]
            kn = k_ref[0, rows(n), qk_cols(hh)]
            vn = v_ref[0, rows(n), v_cols(hh)]
            qd = (qn.astype(F32) * q_dec_b[hh]).astype(BF16)
            ob_ref[rows(n), v_cols(hh)] = jnp.dot(qd, sb_ref[hh].astype(BF16), preferred_element_type=F32)
            state_update(sb_ref, hh, kn, vn, k_dec_b[hh], chunk_b[hh])
        return carry

    lax.fori_loop(0, nc, bwd_body, 0)

    def fwd_body(n, carry):
        for hh in range(heads):
            qn = q_ref[0, rows(n), qk_cols(hh)]
            kn = k_ref[0, rows(n), qk_cols(hh)]
            vn = v_ref[0, rows(n), v_cols(hh)]
            sc = lax.dot_general(qn, kn, (((1,), (1,)), ((), ())), preferred_element_type=F32)
            inner = jnp.dot((sc * dec_ref[hh]).astype(BF16), vn, preferred_element_type=F32)
            qd = (qn.astype(F32) * q_dec_f[hh]).astype(BF16)
            cross = jnp.dot(qd, sf_ref[hh].astype(BF16), preferred_element_type=F32)
            o = inner + cross + ob_ref[rows(n), v_cols(hh)]
            state_update(sf_ref, hh, kn, vn, k_dec_f[hh], chunk_f[hh])
            o = o * lax.rsqrt(jnp.mean(o * o, axis=-1, keepdims=True) + NORM_EPS) * gain_ref[:, v_cols(hh)]
            gate = g_ref[0, rows(n), v_cols(hh)].astype(F32)
            o_ref[0, rows(n), v_cols(hh)] = (o * (gate * jax.nn.sigmoid(gate))).astype(o_ref.dtype)
        return carry

    lax.fori_loop(0, nc, fwd_body, 0)


def _retention(z, lg_f, lg_b, ret_gain):
    b, lp, _ = z.shape
    nc = lp // CHUNK
    dk, dv, nh = RET_QK_DIM, RET_V_DIM, RET_HEADS
    heads = 2
    groups = nh // heads
    k_blk0 = groups
    v_blk0 = 2 * nh * dk // (heads * dv)
    g_blk0 = v_blk0 + groups
    smem = pl.BlockSpec(memory_space=pltpu.SMEM)
    return pl.pallas_call(
        functools.partial(_retention_kernel, nc=nc, heads=heads),
        grid=(b, groups),
        in_specs=[
            smem, smem,
            pl.BlockSpec((1, lp, heads * dk), lambda i, h: (i, 0, h)),
            pl.BlockSpec((1, lp, heads * dk), lambda i, h: (i, 0, k_blk0 + h)),
            pl.BlockSpec((1, lp, heads * dv), lambda i, h: (i, 0, v_blk0 + h)),
            pl.BlockSpec((1, lp, heads * dv), lambda i, h: (i, 0, g_blk0 + h)),
            pl.BlockSpec((1, heads * dv), lambda i, h: (0, h)),
        ],
        out_specs=pl.BlockSpec((1, lp, heads * dv), lambda i, h: (i, 0, h)),
        out_shape=jax.ShapeDtypeStruct((b, lp, nh * dv), BF16),
        scratch_shapes=[
            pltpu.VMEM((lp, heads * dv), F32),
            pltpu.VMEM((heads, dk, dv), F32),
            pltpu.VMEM((heads, dk, dv), F32),
            pltpu.VMEM((heads, CHUNK, CHUNK), F32),
        ],
        compiler_params=_compiler_params(("parallel", "arbitrary")),
        name="retention",
    )(lg_f, lg_b, z, z, z, z, ret_gain.reshape(1, nh * dv))


def _attn_kernel(lam_ref, slope_ref, q_ref, k_ref, v_ref, qg_ref, kg_ref, og_ref, o_ref,
                 kaug_ref, corr_ref, qaug_ref, s_ref, p_ref, m_ref, *, tq, tk, n_pad):
    h = pl.program_id(1)
    qi = pl.program_id(2)
    lp = k_ref.shape[1]
    d = DIFF_QK_DIM
    n_chunks = lp // tk
    lanes = 128
    lam = lam_ref[0]
    log2e = math.log2(math.e)
    slope = slope_ref[h] * log2e
    contract_last = (((1,), (1,)), ((), ()))

    def qk_norm(x, gain):
        return x * lax.rsqrt(jnp.mean(x * x, axis=-1, keepdims=True) + NORM_EPS) * gain

    def split3(v):
        hi = v.astype(BF16).astype(F32)
        mid = (v - hi).astype(BF16).astype(F32)
        lo = (v - hi - mid).astype(BF16).astype(F32)
        return [hi, mid, lo]

    def extra_lanes(cols):
        rows = cols[0].shape[0]
        lane = lax.broadcasted_iota(jnp.int32, (rows, d), 1)
        out = jnp.zeros((rows, d), F32)
        for t, col in enumerate(cols):
            out = jnp.where(lane == t, col, out)
        return out.astype(BF16)

    @pl.when(qi == 0)
    def _():
        for c in range(n_chunks):
            r = slice(c * tk, (c + 1) * tk)
            kk = k_ref[0, r, :].astype(F32)
            k_row = c * tk + lax.broadcasted_iota(jnp.int32, (tk, 1), 0)
            real = k_row >= CHUNK
            one = jnp.where(real, 1.0, 0.0)
            first = jnp.where(k_row < n_pad, NEG_BIG, 0.0)
            sk = [jnp.where(real, t, 0.0) for t in split3(slope * k_row.astype(F32))]
            extra = extra_lanes([sk[0] + first, sk[1], sk[2], one, one, one])
            for m in range(2):
                kaug_ref[m, r, :d] = qk_norm(kk[:, m * d:(m + 1) * d], kg_ref[...]).astype(BF16)
                kaug_ref[m, r, d:] = extra
        i_loc = lax.broadcasted_iota(jnp.int32, (tq, tk), 0)
        j_loc = lax.broadcasted_iota(jnp.int32, (tq, tk), 1)
        corr_ref[...] = (2.0 * slope) * jnp.maximum(j_loc - i_loc, 0).astype(F32)

    qq = q_ref[0].astype(F32)
    qs = [(qk_norm(qq[:, m * d:(m + 1) * d], qg_ref[...]) * (d ** -0.5 * log2e)).astype(BF16)
          for m in range(2)]
    q_row = qi * tq + lax.broadcasted_iota(jnp.int32, (tq, 1), 0)
    real_q = q_row >= CHUNK
    sq = [jnp.where(real_q, t, 0.0) for t in split3(slope * q_row.astype(F32))]
    one = jnp.where(real_q, 1.0, 0.0)
    extra_before = extra_lanes([one, one, one] + [-t for t in sq])
    extra_after = extra_lanes([-one, -one, -one] + sq)
    for m in range(2):
        qaug_ref[0, m] = jnp.concatenate([qs[m], extra_before], axis=1)
        qaug_ref[1, m] = jnp.concatenate([qs[m], extra_after], axis=1)

    def lane_fold(x, op):
        out = x[:, :lanes]
        for t in range(1, x.shape[1] // lanes):
            out = op(out, x[:, t * lanes:(t + 1) * lanes])
        return out

    m_ref[...] = jnp.full_like(m_ref, NEG_BIG)
    for c in range(n_chunks):
        r = slice(c * tk, (c + 1) * tk)
        side = jnp.where(c <= qi, 0, 1)
        for m in range(2):
            s = lax.dot_general(qaug_ref[side, m], kaug_ref[m, r, :], contract_last,
                                preferred_element_type=F32)
            s_ref[m, c] = s
            m_ref[m] = jnp.maximum(m_ref[m], jnp.where(c == qi, NEG_BIG, lane_fold(s, jnp.maximum)))
    for m in range(2):
        s = s_ref[m, qi] - corr_ref[...]
        s_ref[m, qi] = s
        m_ref[m] = jnp.maximum(m_ref[m], lane_fold(s, jnp.maximum))

    @pl.when(qi == 0)
    def _():
        q_pos = lax.broadcasted_iota(jnp.int32, (tq, tk), 0)
        k_pos = lax.broadcasted_iota(jnp.int32, (tq, tk), 1)
        dist = jnp.abs(q_pos - k_pos).astype(F32)
        both_real = jnp.logical_and(q_pos >= CHUNK, k_pos >= CHUNK)
        bias = jnp.where(both_real, -slope * dist, 0.0)
        for m in range(2):
            s = lax.dot_general(qs[m], kaug_ref[m, :tk, :d], contract_last, preferred_element_type=F32)
            s = jnp.where(k_pos >= n_pad, s + bias, NEG_BIG)
            s_ref[m, 0] = s
            m_ref[m] = jnp.maximum(m_ref[m], lane_fold(s, jnp.maximum))

    band = lanes
    inv_l = []
    for m in range(2):
        for rb in range(tq // band):
            rows = slice(rb * band, (rb + 1) * band)
            row_max = jnp.max(m_ref[m, rows, :], axis=-1, keepdims=True)
            shift = jnp.concatenate([jnp.broadcast_to(row_max, (band, lanes))] * (tk // lanes), axis=1)
            l_part = jnp.zeros((band, lanes), F32)
            for c in range(n_chunks):
                e = jnp.exp2(s_ref[m, c, rows, :] - shift)
                l_part = l_part + lane_fold(e, jnp.add)
                p_ref[m * tq + rb * band:m * tq + (rb + 1) * band, c * tk:(c + 1) * tk] = e.astype(BF16)
            inv_l.append(1.0 / jnp.sum(l_part, axis=-1, keepdims=True))
    acc = jnp.dot(p_ref[...], v_ref[0], preferred_element_type=F32) * jnp.concatenate(inv_l, axis=0)

    o = acc[:tq] - lam * acc[tq:]
    o = o * lax.rsqrt(jnp.mean(o * o, axis=-1, keepdims=True) + NORM_EPS) * og_ref[...]
    o_ref[0] = (o * (1.0 - LAMBDA_INIT)).astype(o_ref.dtype)


def _attention(z, lam, slopes, q_gain, k_gain, out_gain, *, tq, tk):
    b, lp, _ = z.shape
    nh, d, dv = DIFF_HEADS, DIFF_QK_DIM, DIFF_V_DIM
    assert tq == tk and tq >= CHUNK and lp % tq == 0
    ret_cols = 2 * RET_HEADS * RET_QK_DIM + 2 * RET_HEADS * RET_V_DIM
    q_blk0 = ret_cols // (2 * d)
    k_blk0 = q_blk0 + nh
    v_blk0 = k_blk0 + nh
    smem = pl.BlockSpec(memory_space=pltpu.SMEM)
    return pl.pallas_call(
        functools.partial(_attn_kernel, tq=tq, tk=tk, n_pad=CHUNK - N_META),
        grid=(b, nh, lp // tq),
        in_specs=[
            smem, smem,
            pl.BlockSpec((1, tq, 2 * d), lambda i, h, q: (i, q, q_blk0 + h)),
            pl.BlockSpec((1, lp, 2 * d), lambda i, h, q: (i, 0, k_blk0 + h)),
            pl.BlockSpec((1, lp, dv), lambda i, h, q: (i, 0, v_blk0 + h)),
            pl.BlockSpec((1, d), lambda i, h, q: (0, 0)),
            pl.BlockSpec((1, d), lambda i, h, q: (0, 0)),
            pl.BlockSpec((1, dv), lambda i, h, q: (0, h)),
        ],
        out_specs=pl.BlockSpec((1, tq, dv), lambda i, h, q: (i, q, h)),
        out_shape=jax.ShapeDtypeStruct((b, lp, nh * dv), BF16),
        scratch_shapes=[
            pltpu.VMEM((2, lp, 2 * d), BF16),
            pltpu.VMEM((tq, tk), F32),
            pltpu.VMEM((2, 2, tq, 2 * d), BF16),
            pltpu.VMEM((2, lp // tk, tq, tk), F32),
            pltpu.VMEM((2 * tq, lp), BF16),
            pltpu.VMEM((2, tq, 128), F32),
        ],
        compiler_params=_compiler_params(("parallel", "parallel", "arbitrary")),
        name="attention",
    )(lam, slopes, z, z, z, q_gain.reshape(1, d), k_gain.reshape(1, d), out_gain.reshape(1, nh * dv))


def _pack_bf16_pairs(x):
    n = x.shape[1] // 2
    xb = x.astype(BF16).astype(F32)
    lo = lax.bitcast_convert_type(xb[:, :n], jnp.uint32)
    hi = lax.bitcast_convert_type(xb[:, n:], jnp.uint32)
    return (hi & jnp.uint32(0xFFFF0000)) | (lo >> 16)


def _unpack_bf16_pairs(p):
    lo = lax.bitcast_convert_type(p << 16, F32).astype(BF16)
    hi = lax.bitcast_convert_type(p & jnp.uint32(0xFFFF0000), F32).astype(BF16)
    return lo, hi


def _out_proj_kernel(ret_ref, att_ref, hp_ref, w_ref, g_ref, wr_ref, br_ref,
                     h1_ref, xt_ref, route_ref, cnt_ref, run_ref, *, lp, n_pad):
    i = pl.program_id(0)

    @pl.when(i == 0)
    def _():
        run_ref[...] = jnp.zeros_like(run_ref)

    kr = ret_ref.shape[1]
    y = jnp.dot(ret_ref[...], w_ref[:kr, :], preferred_element_type=F32)
    y = y + jnp.dot(att_ref[...], w_ref[kr:, :], preferred_element_type=F32)
    h1 = hp_ref[...] + y
    h1_ref[...] = h1
    xt = h1 * lax.rsqrt(jnp.mean(h1 * h1, axis=-1, keepdims=True) + NORM_EPS) * g_ref[...]
    xt_ref[...] = _pack_bf16_pairs(xt)
    xt_hi = xt.astype(BF16)
    xt_lo = (xt - xt_hi.astype(F32)).astype(BF16)
    wr = wr_ref[...]
    wr_hi = wr.astype(BF16)
    wr_lo = (wr - wr_hi.astype(F32)).astype(BF16)
    logits = (jnp.dot(xt_hi, wr_hi, preferred_element_type=F32)
              + jnp.dot(xt_lo, wr_hi, preferred_element_type=F32)
              + jnp.dot(xt_hi, wr_lo, preferred_element_type=F32)) + br_ref[...]
    tm, ne = logits.shape
    lane = lax.broadcasted_iota(jnp.int32, (tm, ne), 1).astype(F32)
    out_lane = lax.broadcasted_iota(jnp.int32, route_ref.shape, 1)
    vals, idxs = [], []
    work = logits
    for _ in range(TOP_K):
        top = jnp.max(work, axis=-1, keepdims=True)
        idx = jnp.min(jnp.where(work == top, lane, float(ne)), axis=-1, keepdims=True)
        work = jnp.where(lane == idx, -jnp.inf, work)
        vals.append(top)
        idxs.append(idx)
    exps = [jnp.exp(v - vals[0]) for v in vals]
    denom = exps[0] + exps[1] + exps[2] + exps[3]

    local_row = (i * tm) % lp + lax.broadcasted_iota(jnp.int32, (tm, 1), 0)
    member = jnp.zeros((tm, ne), F32)
    for k in range(TOP_K):
        member = member + jnp.where(lane == idxs[k], 1.0, 0.0)
    member = jnp.where(local_row >= n_pad, member, 0.0)
    earlier = (lax.broadcasted_iota(jnp.int32, (tm, tm), 0)
               > lax.broadcasted_iota(jnp.int32, (tm, tm), 1))
    before = run_ref[...] + jnp.dot(jnp.where(earlier, 1.0, 0.0).astype(BF16), member.astype(BF16),
                                    preferred_element_type=F32)
    run_ref[...] = run_ref[...] + jnp.sum(member, axis=0, keepdims=True)
    cnt_ref[...] = run_ref[...]

    slab = jnp.zeros(route_ref.shape, F32)
    for k in range(TOP_K):
        rank = jnp.sum(jnp.where(lane == idxs[k], before, 0.0), axis=-1, keepdims=True)
        slab = jnp.where(out_lane == k, idxs[k], slab)
        slab = jnp.where(out_lane == TOP_K + k, exps[k] / denom, slab)
        slab = jnp.where(out_lane == 2 * TOP_K + k, rank, slab)
    route_ref[...] = slab


def _out_proj(ret2d, att2d, hp2d, w_bf16, gain, w_router, b_router, *, tm, lp, n_pad):
    m, kr = ret2d.shape
    d = hp2d.shape[1]
    ne = w_router.shape[1]
    assert lp % tm == 0
    row = lambda i: (i, 0)
    const = lambda i: (0, 0)
    return pl.pallas_call(
        functools.partial(_out_proj_kernel, lp=lp, n_pad=n_pad),
        grid=(m // tm,),
        in_specs=[
            pl.BlockSpec((tm, kr), row),
            pl.BlockSpec((tm, kr), row),
            pl.BlockSpec((tm, d), row),
            pl.BlockSpec((2 * kr, d), const),
            pl.BlockSpec((1, d), const),
            pl.BlockSpec((d, ne), const),
            pl.BlockSpec((1, ne), const),
        ],
        out_specs=[
            pl.BlockSpec((tm, d), row),
            pl.BlockSpec((tm, d // 2), row),
            pl.BlockSpec((tm, 128), row),
            pl.BlockSpec((1, ne), const),
        ],
        out_shape=[
            jax.ShapeDtypeStruct((m, d), F32),
            jax.ShapeDtypeStruct((m, d // 2), jnp.uint32),
            jax.ShapeDtypeStruct((m, 128), F32),
            jax.ShapeDtypeStruct((1, ne), F32),
        ],
        scratch_shapes=[pltpu.VMEM((1, ne), F32)],
        compiler_params=_compiler_params(("arbitrary",)),
        name="out_proj",
    )(ret2d, att2d, hp2d, w_bf16, gain.reshape(1, d), w_router, b_router.reshape(1, ne))


EXPERT_TILE = 2176
EXPERT_SUB = 272
EXPERT_COLS = 256
EXPERT_WEIGHT_SLOTS = 3


def _expert_kernel(te_ref, nv_ref, q0_ref, nt_ref, rows_ref,
                   x_hbm, wgu_hbm, wd_hbm, bg_ref, bu_ref, bd_ref, o_ref,
                   xbuf, xb_ref, act_ref, wring, sem, wsem, *, tm, sub, n_up, n_down):
    i = pl.program_id(0)
    j = pl.program_id(1)
    n_used = nt_ref[0]
    used = i < n_used
    nv = nv_ref[i]
    half = xbuf.shape[1]
    tf = wring.shape[-1]
    f = wgu_hbm.shape[2] // 2
    steps = n_up + n_down
    n_slots = wring.shape[0]
    rows_per_step = tm // n_up
    up_phase = j < n_up
    wslot = (i * steps + j) % n_slots

    def weight_copy(src, s, piece):
        return pltpu.make_async_copy(src, wring.at[s, piece], wsem.at[s])

    def request_weights(ahead):
        wrap = j + ahead >= steps
        tile = jnp.where(wrap, i + 1, i)
        step = jnp.where(wrap, j + ahead - steps, j + ahead)
        s = (i * steps + j + ahead) % n_slots
        e = te_ref[tile]
        wanted = tile < n_used

        @pl.when(jnp.logical_and(wanted, step < n_up))
        def _():
            col = pl.multiple_of(step * tf, tf)
            weight_copy(wgu_hbm.at[e, :, pl.ds(col, tf)], s, 0).start()
            weight_copy(wgu_hbm.at[e, :, pl.ds(f + col, tf)], s, 1).start()

        @pl.when(jnp.logical_and(wanted, step >= n_up))
        def _():
            col = pl.multiple_of((step - n_up) * 2 * tf, 2 * tf)
            weight_copy(wd_hbm.at[e, :, pl.ds(col, tf)], s, 0).start()
            weight_copy(wd_hbm.at[e, :, pl.ds(col + tf, tf)], s, 1).start()

    @pl.when(jnp.logical_and(i == 0, j == 0))
    def _():
        for ahead in range(n_slots - 1):
            request_weights(ahead)

    @pl.when(used)
    def _():
        request_weights(n_slots - 1)
        for piece in range(2):
            weight_copy(wd_hbm.at[0, :, pl.ds(0, tf)], wslot, piece).wait()

    def row_copy(q, r):
        return pltpu.make_async_copy(x_hbm.at[rows_ref[q]], xbuf.at[r], sem)

    def gather_next_rows():
        r0 = pl.multiple_of(j * rows_per_step, rows_per_step)
        q = q0_ref[i + 1] + r0
        stride = next(s for s in range(rows_per_step // 4 + 1, rows_per_step + 2)
                      if math.gcd(s, rows_per_step) == 1)
        for n in range(rows_per_step):
            r = n * stride % rows_per_step
            row_copy(q + r, r0 + r).start(priority=ROW_DMA_QUEUE)

    @pl.when(jnp.logical_and(i == 0, j == 0))
    def _():
        def body(r, carry):
            row_copy(q0_ref[0] + r, r).start(priority=ROW_DMA_QUEUE)
            return carry
        lax.fori_loop(0, tm, body, 0)

    @pl.when(jnp.logical_and(i <= n_used, j == 0))
    def _():
        pltpu.make_async_copy(xbuf, xbuf, sem).wait()

    @pl.when(jnp.logical_and(used, j == 0))
    def _():
        for s in range(tm // sub):
            r = slice(s * sub, (s + 1) * sub)
            lo, hi = _unpack_bf16_pairs(xbuf[r, :])
            xb_ref[r, :half] = lo
            xb_ref[r, half:] = hi

    def activation(x, wg, wu):
        gate = jnp.dot(x, wg, preferred_element_type=F32) + bg_ref[0]
        up = jnp.dot(x, wu, preferred_element_type=F32) + bu_ref[0]
        gate = jnp.minimum(gate, SWIGLU_LIMIT)
        up = jnp.clip(up, -SWIGLU_LIMIT, SWIGLU_LIMIT)
        glu = gate * jax.nn.sigmoid(SWIGLU_ALPHA * gate)
        return ((up + 1.0) * glu).astype(BF16)

    def down(rows, wd, cols):
        act = jnp.concatenate([act_ref[c, rows, :] for c in range(n_up)], axis=1)
        return jnp.dot(act, wd, preferred_element_type=F32) + bd_ref[0, :, cols]

    whole = jnp.logical_and(used, nv > tm - sub)
    partial = jnp.logical_and(used, nv <= tm - sub)

    @pl.when(jnp.logical_and(whole, up_phase))
    def _():
        gather_next_rows()
        wg = wring[wslot, 0].astype(BF16)
        wu = wring[wslot, 1].astype(BF16)
        for part in range(2):
            rows = slice(part * tm // 2, (part + 1) * tm // 2)
            act_ref[j, rows, :] = activation(xb_ref[rows, :], wg, wu)

    @pl.when(jnp.logical_and(whole, jnp.logical_not(up_phase)))
    def _():
        for piece in range(2):
            cols = slice(piece * tf, (piece + 1) * tf)
            o_ref[:, cols] = down(slice(None), wring[wslot, piece].astype(BF16), cols)

    for s in range(tm // sub):
        r = slice(s * sub, (s + 1) * sub)
        live = jnp.logical_and(partial, s * sub < nv)
        dead = jnp.logical_not(jnp.logical_or(whole, live))

        @pl.when(jnp.logical_and(live, up_phase))
        def _():
            if s == 0:
                gather_next_rows()
            act_ref[j, r, :] = activation(xb_ref[r, :], wring[wslot, 0].astype(BF16),
                                          wring[wslot, 1].astype(BF16))

        @pl.when(jnp.logical_and(live, jnp.logical_not(up_phase)))
        def _():
            for piece in range(2):
                cols = slice(piece * tf, (piece + 1) * tf)
                o_ref[r, cols] = down(r, wring[wslot, piece].astype(BF16), cols)

        @pl.when(jnp.logical_and(dead, jnp.logical_not(up_phase)))
        def _():
            o_ref[r, :] = jnp.zeros((sub, o_ref.shape[1]), F32)


def _experts(tile_expert, tile_valid, tile_q0, n_used, sorted_rows, x_packed,
             w_gate_up, b_gate_up, w_down, b_down, *, n_tiles):
    tm, sub = EXPERT_TILE, EXPERT_SUB
    half = x_packed.shape[1]
    ne, d, f2 = w_gate_up.shape
    f = f2 // 2
    assert d == 2 * half
    tf = EXPERT_COLS
    tn = 2 * tf
    n_up, n_down = f // tf, d // tn
    assert tm % n_up == 0 and tm % sub == 0
    assert f == d

    def up_chunk(i, j, nt):
        return jnp.where(i < nt[0], jnp.minimum(j, n_up - 1), n_up - 1)

    def down_chunk(i, j, nt):
        return jnp.where(i < nt[0], jnp.maximum(j - n_up, 0), 0)

    b_gu = b_gate_up.reshape(ne, 1, f2)
    b_d = b_down.reshape(ne, 1, d)
    return pl.pallas_call(
        functools.partial(_expert_kernel, tm=tm, sub=sub, n_up=n_up, n_down=n_down),
        grid_spec=pltpu.PrefetchScalarGridSpec(
            num_scalar_prefetch=5,
            grid=(n_tiles, n_up + n_down),
            in_specs=[
                pl.BlockSpec(memory_space=pl.ANY),
                pl.BlockSpec(memory_space=pl.ANY),
                pl.BlockSpec(memory_space=pl.ANY),
                pl.BlockSpec((1, 1, tf), lambda i, j, te, nv, q0, nt, rows: (te[i], 0, up_chunk(i, j, nt))),
                pl.BlockSpec((1, 1, tf), lambda i, j, te, nv, q0, nt, rows: (te[i], 0, n_up + up_chunk(i, j, nt))),
                pl.BlockSpec((1, 1, tn), lambda i, j, te, nv, q0, nt, rows: (te[i], 0, down_chunk(i, j, nt))),
            ],
            out_specs=pl.BlockSpec((tm, tn), lambda i, j, te, nv, q0, nt, rows: (i, jnp.maximum(j - n_up, 0))),
            scratch_shapes=[
                pltpu.VMEM((tm, half), jnp.uint32),
                pltpu.VMEM((tm, d), BF16),
                pltpu.VMEM((n_up, tm, tf), BF16),
                pltpu.VMEM((EXPERT_WEIGHT_SLOTS, 2, d, tf), F32),
                pltpu.SemaphoreType.DMA,
                pltpu.SemaphoreType.DMA((EXPERT_WEIGHT_SLOTS,)),
            ],
        ),
        out_shape=jax.ShapeDtypeStruct((n_tiles * tm, d), F32),
        compiler_params=_compiler_params(("arbitrary", "arbitrary")),
        name="experts",
    )(tile_expert, tile_valid, tile_q0, n_used, sorted_rows, x_packed,
      w_gate_up, w_down, b_gu, b_gu, b_d)


def _combine_kernel(dest_ref, h1_ref, gate_ref, y_hbm, o_ref, buf, sem, *, tt, tiles_per_batch,
                    tok_per_batch, n_steps):
    t = pl.program_id(0)
    slot = t % 2

    def request_rows(step, s):
        base = ((step // tiles_per_batch) * tok_per_batch + N_META + (step % tiles_per_batch) * tt) * TOP_K
        for r in range(tt):
            for k in range(TOP_K):
                pltpu.make_async_copy(y_hbm.at[dest_ref[base + r * TOP_K + k]], buf.at[s, k, r],
                                      sem.at[s]).start(priority=ROW_DMA_QUEUE)

    def wait_rows(s):
        pltpu.make_async_copy(buf.at[s], buf.at[s], sem.at[s]).wait()

    @pl.when(t == 0)
    def _():
        request_rows(0, 0)

    wait_rows(slot)
    request_rows(jnp.minimum(t + 1, n_steps - 1), 1 - slot)
    g = gate_ref[0]
    moe = buf[slot, 0] * g[:, 0:1]
    for k in range(1, TOP_K):
        moe = moe + buf[slot, k] * g[:, k:k + 1]
    o_ref[0] = h1_ref[0] + moe

    @pl.when(t == n_steps - 1)
    def _():
        wait_rows(1 - slot)


def _combine(dest_flat, h1, gates, y_sorted, *, tt):
    b, lp, d = h1.shape
    s = lp - CHUNK
    tok_per_batch = s + N_META
    tiles = s // tt
    off = CHUNK // tt
    gates_tok = gates[:, N_META:, :]
    n_steps = b * tiles
    return pl.pallas_call(
        functools.partial(_combine_kernel, tt=tt, tiles_per_batch=tiles, tok_per_batch=tok_per_batch,
                          n_steps=n_steps),
        grid_spec=pltpu.PrefetchScalarGridSpec(
            num_scalar_prefetch=1,
            grid=(n_steps,),
            in_specs=[
                pl.BlockSpec((1, tt, d), lambda t, dest: (t // tiles, off + t % tiles, 0)),
                pl.BlockSpec((1, tt, TOP_K), lambda t, dest: (t // tiles, t % tiles, 0)),
                pl.BlockSpec(memory_space=pl.ANY),
            ],
            out_specs=pl.BlockSpec((1, tt, d), lambda t, dest: (t // tiles, t % tiles, 0)),
            scratch_shapes=[pltpu.VMEM((2, TOP_K, tt, d), F32), pltpu.SemaphoreType.DMA((2,))],
        ),
        out_shape=jax.ShapeDtypeStruct((b, s, d), F32),
        compiler_params=_compiler_params(("arbitrary",)),
        name="combine",
    )(dest_flat, h1, gates_tok, y_sorted)


def _lookup(table, idx):
    n = table.shape[0]
    hit = idx[..., None] == jnp.arange(n, dtype=idx.dtype)
    return jnp.sum(jnp.where(hit, table, 0), axis=-1)


def _routing(top_idx, rank, counts, lp, n_pad, *, tm):
    b, l, k = top_idx.shape
    ne = counts.shape[0]
    p = b * l * k
    n_tiles = p // tm + ne + 1
    tiles_per = (counts + tm - 1) // tm
    tile_end = jnp.cumsum(tiles_per)
    tile_start = tile_end - tiles_per
    n_used = tile_end[-1]
    starts = jnp.cumsum(counts) - counts
    tile_id = jnp.arange(n_tiles, dtype=jnp.int32)
    tile_expert = jnp.minimum(jnp.sum(tile_end[None, :] <= tile_id[:, None], axis=-1), ne - 1).astype(jnp.int32)
    within = tile_id - _lookup(tile_start, tile_expert)
    tile_valid = jnp.where(tile_id < n_used, jnp.clip(_lookup(counts, tile_expert) - within * tm, 0, tm), 0)
    tile_q0 = jnp.where(tile_id < n_used, _lookup(starts, tile_expert) + within * tm, 0)
    tok = jnp.arange(b * l, dtype=jnp.int32)
    tok_bits = max(1, (b * l - 1).bit_length())
    keys = jnp.sort((top_idx.reshape(b * l, k) << tok_bits | tok[:, None]).reshape(p))
    sorted_tok = keys & ((1 << tok_bits) - 1)
    sorted_rows = (sorted_tok // l) * lp + n_pad + sorted_tok % l
    sorted_rows = jnp.concatenate([sorted_rows, jnp.arange(tm, dtype=jnp.int32) % (b * lp)])
    dest = _lookup(tile_start * tm, top_idx) + rank
    return (tile_expert, tile_valid.astype(jnp.int32), tile_q0.astype(jnp.int32),
            n_used.astype(jnp.int32).reshape(1), sorted_rows.astype(jnp.int32),
            dest.astype(jnp.int32).reshape(p), n_tiles)


def kernel(x, meta_tokens, attn_norm_gain, w_in, ret_decay_fwd, ret_decay_bwd, ret_out_gain,
           diff_q_gain, diff_k_gain, diff_lambda_q1, diff_lambda_k1, diff_lambda_q2, diff_lambda_k2,
           diff_out_gain, w_out, ffn_norm_gain, w_router, b_router, w_gate_up, b_gate_up,
           w_down, b_down):
    b, s, d = x.shape
    assert s % CHUNK == 0 and w_in.shape[0] == 1, "single layer, token count a chunk multiple"
    n_pad = CHUNK - N_META
    lp = s + CHUNK
    l = s + N_META

    hp, u = _prep(x, meta_tokens.astype(x.dtype), attn_norm_gain[0])
    z = _in_proj(u.reshape(b * lp, d), w_in[0].astype(BF16), tm=lp // 4, tn=1024)
    z = z.reshape(b, lp, -1)

    lg_f = jnp.log1p(-jnp.exp2(-ret_decay_fwd[0].astype(F32)))
    lg_b = jnp.log1p(-jnp.exp2(-ret_decay_bwd[0].astype(F32)))
    ret = _retention(z, lg_f, lg_b, ret_out_gain[0])

    lam = (jnp.exp(jnp.sum(diff_lambda_q1[0].astype(F32) * diff_lambda_k1[0].astype(F32)))
           - jnp.exp(jnp.sum(diff_lambda_q2[0].astype(F32) * diff_lambda_k2[0].astype(F32)))
           + LAMBDA_INIT).reshape(1)
    slopes = jnp.exp2(-8.0 * jnp.arange(1, DIFF_HEADS + 1, dtype=F32) / DIFF_HEADS)
    att = _attention(z, lam, slopes, diff_q_gain[0], diff_k_gain[0], diff_out_gain[0], tq=384, tk=384)

    h1, xt, route, counts = _out_proj(
        ret.reshape(b * lp, -1), att.reshape(b * lp, -1), hp.reshape(b * lp, d),
        w_out[0].astype(BF16), ffn_norm_gain[0], w_router[0], b_router[0], tm=lp // 8, lp=lp, n_pad=n_pad)

    route = route.reshape(b, lp, 128)[:, n_pad:, :]
    top_idx = route[:, :, :TOP_K].astype(jnp.int32)
    gates = route[:, :, TOP_K:2 * TOP_K]
    rank = route[:, :, 2 * TOP_K:3 * TOP_K].astype(jnp.int32)

    tile_expert, tile_valid, tile_q0, n_used, sorted_rows, dest, n_tiles = _routing(
        top_idx, rank, counts.reshape(-1).astype(jnp.int32), lp, n_pad, tm=EXPERT_TILE)
    ys = _experts(tile_expert, tile_valid, tile_q0, n_used, sorted_rows, xt,
                  w_gate_up[0], b_gate_up[0], w_down[0], b_down[0], n_tiles=n_tiles)
    return _combine(dest, h1.reshape(b, lp, d), gates, ys, tt=CHUNK)
```

```python
import functools
import math

import jax
import jax.numpy as jnp
from jax import lax
from jax.experimental import pallas as pl
from jax.experimental.pallas import tpu as pltpu

N_META = 16
CHUNK = 128
RET_HEADS = 4
RET_QK_DIM = 128
RET_V_DIM = 256
DIFF_HEADS = 4
DIFF_QK_DIM = 128
DIFF_V_DIM = 256
TOP_K = 4
SWIGLU_ALPHA = 1.702
SWIGLU_LIMIT = 7.0
NORM_EPS = 1e-6
RET_DECAY_BASE = 5.0
LAMBDA_INIT = 0.8 - 0.6 * math.exp(-0.3 * 0)
NEG_BIG = -1e30

V7X_VMEM_BYTES = 64 * 1024 * 1024
VMEM_LIMIT_BYTES = V7X_VMEM_BYTES - 8 * 1024 * 1024

F32 = jnp.float32
BF16 = jnp.bfloat16

ROW_DMA_QUEUE = 1


def _compiler_params(semantics):
    return pltpu.CompilerParams(dimension_semantics=semantics, vmem_limit_bytes=VMEM_LIMIT_BYTES)


def _prep_kernel(x_ref, meta_ref, g_ref, hp_ref, u_ref, *, n_pad):
    c = pl.program_id(1)

    def emit(h):
        hp_ref[0] = h
        ms = jnp.mean(h * h, axis=-1, keepdims=True)
        u_ref[0] = (h * lax.rsqrt(ms + NORM_EPS) * g_ref[...]).astype(u_ref.dtype)

    @pl.when(c == 0)
    def _():
        d = meta_ref.shape[-1]
        emit(jnp.concatenate([jnp.zeros((n_pad, d), F32), meta_ref[...]], axis=0))

    @pl.when(c > 0)
    def _():
        emit(x_ref[0])


def _prep(x, meta, gain):
    b, s, d = x.shape
    n_pad = CHUNK - N_META
    nc = s // CHUNK + 1
    lp = nc * CHUNK
    return pl.pallas_call(
        functools.partial(_prep_kernel, n_pad=n_pad),
        grid=(b, nc),
        in_specs=[
            pl.BlockSpec((1, CHUNK, d), lambda i, c: (i, jnp.maximum(c - 1, 0), 0)),
            pl.BlockSpec((N_META, d), lambda i, c: (0, 0)),
            pl.BlockSpec((1, d), lambda i, c: (0, 0)),
        ],
        out_specs=[
            pl.BlockSpec((1, CHUNK, d), lambda i, c: (i, c, 0)),
            pl.BlockSpec((1, CHUNK, d), lambda i, c: (i, c, 0)),
        ],
        out_shape=[
            jax.ShapeDtypeStruct((b, lp, d), F32),
            jax.ShapeDtypeStruct((b, lp, d), BF16),
        ],
        compiler_params=_compiler_params(("parallel", "arbitrary")),
        name="prep",
    )(x, meta, gain.reshape(1, d))


def _matmul_kernel(a_ref, b_ref, o_ref):
    o_ref[...] = jnp.dot(a_ref[...], b_ref[...], preferred_element_type=F32).astype(o_ref.dtype)


def _in_proj(u2d, w_bf16, *, tm, tn):
    m, k = u2d.shape
    n = w_bf16.shape[1]
    return pl.pallas_call(
        _matmul_kernel,
        grid=(m // tm, n // tn),
        in_specs=[
            pl.BlockSpec((tm, k), lambda i, j: (i, 0)),
            pl.BlockSpec((k, tn), lambda i, j: (0, j)),
        ],
        out_specs=pl.BlockSpec((tm, tn), lambda i, j: (i, j)),
        out_shape=jax.ShapeDtypeStruct((m, n), BF16),
        compiler_params=_compiler_params(("parallel", "arbitrary")),
        name="in_proj",
    )(u2d, w_bf16)


def _retention_kernel(lgf_ref, lgb_ref, q_ref, k_ref, v_ref, g_ref, gain_ref, o_ref,
                      ob_ref, sf_ref, sb_ref, dec_ref, *, nc, heads):
    c = CHUNK
    dk, dv = RET_QK_DIM, RET_V_DIM
    scale = dk ** -0.5
    row = lax.broadcasted_iota(jnp.int32, (c, c), 0)
    col = lax.broadcasted_iota(jnp.int32, (c, c), 1)
    rel = (row - col).astype(F32)
    pos = lax.broadcasted_iota(jnp.int32, (c, 1), 0).astype(F32)
    q_dec_f, k_dec_f, q_dec_b, k_dec_b, chunk_f, chunk_b = [], [], [], [], [], []
    for hh in range(heads):
        lgf = lgf_ref[pl.program_id(1) * heads + hh]
        lgb = lgb_ref[pl.program_id(1) * heads + hh]
        dec_ref[hh] = jnp.where(rel >= 0, jnp.exp(lgf * jnp.maximum(rel, 0.0)),
                                jnp.exp(lgb * jnp.maximum(-rel, 0.0))) * scale
        q_dec_f.append(jnp.exp(lgf * (pos + 1.0)))
        k_dec_f.append(jnp.exp(lgf * (c - 1.0 - pos)) * scale)
        q_dec_b.append(jnp.exp(lgb * (c - pos)))
        k_dec_b.append(jnp.exp(lgb * pos) * scale)
        chunk_f.append(jnp.exp(lgf * c))
        chunk_b.append(jnp.exp(lgb * c))

    def rows(n):
        return pl.ds(pl.multiple_of(n * c, c), c)

    def qk_cols(hh):
        return slice(hh * dk, (hh + 1) * dk)

    def v_cols(hh):
        return slice(hh * dv, (hh + 1) * dv)

    def state_update(s_ref, hh, kn, vn, k_dec, chunk_decay):
        kd_t = (kn.astype(F32) * k_dec).T.astype(BF16)
        s_ref[hh] = s_ref[hh] * chunk_decay + jnp.dot(kd_t, vn, preferred_element_type=F32)

    sb_ref[...] = jnp.zeros_like(sb_ref)
    sf_ref[...] = jnp.zeros_like(sf_ref)

    def bwd_body(t, carry):
        n = nc - 1 - t
        for hh in range(heads):
            qn = q_ref[0, rows(n), qk_cols(hh)]
            kn = k_ref[0, rows(n), qk_cols(hh)]
            vn = v_ref[0, rows(n), v_cols(hh)]
            qd = (qn.astype(F32) * q_dec_b[hh]).astype(BF16)
            ob_ref[rows(n), v_cols(hh)] = jnp.dot(qd, sb_ref[hh].astype(BF16), preferred_element_type=F32)
            state_update(sb_ref, hh, kn, vn, k_dec_b[hh], chunk_b[hh])
        return carry

    lax.fori_loop(0, nc, bwd_body, 0)

    def fwd_body(n, carry):
        for hh in range(heads):
            qn = q_ref[0, rows(n), qk_cols(hh)]
            kn = k_ref[0, rows(n), qk_cols(hh)]
            vn = v_ref[0, rows(n), v_cols(hh)]
            sc = lax.dot_general(qn, kn, (((1,), (1,)), ((), ())), preferred_element_type=F32)
            inner = jnp.dot((sc * dec_ref[hh]).astype(BF16), vn, preferred_element_type=F32)
            qd = (qn.astype(F32) * q_dec_f[hh]).astype(BF16)
            cross = jnp.dot(qd, sf_ref[hh].astype(BF16), preferred_element_type=F32)
            o = inner + cross + ob_ref[rows(n), v_cols(hh)]
            state_update(sf_ref, hh, kn, vn, k_dec_f[hh], chunk_f[hh])
            o = o * lax.rsqrt(jnp.mean(o * o, axis=-1, keepdims=True) + NORM_EPS) * gain_ref[:, v_cols(hh)]
            gate = g_ref[0, rows(n), v_cols(hh)].astype(F32)
            o_ref[0, rows(n), v_cols(hh)] = (o * (gate * jax.nn.sigmoid(gate))).astype(o_ref.dtype)
        return carry

    lax.fori_loop(0, nc, fwd_body, 0)


def _retention(z, lg_f, lg_b, ret_gain):
    b, lp, _ = z.shape
    nc = lp // CHUNK
    dk, dv, nh = RET_QK_DIM, RET_V_DIM, RET_HEADS
    heads = 2
    groups = nh // heads
    k_blk0 = groups
    v_blk0 = 2 * nh * dk // (heads * dv)
    g_blk0 = v_blk0 + groups
    smem = pl.BlockSpec(memory_space=pltpu.SMEM)
    return pl.pallas_call(
        functools.partial(_retention_kernel, nc=nc, heads=heads),
        grid=(b, groups),
        in_specs=[
            smem, smem,
            pl.BlockSpec((1, lp, heads * dk), lambda i, h: (i, 0, h)),
            pl.BlockSpec((1, lp, heads * dk), lambda i, h: (i, 0, k_blk0 + h)),
            pl.BlockSpec((1, lp, heads * dv), lambda i, h: (i, 0, v_blk0 + h)),
            pl.BlockSpec((1, lp, heads * dv), lambda i, h: (i, 0, g_blk0 + h)),
            pl.BlockSpec((1, heads * dv), lambda i, h: (0, h)),
        ],
        out_specs=pl.BlockSpec((1, lp, heads * dv), lambda i, h: (i, 0, h)),
        out_shape=jax.ShapeDtypeStruct((b, lp, nh * dv), BF16),
        scratch_shapes=[
            pltpu.VMEM((lp, heads * dv), F32),
            pltpu.VMEM((heads, dk, dv), F32),
            pltpu.VMEM((heads, dk, dv), F32),
            pltpu.VMEM((heads, CHUNK, CHUNK), F32),
        ],
        compiler_params=_compiler_params(("parallel", "arbitrary")),
        name="retention",
    )(lg_f, lg_b, z, z, z, z, ret_gain.reshape(1, nh * dv))


def _attn_kernel(lam_ref, slope_ref, q_ref, k_ref, v_ref, qg_ref, kg_ref, og_ref, o_ref,
                 kaug_ref, corr_ref, qaug_ref, s_ref, p_ref, m_ref, *, tq, tk, n_pad):
    h = pl.program_id(1)
    qi = pl.program_id(2)
    lp = k_ref.shape[1]
    d = DIFF_QK_DIM
    n_chunks = lp // tk
    lanes = 128
    lam = lam_ref[0]
    log2e = math.log2(math.e)
    slope = slope_ref[h] * log2e
    contract_last = (((1,), (1,)), ((), ()))

    def qk_norm(x, gain):
        return x * lax.rsqrt(jnp.mean(x * x, axis=-1, keepdims=True) + NORM_EPS) * gain

    def split3(v):
        hi = v.astype(BF16).astype(F32)
        mid = (v - hi).astype(BF16).astype(F32)
        lo = (v - hi - mid).astype(BF16).astype(F32)
        return [hi, mid, lo]

    def extra_lanes(cols):
        rows = cols[0].shape[0]
        lane = lax.broadcasted_iota(jnp.int32, (rows, d), 1)
        out = jnp.zeros((rows, d), F32)
        for t, col in enumerate(cols):
            out = jnp.where(lane == t, col, out)
        return out.astype(BF16)

    @pl.when(qi == 0)
    def _():
        for c in range(n_chunks):
            r = slice(c * tk, (c + 1) * tk)
            kk = k_ref[0, r, :].astype(F32)
            k_row = c * tk + lax.broadcasted_iota(jnp.int32, (tk, 1), 0)
            real = k_row >= CHUNK
            one = jnp.where(real, 1.0, 0.0)
            first = jnp.where(k_row < n_pad, NEG_BIG, 0.0)
            sk = [jnp.where(real, t, 0.0) for t in split3(slope * k_row.astype(F32))]
            extra = extra_lanes([sk[0] + first, sk[1], sk[2], one, one, one])
            for m in range(2):
                kaug_ref[m, r, :d] = qk_norm(kk[:, m * d:(m + 1) * d], kg_ref[...]).astype(BF16)
                kaug_ref[m, r, d:] = extra
        i_loc = lax.broadcasted_iota(jnp.int32, (tq, tk), 0)
        j_loc = lax.broadcasted_iota(jnp.int32, (tq, tk), 1)
        corr_ref[...] = (2.0 * slope) * jnp.maximum(j_loc - i_loc, 0).astype(F32)

    qq = q_ref[0].astype(F32)
    qs = [(qk_norm(qq[:, m * d:(m + 1) * d], qg_ref[...]) * (d ** -0.5 * log2e)).astype(BF16)
          for m in range(2)]
    q_row = qi * tq + lax.broadcasted_iota(jnp.int32, (tq, 1), 0)
    real_q = q_row >= CHUNK
    sq = [jnp.where(real_q, t, 0.0) for t in split3(slope * q_row.astype(F32))]
    one = jnp.where(real_q, 1.0, 0.0)
    extra_before = extra_lanes([one, one, one] + [-t for t in sq])
    extra_after = extra_lanes([-one, -one, -one] + sq)
    for m in range(2):
        qaug_ref[0, m] = jnp.concatenate([qs[m], extra_before], axis=1)
        qaug_ref[1, m] = jnp.concatenate([qs[m], extra_after], axis=1)

    def lane_fold(x, op):
        out = x[:, :lanes]
        for t in range(1, x.shape[1] // lanes):
            out = op(out, x[:, t * lanes:(t + 1) * lanes])
        return out

    m_ref[...] = jnp.full_like(m_ref, NEG_BIG)
    for c in range(n_chunks):
        r = slice(c * tk, (c + 1) * tk)
        side = jnp.where(c <= qi, 0, 1)
        for m in range(2):
            s = lax.dot_general(qaug_ref[side, m], kaug_ref[m, r, :], contract_last,
                                preferred_element_type=F32)
            s_ref[m, c] = s
            m_ref[m] = jnp.maximum(m_ref[m], jnp.where(c == qi, NEG_BIG, lane_fold(s, jnp.maximum)))
    for m in range(2):
        s = s_ref[m, qi] - corr_ref[...]
        s_ref[m, qi] = s
        m_ref[m] = jnp.maximum(m_ref[m], lane_fold(s, jnp.maximum))

    @pl.when(qi == 0)
    def _():
        q_pos = lax.broadcasted_iota(jnp.int32, (tq, tk), 0)
        k_pos = lax.broadcasted_iota(jnp.int32, (tq, tk), 1)
        dist = jnp.abs(q_pos - k_pos).astype(F32)
        both_real = jnp.logical_and(q_pos >= CHUNK, k_pos >= CHUNK)
        bias = jnp.where(both_real, -slope * dist, 0.0)
        for m in range(2):
            s = lax.dot_general(qs[m], kaug_ref[m, :tk, :d], contract_last, preferred_element_type=F32)
            s = jnp.where(k_pos >= n_pad, s + bias, NEG_BIG)
            s_ref[m, 0] = s
            m_ref[m] = jnp.maximum(m_ref[m], lane_fold(s, jnp.maximum))

    band = lanes
    inv_l = []
    for m in range(2):
        for rb in range(tq // band):
            rows = slice(rb * band, (rb + 1) * band)
            row_max = jnp.max(m_ref[m, rows, :], axis=-1, keepdims=True)
            shift = jnp.concatenate([jnp.broadcast_to(row_max, (band, lanes))] * (tk // lanes), axis=1)
            l_part = jnp.zeros((band, lanes), F32)
            for c in range(n_chunks):
                e = jnp.exp2(s_ref[m, c, rows, :] - shift)
                l_part = l_part + lane_fold(e, jnp.add)
                p_ref[m * tq + rb * band:m * tq + (rb + 1) * band, c * tk:(c + 1) * tk] = e.astype(BF16)
            inv_l.append(1.0 / jnp.sum(l_part, axis=-1, keepdims=True))
    acc = jnp.dot(p_ref[...], v_ref[0], preferred_element_type=F32) * jnp.concatenate(inv_l, axis=0)

    o = acc[:tq] - lam * acc[tq:]
    o = o * lax.rsqrt(jnp.mean(o * o, axis=-1, keepdims=True) + NORM_EPS) * og_ref[...]
    o_ref[0] = (o * (1.0 - LAMBDA_INIT)).astype(o_ref.dtype)


def _attention(z, lam, slopes, q_gain, k_gain, out_gain, *, tq, tk):
    b, lp, _ = z.shape
    nh, d, dv = DIFF_HEADS, DIFF_QK_DIM, DIFF_V_DIM
    assert tq == tk and tq >= CHUNK and lp % tq == 0
    ret_cols = 2 * RET_HEADS * RET_QK_DIM + 2 * RET_HEADS * RET_V_DIM
    q_blk0 = ret_cols // (2 * d)
    k_blk0 = q_blk0 + nh
    v_blk0 = k_blk0 + nh
    smem = pl.BlockSpec(memory_space=pltpu.SMEM)
    return pl.pallas_call(
        functools.partial(_attn_kernel, tq=tq, tk=tk, n_pad=CHUNK - N_META),
        grid=(b, nh, lp // tq),
        in_specs=[
            smem, smem,
            pl.BlockSpec((1, tq, 2 * d), lambda i, h, q: (i, q, q_blk0 + h)),
            pl.BlockSpec((1, lp, 2 * d), lambda i, h, q: (i, 0, k_blk0 + h)),
            pl.BlockSpec((1, lp, dv), lambda i, h, q: (i, 0, v_blk0 + h)),
            pl.BlockSpec((1, d), lambda i, h, q: (0, 0)),
            pl.BlockSpec((1, d), lambda i, h, q: (0, 0)),
            pl.BlockSpec((1, dv), lambda i, h, q: (0, h)),
        ],
        out_specs=pl.BlockSpec((1, tq, dv), lambda i, h, q: (i, q, h)),
        out_shape=jax.ShapeDtypeStruct((b, lp, nh * dv), BF16),
        scratch_shapes=[
            pltpu.VMEM((2, lp, 2 * d), BF16),
            pltpu.VMEM((tq, tk), F32),
            pltpu.VMEM((2, 2, tq, 2 * d), BF16),
            pltpu.VMEM((2, lp // tk, tq, tk), F32),
            pltpu.VMEM((2 * tq, lp), BF16),
            pltpu.VMEM((2, tq, 128), F32),
        ],
        compiler_params=_compiler_params(("parallel", "parallel", "arbitrary")),
        name="attention",
    )(lam, slopes, z, z, z, q_gain.reshape(1, d), k_gain.reshape(1, d), out_gain.reshape(1, nh * dv))


def _pack_bf16_pairs(x):
    n = x.shape[1] // 2
    xb = x.astype(BF16).astype(F32)
    lo = lax.bitcast_convert_type(xb[:, :n], jnp.uint32)
    hi = lax.bitcast_convert_type(xb[:, n:], jnp.uint32)
    return (hi & jnp.uint32(0xFFFF0000)) | (lo >> 16)


def _unpack_bf16_pairs(p):
    lo = lax.bitcast_convert_type(p << 16, F32).astype(BF16)
    hi = lax.bitcast_convert_type(p & jnp.uint32(0xFFFF0000), F32).astype(BF16)
    return lo, hi


def _out_proj_kernel(ret_ref, att_ref, hp_ref, w_ref, g_ref, wr_ref, br_ref,
                     h1_ref, xt_ref, route_ref, cnt_ref, run_ref, *, lp, n_pad):
    i = pl.program_id(0)

    @pl.when(i == 0)
    def _():
        run_ref[...] = jnp.zeros_like(run_ref)

    kr = ret_ref.shape[1]
    y = jnp.dot(ret_ref[...], w_ref[:kr, :], preferred_element_type=F32)
    y = y + jnp.dot(att_ref[...], w_ref[kr:, :], preferred_element_type=F32)
    h1 = hp_ref[...] + y
    h1_ref[...] = h1
    xt = h1 * lax.rsqrt(jnp.mean(h1 * h1, axis=-1, keepdims=True) + NORM_EPS) * g_ref[...]
    xt_ref[...] = _pack_bf16_pairs(xt)
    xt_hi = xt.astype(BF16)
    xt_lo = (xt - xt_hi.astype(F32)).astype(BF16)
    wr = wr_ref[...]
    wr_hi = wr.astype(BF16)
    wr_lo = (wr - wr_hi.astype(F32)).astype(BF16)
    logits = (jnp.dot(xt_hi, wr_hi, preferred_element_type=F32)
              + jnp.dot(xt_lo, wr_hi, preferred_element_type=F32)
              + jnp.dot(xt_hi, wr_lo, preferred_element_type=F32)) + br_ref[...]
    tm, ne = logits.shape
    lane = lax.broadcasted_iota(jnp.int32, (tm, ne), 1).astype(F32)
    out_lane = lax.broadcasted_iota(jnp.int32, route_ref.shape, 1)
    vals, idxs = [], []
    work = logits
    for _ in range(TOP_K):
        top = jnp.max(work, axis=-1, keepdims=True)
        idx = jnp.min(jnp.where(work == top, lane, float(ne)), axis=-1, keepdims=True)
        work = jnp.where(lane == idx, -jnp.inf, work)
        vals.append(top)
        idxs.append(idx)
    exps = [jnp.exp(v - vals[0]) for v in vals]
    denom = exps[0] + exps[1] + exps[2] + exps[3]

    local_row = (i * tm) % lp + lax.broadcasted_iota(jnp.int32, (tm, 1), 0)
    member = jnp.zeros((tm, ne), F32)
    for k in range(TOP_K):
        member = member + jnp.where(lane == idxs[k], 1.0, 0.0)
    member = jnp.where(local_row >= n_pad, member, 0.0)
    earlier = (lax.broadcasted_iota(jnp.int32, (tm, tm), 0)
               > lax.broadcasted_iota(jnp.int32, (tm, tm), 1))
    before = run_ref[...] + jnp.dot(jnp.where(earlier, 1.0, 0.0).astype(BF16), member.astype(BF16),
                                    preferred_element_type=F32)
    run_ref[...] = run_ref[...] + jnp.sum(member, axis=0, keepdims=True)
    cnt_ref[...] = run_ref[...]

    slab = jnp.zeros(route_ref.shape, F32)
    for k in range(TOP_K):
        rank = jnp.sum(jnp.where(lane == idxs[k], before, 0.0), axis=-1, keepdims=True)
        slab = jnp.where(out_lane == k, idxs[k], slab)
        slab = jnp.where(out_lane == TOP_K + k, exps[k] / denom, slab)
        slab = jnp.where(out_lane == 2 * TOP_K + k, rank, slab)
    route_ref[...] = slab


def _out_proj(ret2d, att2d, hp2d, w_bf16, gain, w_router, b_router, *, tm, lp, n_pad):
    m, kr = ret2d.shape
    d = hp2d.shape[1]
    ne = w_router.shape[1]
    assert lp % tm == 0
    row = lambda i: (i, 0)
    const = lambda i: (0, 0)
    return pl.pallas_call(
        functools.partial(_out_proj_kernel, lp=lp, n_pad=n_pad),
        grid=(m // tm,),
        in_specs=[
            pl.BlockSpec((tm, kr), row),
            pl.BlockSpec((tm, kr), row),
            pl.BlockSpec((tm, d), row),
            pl.BlockSpec((2 * kr, d), const),
            pl.BlockSpec((1, d), const),
            pl.BlockSpec((d, ne), const),
            pl.BlockSpec((1, ne), const),
        ],
        out_specs=[
            pl.BlockSpec((tm, d), row),
            pl.BlockSpec((tm, d // 2), row),
            pl.BlockSpec((tm, 128), row),
            pl.BlockSpec((1, ne), const),
        ],
        out_shape=[
            jax.ShapeDtypeStruct((m, d), F32),
            jax.ShapeDtypeStruct((m, d // 2), jnp.uint32),
            jax.ShapeDtypeStruct((m, 128), F32),
            jax.ShapeDtypeStruct((1, ne), F32),
        ],
        scratch_shapes=[pltpu.VMEM((1, ne), F32)],
        compiler_params=_compiler_params(("arbitrary",)),
        name="out_proj",
    )(ret2d, att2d, hp2d, w_bf16, gain.reshape(1, d), w_router, b_router.reshape(1, ne))


EXPERT_TILE = 2176
EXPERT_SUB = 1088
EXPERT_COLS = 256
EXPERT_WEIGHT_SLOTS = 3


def _expert_kernel(te_ref, nv_ref, q0_ref, nt_ref, rows_ref,
                   x_hbm, wgu_hbm, wd_hbm, bg_ref, bu_ref, bd_ref, o_ref,
                   xbuf, xb_ref, act_ref, wring, wbf, sem, wsem, *, tm, sub, n_up, n_down):
    i = pl.program_id(0)
    j = pl.program_id(1)
    n_used = nt_ref[0]
    used = i < n_used
    nv = nv_ref[i]
    half = xbuf.shape[1]
    tf = wring.shape[-1]
    f = wgu_hbm.shape[2] // 2
    steps = n_up + n_down
    n_slots = wring.shape[0]
    n_sub = tm // sub
    rows_per_step = tm // n_up
    up_phase = j < n_up
    wslot = (i * steps + j) % n_slots

    def weight_copy(src, s, piece):
        return pltpu.make_async_copy(src, wring.at[s, piece], wsem.at[s])

    def request_weights(ahead):
        wrap = j + ahead >= steps
        tile = jnp.where(wrap, i + 1, i)
        step = jnp.where(wrap, j + ahead - steps, j + ahead)
        s = (i * steps + j + ahead) % n_slots
        e = te_ref[tile]
        wanted = tile < n_used

        @pl.when(jnp.logical_and(wanted, step < n_up))
        def _():
            col = pl.multiple_of(step * tf, tf)
            weight_copy(wgu_hbm.at[e, :, pl.ds(col, tf)], s, 0).start()
            weight_copy(wgu_hbm.at[e, :, pl.ds(f + col, tf)], s, 1).start()

        @pl.when(jnp.logical_and(wanted, step >= n_up))
        def _():
            col = pl.multiple_of((step - n_up) * 2 * tf, 2 * tf)
            weight_copy(wd_hbm.at[e, :, pl.ds(col, tf)], s, 0).start()
            weight_copy(wd_hbm.at[e, :, pl.ds(col + tf, tf)], s, 1).start()

    @pl.when(jnp.logical_and(i == 0, j == 0))
    def _():
        for ahead in range(n_slots - 1):
            request_weights(ahead)

    @pl.when(used)
    def _():
        request_weights(n_slots - 1)
        for piece in range(2):
            weight_copy(wd_hbm.at[0, :, pl.ds(0, tf)], wslot, piece).wait()

    @pl.when(used)
    def _():
        for piece in range(2):
            wbf[piece] = wring[wslot, piece].astype(BF16)

    def row_copy(q, r):
        return pltpu.make_async_copy(x_hbm.at[rows_ref[q]], xbuf.at[r], sem)

    def gather_next_rows():
        r0 = pl.multiple_of(j * rows_per_step, rows_per_step)
        q = q0_ref[i + 1] + r0

        def body(n, carry):
            for u in range(8):
                r = n * 8 + u
                row_copy(q + r, r0 + r).start(priority=ROW_DMA_QUEUE)
            return carry
        lax.fori_loop(0, rows_per_step // 8, body, 0)

    def sub_rows(s):
        return pl.ds(pl.multiple_of(s * sub, sub), sub)

    @pl.when(jnp.logical_and(i == 0, j == 0))
    def _():
        def body(r, carry):
            row_copy(q0_ref[0] + r, r).start(priority=ROW_DMA_QUEUE)
            return carry
        lax.fori_loop(0, tm, body, 0)

    @pl.when(jnp.logical_and(i <= n_used, j == 0))
    def _():
        pltpu.make_async_copy(xbuf, xbuf, sem).wait()

    @pl.when(jnp.logical_and(used, j == 0))
    def _():
        def body(s, carry):
            lo, hi = _unpack_bf16_pairs(xbuf[sub_rows(s), :])
            xb_ref[sub_rows(s), :half] = lo
            xb_ref[sub_rows(s), half:] = hi
            return carry
        lax.fori_loop(0, n_sub, body, 0)

    n_live = jnp.where(used, (nv + sub - 1) // sub, 0)

    @pl.when(jnp.logical_and(used, up_phase))
    def _():
        gather_next_rows()

        def body(s, carry):
            x = xb_ref[sub_rows(s), :]
            gate = jnp.dot(x, wbf[0], preferred_element_type=F32) + bg_ref[0]
            up = jnp.dot(x, wbf[1], preferred_element_type=F32) + bu_ref[0]
            gate = jnp.minimum(gate, SWIGLU_LIMIT)
            up = jnp.clip(up, -SWIGLU_LIMIT, SWIGLU_LIMIT)
            glu = gate * jax.nn.sigmoid(SWIGLU_ALPHA * gate)
            act_ref[j, sub_rows(s), :] = ((up + 1.0) * glu).astype(BF16)
            return carry
        lax.fori_loop(0, n_live, body, 0)

    @pl.when(jnp.logical_not(up_phase))
    def _():
        def body(s, carry):
            act = jnp.concatenate([act_ref[c, sub_rows(s), :] for c in range(n_up)], axis=1)
            for piece in range(2):
                cols = slice(piece * tf, (piece + 1) * tf)
                o_ref[sub_rows(s), cols] = (jnp.dot(act, wbf[piece], preferred_element_type=F32)
                                            + bd_ref[0, :, cols])
            return carry
        lax.fori_loop(0, n_live, body, 0)

        def clear(s, carry):
            o_ref[sub_rows(s), :] = jnp.zeros((sub, o_ref.shape[1]), F32)
            return carry
        lax.fori_loop(n_live, n_sub, clear, 0)


def _experts(tile_expert, tile_valid, tile_q0, n_used, sorted_rows, x_packed,
             w_gate_up, b_gate_up, w_down, b_down, *, n_tiles):
    tm, sub = EXPERT_TILE, EXPERT_SUB
    half = x_packed.shape[1]
    ne, d, f2 = w_gate_up.shape
    f = f2 // 2
    assert d == 2 * half
    tf = EXPERT_COLS
    tn = 2 * tf
    n_up, n_down = f // tf, d // tn
    assert tm % n_up == 0 and tm % sub == 0
    assert f == d

    def up_chunk(i, j, nt):
        return jnp.where(i < nt[0], jnp.minimum(j, n_up - 1), n_up - 1)

    def down_chunk(i, j, nt):
        return jnp.where(i < nt[0], jnp.maximum(j - n_up, 0), 0)

    b_gu = b_gate_up.reshape(ne, 1, f2)
    b_d = b_down.reshape(ne, 1, d)
    return pl.pallas_call(
        functools.partial(_expert_kernel, tm=tm, sub=sub, n_up=n_up, n_down=n_down),
        grid_spec=pltpu.PrefetchScalarGridSpec(
            num_scalar_prefetch=5,
            grid=(n_tiles, n_up + n_down),
            in_specs=[
                pl.BlockSpec(memory_space=pl.ANY),
                pl.BlockSpec(memory_space=pl.ANY),
                pl.BlockSpec(memory_space=pl.ANY),
                pl.BlockSpec((1, 1, tf), lambda i, j, te, nv, q0, nt, rows: (te[i], 0, up_chunk(i, j, nt))),
                pl.BlockSpec((1, 1, tf), lambda i, j, te, nv, q0, nt, rows: (te[i], 0, n_up + up_chunk(i, j, nt))),
                pl.BlockSpec((1, 1, tn), lambda i, j, te, nv, q0, nt, rows: (te[i], 0, down_chunk(i, j, nt))),
            ],
            out_specs=pl.BlockSpec((tm, tn), lambda i, j, te, nv, q0, nt, rows: (i, jnp.maximum(j - n_up, 0))),
            scratch_shapes=[
                pltpu.VMEM((tm, half), jnp.uint32),
                pltpu.VMEM((tm, d), BF16),
                pltpu.VMEM((n_up, tm, tf), BF16),
                pltpu.VMEM((EXPERT_WEIGHT_SLOTS, 2, d, tf), F32),
                pltpu.VMEM((2, d, tf), BF16),
                pltpu.SemaphoreType.DMA,
                pltpu.SemaphoreType.DMA((EXPERT_WEIGHT_SLOTS,)),
            ],
        ),
        out_shape=jax.ShapeDtypeStruct((n_tiles * tm, d), F32),
        compiler_params=_compiler_params(("arbitrary", "arbitrary")),
        name="experts",
    )(tile_expert, tile_valid, tile_q0, n_used, sorted_rows, x_packed,
      w_gate_up, w_down, b_gu, b_gu, b_d)


def _combine_kernel(dest_ref, h1_ref, gate_ref, y_hbm, o_ref, buf, sem, *, tt, tiles_per_batch,
                    tok_per_batch, n_steps):
    t = pl.program_id(0)
    slot = t % 2

    def request_rows(step, s):
        base = ((step // tiles_per_batch) * tok_per_batch + N_META + (step % tiles_per_batch) * tt) * TOP_K
        for r in range(tt):
            for k in range(TOP_K):
                pltpu.make_async_copy(y_hbm.at[dest_ref[base + r * TOP_K + k]], buf.at[s, k, r],
                                      sem.at[s]).start(priority=ROW_DMA_QUEUE)

    def wait_rows(s):
        pltpu.make_async_copy(buf.at[s], buf.at[s], sem.at[s]).wait()

    @pl.when(t == 0)
    def _():
        request_rows(0, 0)

    wait_rows(slot)
    request_rows(jnp.minimum(t + 1, n_steps - 1), 1 - slot)
    g = gate_ref[0]
    moe = buf[slot, 0] * g[:, 0:1]
    for k in range(1, TOP_K):
        moe = moe + buf[slot, k] * g[:, k:k + 1]
    o_ref[0] = h1_ref[0] + moe

    @pl.when(t == n_steps - 1)
    def _():
        wait_rows(1 - slot)


def _combine(dest_flat, h1, gates, y_sorted, *, tt):
    b, lp, d = h1.shape
    s = lp - CHUNK
    tok_per_batch = s + N_META
    tiles = s // tt
    off = CHUNK // tt
    gates_tok = gates[:, N_META:, :]
    n_steps = b * tiles
    return pl.pallas_call(
        functools.partial(_combine_kernel, tt=tt, tiles_per_batch=tiles, tok_per_batch=tok_per_batch,
                          n_steps=n_steps),
        grid_spec=pltpu.PrefetchScalarGridSpec(
            num_scalar_prefetch=1,
            grid=(n_steps,),
            in_specs=[
                pl.BlockSpec((1, tt, d), lambda t, dest: (t // tiles, off + t % tiles, 0)),
                pl.BlockSpec((1, tt, TOP_K), lambda t, dest: (t // tiles, t % tiles, 0)),
                pl.BlockSpec(memory_space=pl.ANY),
            ],
            out_specs=pl.BlockSpec((1, tt, d), lambda t, dest: (t // tiles, t % tiles, 0)),
            scratch_shapes=[pltpu.VMEM((2, TOP_K, tt, d), F32), pltpu.SemaphoreType.DMA((2,))],
        ),
        out_shape=jax.ShapeDtypeStruct((b, s, d), F32),
        compiler_params=_compiler_params(("arbitrary",)),
        name="combine",
    )(dest_flat, h1, gates_tok, y_sorted)


def _lookup(table, idx):
    n = table.shape[0]
    hit = idx[..., None] == jnp.arange(n, dtype=idx.dtype)
    return jnp.sum(jnp.where(hit, table, 0), axis=-1)


def _routing(top_idx, rank, counts, lp, n_pad, *, tm):
    b, l, k = top_idx.shape
    ne = counts.shape[0]
    p = b * l * k
    n_tiles = p // tm + ne + 1
    tiles_per = (counts + tm - 1) // tm
    tile_end = jnp.cumsum(tiles_per)
    tile_start = tile_end - tiles_per
    n_used = tile_end[-1]
    starts = jnp.cumsum(counts) - counts
    tile_id = jnp.arange(n_tiles, dtype=jnp.int32)
    tile_expert = jnp.minimum(jnp.sum(tile_end[None, :] <= tile_id[:, None], axis=-1), ne - 1).astype(jnp.int32)
    within = tile_id - _lookup(tile_start, tile_expert)
    tile_valid = jnp.where(tile_id < n_used, jnp.clip(_lookup(counts, tile_expert) - within * tm, 0, tm), 0)
    tile_q0 = jnp.where(tile_id < n_used, _lookup(starts, tile_expert) + within * tm, 0)
    tok = jnp.arange(b * l, dtype=jnp.int32)
    tok_bits = max(1, (b * l - 1).bit_length())
    keys = jnp.sort((top_idx.reshape(b * l, k) << tok_bits | tok[:, None]).reshape(p))
    sorted_tok = keys & ((1 << tok_bits) - 1)
    sorted_rows = (sorted_tok // l) * lp + n_pad + sorted_tok % l
    sorted_rows = jnp.concatenate([sorted_rows, jnp.arange(tm, dtype=jnp.int32) % (b * lp)])
    dest = _lookup(tile_start * tm, top_idx) + rank
    return (tile_expert, tile_valid.astype(jnp.int32), tile_q0.astype(jnp.int32),
            n_used.astype(jnp.int32).reshape(1), sorted_rows.astype(jnp.int32),
            dest.astype(jnp.int32).reshape(p), n_tiles)


def kernel(x, meta_tokens, attn_norm_gain, w_in, ret_decay_fwd, ret_decay_bwd, ret_out_gain,
           diff_q_gain, diff_k_gain, diff_lambda_q1, diff_lambda_k1, diff_lambda_q2, diff_lambda_k2,
           diff_out_gain, w_out, ffn_norm_gain, w_router, b_router, w_gate_up, b_gate_up,
           w_down, b_down):
    b, s, d = x.shape
    assert s % CHUNK == 0 and w_in.shape[0] == 1, "single layer, token count a chunk multiple"
    n_pad = CHUNK - N_META
    lp = s + CHUNK
    l = s + N_META

    hp, u = _prep(x, meta_tokens.astype(x.dtype), attn_norm_gain[0])
    z = _in_proj(u.reshape(b * lp, d), w_in[0].astype(BF16), tm=lp // 4, tn=1024)
    z = z.reshape(b, lp, -1)

    lg_f = jnp.log1p(-jnp.exp2(-ret_decay_fwd[0].astype(F32)))
    lg_b = jnp.log1p(-jnp.exp2(-ret_decay_bwd[0].astype(F32)))
    ret = _retention(z, lg_f, lg_b, ret_out_gain[0])

    lam = (jnp.exp(jnp.sum(diff_lambda_q1[0].astype(F32) * diff_lambda_k1[0].astype(F32)))
           - jnp.exp(jnp.sum(diff_lambda_q2[0].astype(F32) * diff_lambda_k2[0].astype(F32)))
           + LAMBDA_INIT).reshape(1)
    slopes = jnp.exp2(-8.0 * jnp.arange(1, DIFF_HEADS + 1, dtype=F32) / DIFF_HEADS)
    att = _attention(z, lam, slopes, diff_q_gain[0], diff_k_gain[0], diff_out_gain[0], tq=384, tk=384)

    h1, xt, route, counts = _out_proj(
        ret.reshape(b * lp, -1), att.reshape(b * lp, -1), hp.reshape(b * lp, d),
        w_out[0].astype(BF16), ffn_norm_gain[0], w_router[0], b_router[0], tm=lp // 8, lp=lp, n_pad=n_pad)

    route = route.reshape(b, lp, 128)[:, n_pad:, :]
    top_idx = route[:, :, :TOP_K].astype(jnp.int32)
    gates = route[:, :, TOP_K:2 * TOP_K]
    rank = route[:, :, 2 * TOP_K:3 * TOP_K].astype(jnp.int32)

    tile_expert, tile_valid, tile_q0, n_used, sorted_rows, dest, n_tiles = _routing(
        top_idx, rank, counts.reshape(-1).astype(jnp.int32), lp, n_pad, tm=EXPERT_TILE)
    ys = _experts(tile_expert, tile_valid, tile_q0, n_used, sorted_rows, xt,
                  w_gate_up[0], b_gate_up[0], w_down[0], b_down[0], n_tiles=n_tiles)
    return _combine(dest, h1.reshape(b, lp, d), gates, ys, tt=CHUNK)
```

```python
import functools
import math

import jax
import jax.numpy as jnp
from jax import lax
from jax.experimental import pallas as pl
from jax.experimental.pallas import tpu as pltpu

N_META = 16
CHUNK = 128
RET_HEADS = 4
RET_QK_DIM = 128
RET_V_DIM = 256
DIFF_HEADS = 4
DIFF_QK_DIM = 128
DIFF_V_DIM = 256
TOP_K = 4
SWIGLU_ALPHA = 1.702
SWIGLU_LIMIT = 7.0
NORM_EPS = 1e-6
RET_DECAY_BASE = 5.0
LAMBDA_INIT = 0.8 - 0.6 * math.exp(-0.3 * 0)
NEG_BIG = -1e30

V7X_VMEM_BYTES = 64 * 1024 * 1024
VMEM_LIMIT_BYTES = V7X_VMEM_BYTES - 8 * 1024 * 1024

F32 = jnp.float32
BF16 = jnp.bfloat16

ROW_DMA_QUEUE = 1


def _compiler_params(semantics):
    return pltpu.CompilerParams(dimension_semantics=semantics, vmem_limit_bytes=VMEM_LIMIT_BYTES)


def _prep_kernel(x_ref, meta_ref, g_ref, hp_ref, u_ref, *, n_pad):
    c = pl.program_id(1)

    def emit(h):
        hp_ref[0] = h
        ms = jnp.mean(h * h, axis=-1, keepdims=True)
        u_ref[0] = (h * lax.rsqrt(ms + NORM_EPS) * g_ref[...]).astype(u_ref.dtype)

    @pl.when(c == 0)
    def _():
        d = meta_ref.shape[-1]
        emit(jnp.concatenate([jnp.zeros((n_pad, d), F32), meta_ref[...]], axis=0))

    @pl.when(c > 0)
    def _():
        emit(x_ref[0])


def _prep(x, meta, gain):
    b, s, d = x.shape
    n_pad = CHUNK - N_META
    nc = s // CHUNK + 1
    lp = nc * CHUNK
    return pl.pallas_call(
        functools.partial(_prep_kernel, n_pad=n_pad),
        grid=(b, nc),
        in_specs=[
            pl.BlockSpec((1, CHUNK, d), lambda i, c: (i, jnp.maximum(c - 1, 0), 0)),
            pl.BlockSpec((N_META, d), lambda i, c: (0, 0)),
            pl.BlockSpec((1, d), lambda i, c: (0, 0)),
        ],
        out_specs=[
            pl.BlockSpec((1, CHUNK, d), lambda i, c: (i, c, 0)),
            pl.BlockSpec((1, CHUNK, d), lambda i, c: (i, c, 0)),
        ],
        out_shape=[
            jax.ShapeDtypeStruct((b, lp, d), F32),
            jax.ShapeDtypeStruct((b, lp, d), BF16),
        ],
        compiler_params=_compiler_params(("parallel", "arbitrary")),
        name="prep",
    )(x, meta, gain.reshape(1, d))


def _matmul_kernel(a_ref, b_ref, o_ref):
    o_ref[...] = jnp.dot(a_ref[...], b_ref[...], preferred_element_type=F32).astype(o_ref.dtype)


def _in_proj(u2d, w_bf16, *, tm, tn):
    m, k = u2d.shape
    n = w_bf16.shape[1]
    return pl.pallas_call(
        _matmul_kernel,
        grid=(m // tm, n // tn),
        in_specs=[
            pl.BlockSpec((tm, k), lambda i, j: (i, 0)),
            pl.BlockSpec((k, tn), lambda i, j: (0, j)),
        ],
        out_specs=pl.BlockSpec((tm, tn), lambda i, j: (i, j)),
        out_shape=jax.ShapeDtypeStruct((m, n), BF16),
        compiler_params=_compiler_params(("parallel", "arbitrary")),
        name="in_proj",
    )(u2d, w_bf16)


def _retention_kernel(lgf_ref, lgb_ref, q_ref, k_ref, v_ref, g_ref, gain_ref, o_ref,
                      ob_ref, sf_ref, sb_ref, dec_ref, *, nc, heads):
    c = CHUNK
    dk, dv = RET_QK_DIM, RET_V_DIM
    scale = dk ** -0.5
    row = lax.broadcasted_iota(jnp.int32, (c, c), 0)
    col = lax.broadcasted_iota(jnp.int32, (c, c), 1)
    rel = (row - col).astype(F32)
    pos = lax.broadcasted_iota(jnp.int32, (c, 1), 0).astype(F32)
    q_dec_f, k_dec_f, q_dec_b, k_dec_b, chunk_f, chunk_b = [], [], [], [], [], []
    for hh in range(heads):
        lgf = lgf_ref[pl.program_id(1) * heads + hh]
        lgb = lgb_ref[pl.program_id(1) * heads + hh]
        dec_ref[hh] = jnp.where(rel >= 0, jnp.exp(lgf * jnp.maximum(rel, 0.0)),
                                jnp.exp(lgb * jnp.maximum(-rel, 0.0))) * scale
        q_dec_f.append(jnp.exp(lgf * (pos + 1.0)))
        k_dec_f.append(jnp.exp(lgf * (c - 1.0 - pos)) * scale)
        q_dec_b.append(jnp.exp(lgb * (c - pos)))
        k_dec_b.append(jnp.exp(lgb * pos) * scale)
        chunk_f.append(jnp.exp(lgf * c))
        chunk_b.append(jnp.exp(lgb * c))

    def rows(n):
        return pl.ds(pl.multiple_of(n * c, c), c)

    def qk_cols(hh):
        return slice(hh * dk, (hh + 1) * dk)

    def v_cols(hh):
        return slice(hh * dv, (hh + 1) * dv)

    def state_update(s_ref, hh, kn, vn, k_dec, chunk_decay):
        kd_t = (kn.astype(F32) * k_dec).T.astype(BF16)
        s_ref[hh] = s_ref[hh] * chunk_decay + jnp.dot(kd_t, vn, preferred_element_type=F32)

    sb_ref[...] = jnp.zeros_like(sb_ref)
    sf_ref[...] = jnp.zeros_like(sf_ref)

    def bwd_body(t, carry):
        n = nc - 1 - t
        for hh in range(heads):
            qn = q_ref[0, rows(n), qk_cols(hh)]
            kn = k_ref[0, rows(n), qk_cols(hh)]
            vn = v_ref[0, rows(n), v_cols(hh)]
            qd = (qn.astype(F32) * q_dec_b[hh]).astype(BF16)
            ob_ref[rows(n), v_cols(hh)] = jnp.dot(qd, sb_ref[hh].astype(BF16), preferred_element_type=F32)
            state_update(sb_ref, hh, kn, vn, k_dec_b[hh], chunk_b[hh])
        return carry

    lax.fori_loop(0, nc, bwd_body, 0)

    def fwd_body(n, carry):
        for hh in range(heads):
            qn = q_ref[0, rows(n), qk_cols(hh)]
            kn = k_ref[0, rows(n), qk_cols(hh)]
            vn = v_ref[0, rows(n), v_cols(hh)]
            sc = lax.dot_general(qn, kn, (((1,), (1,)), ((), ())), preferred_element_type=F32)
            inner = jnp.dot((sc * dec_ref[hh]).astype(BF16), vn, preferred_element_type=F32)
            qd = (qn.astype(F32) * q_dec_f[hh]).astype(BF16)
            cross = jnp.dot(qd, sf_ref[hh].astype(BF16), preferred_element_type=F32)
            o = inner + cross + ob_ref[rows(n), v_cols(hh)]
            state_update(sf_ref, hh, kn, vn, k_dec_f[hh], chunk_f[hh])
            o = o * lax.rsqrt(jnp.mean(o * o, axis=-1, keepdims=True) + NORM_EPS) * gain_ref[:, v_cols(hh)]
            gate = g_ref[0, rows(n), v_cols(hh)].astype(F32)
            o_ref[0, rows(n), v_cols(hh)] = (o * (gate * jax.nn.sigmoid(gate))).astype(o_ref.dtype)
        return carry

    lax.fori_loop(0, nc, fwd_body, 0)


def _retention(z, lg_f, lg_b, ret_gain):
    b, lp, _ = z.shape
    nc = lp // CHUNK
    dk, dv, nh = RET_QK_DIM, RET_V_DIM, RET_HEADS
    heads = 2
    groups = nh // heads
    k_blk0 = groups
    v_blk0 = 2 * nh * dk // (heads * dv)
    g_blk0 = v_blk0 + groups
    smem = pl.BlockSpec(memory_space=pltpu.SMEM)
    return pl.pallas_call(
        functools.partial(_retention_kernel, nc=nc, heads=heads),
        grid=(b, groups),
        in_specs=[
            smem, smem,
            pl.BlockSpec((1, lp, heads * dk), lambda i, h: (i, 0, h)),
            pl.BlockSpec((1, lp, heads * dk), lambda i, h: (i, 0, k_blk0 + h)),
            pl.BlockSpec((1, lp, heads * dv), lambda i, h: (i, 0, v_blk0 + h)),
            pl.BlockSpec((1, lp, heads * dv), lambda i, h: (i, 0, g_blk0 + h)),
            pl.BlockSpec((1, heads * dv), lambda i, h: (0, h)),
        ],
        out_specs=pl.BlockSpec((1, lp, heads * dv), lambda i, h: (i, 0, h)),
        out_shape=jax.ShapeDtypeStruct((b, lp, nh * dv), BF16),
        scratch_shapes=[
            pltpu.VMEM((lp, heads * dv), F32),
            pltpu.VMEM((heads, dk, dv), F32),
            pltpu.VMEM((heads, dk, dv), F32),
            pltpu.VMEM((heads, CHUNK, CHUNK), F32),
        ],
        compiler_params=_compiler_params(("parallel", "arbitrary")),
        name="retention",
    )(lg_f, lg_b, z, z, z, z, ret_gain.reshape(1, nh * dv))


def _attn_kernel(lam_ref, slope_ref, q_ref, k_ref, v_ref, qg_ref, kg_ref, og_ref, o_ref,
                 kaug_ref, corr_ref, qaug_ref, s_ref, p_ref, m_ref, *, tq, tk, n_pad):
    h = pl.program_id(1)
    qi = pl.program_id(2)
    lp = k_ref.shape[1]
    d = DIFF_QK_DIM
    n_chunks = lp // tk
    lanes = 128
    lam = lam_ref[0]
    log2e = math.log2(math.e)
    slope = slope_ref[h] * log2e
    contract_last = (((1,), (1,)), ((), ()))

    def qk_norm(x, gain):
        return x * lax.rsqrt(jnp.mean(x * x, axis=-1, keepdims=True) + NORM_EPS) * gain

    def split3(v):
        hi = v.astype(BF16).astype(F32)
        mid = (v - hi).astype(BF16).astype(F32)
        lo = (v - hi - mid).astype(BF16).astype(F32)
        return [hi, mid, lo]

    def extra_lanes(cols):
        rows = cols[0].shape[0]
        lane = lax.broadcasted_iota(jnp.int32, (rows, d), 1)
        out = jnp.zeros((rows, d), F32)
        for t, col in enumerate(cols):
            out = jnp.where(lane == t, col, out)
        return out.astype(BF16)

    @pl.when(qi == 0)
    def _():
        for c in range(n_chunks):
            r = slice(c * tk, (c + 1) * tk)
            kk = k_ref[0, r, :].astype(F32)
            k_row = c * tk + lax.broadcasted_iota(jnp.int32, (tk, 1), 0)
            real = k_row >= CHUNK
            one = jnp.where(real, 1.0, 0.0)
            first = jnp.where(k_row < n_pad, NEG_BIG, 0.0)
            sk = [jnp.where(real, t, 0.0) for t in split3(slope * k_row.astype(F32))]
            extra = extra_lanes([sk[0] + first, sk[1], sk[2], one, one, one])
            for m in range(2):
                kaug_ref[m, r, :d] = qk_norm(kk[:, m * d:(m + 1) * d], kg_ref[...]).astype(BF16)
                kaug_ref[m, r, d:] = extra
        i_loc = lax.broadcasted_iota(jnp.int32, (tq, tk), 0)
        j_loc = lax.broadcasted_iota(jnp.int32, (tq, tk), 1)
        corr_ref[...] = (2.0 * slope) * jnp.maximum(j_loc - i_loc, 0).astype(F32)

    qq = q_ref[0].astype(F32)
    qs = [(qk_norm(qq[:, m * d:(m + 1) * d], qg_ref[...]) * (d ** -0.5 * log2e)).astype(BF16)
          for m in range(2)]
    q_row = qi * tq + lax.broadcasted_iota(jnp.int32, (tq, 1), 0)
    real_q = q_row >= CHUNK
    sq = [jnp.where(real_q, t, 0.0) for t in split3(slope * q_row.astype(F32))]
    one = jnp.where(real_q, 1.0, 0.0)
    extra_before = extra_lanes([one, one, one] + [-t for t in sq])
    extra_after = extra_lanes([-one, -one, -one] + sq)
    for m in range(2):
        qaug_ref[0, m] = jnp.concatenate([qs[m], extra_before], axis=1)
        qaug_ref[1, m] = jnp.concatenate([qs[m], extra_after], axis=1)

    def lane_fold(x, op):
        out = x[:, :lanes]
        for t in range(1, x.shape[1] // lanes):
            out = op(out, x[:, t * lanes:(t + 1) * lanes])
        return out

    m_ref[...] = jnp.full_like(m_ref, NEG_BIG)
    for c in range(n_chunks):
        r = slice(c * tk, (c + 1) * tk)
        side = jnp.where(c <= qi, 0, 1)
        for m in range(2):
            s = lax.dot_general(qaug_ref[side, m], kaug_ref[m, r, :], contract_last,
                                preferred_element_type=F32)
            s_ref[m, c] = s
            m_ref[m] = jnp.maximum(m_ref[m], jnp.where(c == qi, NEG_BIG, lane_fold(s, jnp.maximum)))
    for m in range(2):
        s = s_ref[m, qi] - corr_ref[...]
        s_ref[m, qi] = s
        m_ref[m] = jnp.maximum(m_ref[m], lane_fold(s, jnp.maximum))

    @pl.when(qi == 0)
    def _():
        q_pos = lax.broadcasted_iota(jnp.int32, (tq, tk), 0)
        k_pos = lax.broadcasted_iota(jnp.int32, (tq, tk), 1)
        dist = jnp.abs(q_pos - k_pos).astype(F32)
        both_real = jnp.logical_and(q_pos >= CHUNK, k_pos >= CHUNK)
        bias = jnp.where(both_real, -slope * dist, 0.0)
        for m in range(2):
            s = lax.dot_general(qs[m], kaug_ref[m, :tk, :d], contract_last, preferred_element_type=F32)
            s = jnp.where(k_pos >= n_pad, s + bias, NEG_BIG)
            s_ref[m, 0] = s
            m_ref[m] = jnp.maximum(m_ref[m], lane_fold(s, jnp.maximum))

    band = lanes
    inv_l = []
    for m in range(2):
        for rb in range(tq // band):
            rows = slice(rb * band, (rb + 1) * band)
            row_max = jnp.max(m_ref[m, rows, :], axis=-1, keepdims=True)
            shift = jnp.concatenate([jnp.broadcast_to(row_max, (band, lanes))] * (tk // lanes), axis=1)
            l_part = jnp.zeros((band, lanes), F32)
            for c in range(n_chunks):
                e = jnp.exp2(s_ref[m, c, rows, :] - shift)
                l_part = l_part + lane_fold(e, jnp.add)
                p_ref[m * tq + rb * band:m * tq + (rb + 1) * band, c * tk:(c + 1) * tk] = e.astype(BF16)
            inv_l.append(1.0 / jnp.sum(l_part, axis=-1, keepdims=True))
    acc = jnp.dot(p_ref[...], v_ref[0], preferred_element_type=F32) * jnp.concatenate(inv_l, axis=0)

    o = acc[:tq] - lam * acc[tq:]
    o = o * lax.rsqrt(jnp.mean(o * o, axis=-1, keepdims=True) + NORM_EPS) * og_ref[...]
    o_ref[0] = (o * (1.0 - LAMBDA_INIT)).astype(o_ref.dtype)


def _attention(z, lam, slopes, q_gain, k_gain, out_gain, *, tq, tk):
    b, lp, _ = z.shape
    nh, d, dv = DIFF_HEADS, DIFF_QK_DIM, DIFF_V_DIM
    assert tq == tk and tq >= CHUNK and lp % tq == 0
    ret_cols = 2 * RET_HEADS * RET_QK_DIM + 2 * RET_HEADS * RET_V_DIM
    q_blk0 = ret_cols // (2 * d)
    k_blk0 = q_blk0 + nh
    v_blk0 = k_blk0 + nh
    smem = pl.BlockSpec(memory_space=pltpu.SMEM)
    return pl.pallas_call(
        functools.partial(_attn_kernel, tq=tq, tk=tk, n_pad=CHUNK - N_META),
        grid=(b, nh, lp // tq),
        in_specs=[
            smem, smem,
            pl.BlockSpec((1, tq, 2 * d), lambda i, h, q: (i, q, q_blk0 + h)),
            pl.BlockSpec((1, lp, 2 * d), lambda i, h, q: (i, 0, k_blk0 + h)),
            pl.BlockSpec((1, lp, dv), lambda i, h, q: (i, 0, v_blk0 + h)),
            pl.BlockSpec((1, d), lambda i, h, q: (0, 0)),
            pl.BlockSpec((1, d), lambda i, h, q: (0, 0)),
            pl.BlockSpec((1, dv), lambda i, h, q: (0, h)),
        ],
        out_specs=pl.BlockSpec((1, tq, dv), lambda i, h, q: (i, q, h)),
        out_shape=jax.ShapeDtypeStruct((b, lp, nh * dv), BF16),
        scratch_shapes=[
            pltpu.VMEM((2, lp, 2 * d), BF16),
            pltpu.VMEM((tq, tk), F32),
            pltpu.VMEM((2, 2, tq, 2 * d), BF16),
            pltpu.VMEM((2, lp // tk, tq, tk), F32),
            pltpu.VMEM((2 * tq, lp), BF16),
            pltpu.VMEM((2, tq, 128), F32),
        ],
        compiler_params=_compiler_params(("parallel", "parallel", "arbitrary")),
        name="attention",
    )(lam, slopes, z, z, z, q_gain.reshape(1, d), k_gain.reshape(1, d), out_gain.reshape(1, nh * dv))


def _pack_bf16_pairs(x):
    n = x.shape[1] // 2
    xb = x.astype(BF16).astype(F32)
    lo = lax.bitcast_convert_type(xb[:, :n], jnp.uint32)
    hi = lax.bitcast_convert_type(xb[:, n:], jnp.uint32)
    return (hi & jnp.uint32(0xFFFF0000)) | (lo >> 16)


def _unpack_bf16_pairs(p):
    lo = lax.bitcast_convert_type(p << 16, F32).astype(BF16)
    hi = lax.bitcast_convert_type(p & jnp.uint32(0xFFFF0000), F32).astype(BF16)
    return lo, hi


def _out_proj_kernel(ret_ref, att_ref, hp_ref, w_ref, g_ref, wr_ref, br_ref,
                     h1_ref, xt_ref, route_ref, cnt_ref, run_ref, *, lp, n_pad):
    i = pl.program_id(0)

    @pl.when(i == 0)
    def _():
        run_ref[...] = jnp.zeros_like(run_ref)

    kr = ret_ref.shape[1]
    y = jnp.dot(ret_ref[...], w_ref[:kr, :], preferred_element_type=F32)
    y = y + jnp.dot(att_ref[...], w_ref[kr:, :], preferred_element_type=F32)
    h1 = hp_ref[...] + y
    h1_ref[...] = h1
    xt = h1 * lax.rsqrt(jnp.mean(h1 * h1, axis=-1, keepdims=True) + NORM_EPS) * g_ref[...]
    xt_ref[...] = _pack_bf16_pairs(xt)
    xt_hi = xt.astype(BF16)
    xt_lo = (xt - xt_hi.astype(F32)).astype(BF16)
    wr = wr_ref[...]
    wr_hi = wr.astype(BF16)
    wr_lo = (wr - wr_hi.astype(F32)).astype(BF16)
    logits = (jnp.dot(xt_hi, wr_hi, preferred_element_type=F32)
              + jnp.dot(xt_lo, wr_hi, preferred_element_type=F32)
              + jnp.dot(xt_hi, wr_lo, preferred_element_type=F32)) + br_ref[...]
    tm, ne = logits.shape
    lane = lax.broadcasted_iota(jnp.int32, (tm, ne), 1).astype(F32)
    out_lane = lax.broadcasted_iota(jnp.int32, route_ref.shape, 1)
    vals, idxs = [], []
    work = logits
    for _ in range(TOP_K):
        top = jnp.max(work, axis=-1, keepdims=True)
        idx = jnp.min(jnp.where(work == top, lane, float(ne)), axis=-1, keepdims=True)
        work = jnp.where(lane == idx, -jnp.inf, work)
        vals.append(top)
        idxs.append(idx)
    exps = [jnp.exp(v - vals[0]) for v in vals]
    denom = exps[0] + exps[1] + exps[2] + exps[3]

    local_row = (i * tm) % lp + lax.broadcasted_iota(jnp.int32, (tm, 1), 0)
    member = jnp.zeros((tm, ne), F32)
    for k in range(TOP_K):
        member = member + jnp.where(lane == idxs[k], 1.0, 0.0)
    member = jnp.where(local_row >= n_pad, member, 0.0)
    earlier = (lax.broadcasted_iota(jnp.int32, (tm, tm), 0)
               > lax.broadcasted_iota(jnp.int32, (tm, tm), 1))
    before = run_ref[...] + jnp.dot(jnp.where(earlier, 1.0, 0.0).astype(BF16), member.astype(BF16),
                                    preferred_element_type=F32)
    run_ref[...] = run_ref[...] + jnp.sum(member, axis=0, keepdims=True)
    cnt_ref[...] = run_ref[...]

    slab = jnp.zeros(route_ref.shape, F32)
    for k in range(TOP_K):
        rank = jnp.sum(jnp.where(lane == idxs[k], before, 0.0), axis=-1, keepdims=True)
        slab = jnp.where(out_lane == k, idxs[k], slab)
        slab = jnp.where(out_lane == TOP_K + k, exps[k] / denom, slab)
        slab = jnp.where(out_lane == 2 * TOP_K + k, rank, slab)
    route_ref[...] = slab


def _out_proj(ret2d, att2d, hp2d, w_bf16, gain, w_router, b_router, *, tm, lp, n_pad):
    m, kr = ret2d.shape
    d = hp2d.shape[1]
    ne = w_router.shape[1]
    assert lp % tm == 0
    row = lambda i: (i, 0)
    const = lambda i: (0, 0)
    return pl.pallas_call(
        functools.partial(_out_proj_kernel, lp=lp, n_pad=n_pad),
        grid=(m // tm,),
        in_specs=[
            pl.BlockSpec((tm, kr), row),
            pl.BlockSpec((tm, kr), row),
            pl.BlockSpec((tm, d), row),
            pl.BlockSpec((2 * kr, d), const),
            pl.BlockSpec((1, d), const),
            pl.BlockSpec((d, ne), const),
            pl.BlockSpec((1, ne), const),
        ],
        out_specs=[
            pl.BlockSpec((tm, d), row),
            pl.BlockSpec((tm, d // 2), row),
            pl.BlockSpec((tm, 128), row),
            pl.BlockSpec((1, ne), const),
        ],
        out_shape=[
            jax.ShapeDtypeStruct((m, d), F32),
            jax.ShapeDtypeStruct((m, d // 2), jnp.uint32),
            jax.ShapeDtypeStruct((m, 128), F32),
            jax.ShapeDtypeStruct((1, ne), F32),
        ],
        scratch_shapes=[pltpu.VMEM((1, ne), F32)],
        compiler_params=_compiler_params(("arbitrary",)),
        name="out_proj",
    )(ret2d, att2d, hp2d, w_bf16, gain.reshape(1, d), w_router, b_router.reshape(1, ne))


EXPERT_TILE = 2176
EXPERT_SUB = 272
WHOLE_PARTS = 4
EXPERT_COLS = 256
EXPERT_WEIGHT_SLOTS = 3


def _expert_kernel(te_ref, nv_ref, q0_ref, nt_ref, rows_ref,
                   x_hbm, wgu_hbm, wd_hbm, bg_ref, bu_ref, bd_ref, o_ref,
                   xbuf, xb_ref, act_ref, wring, sem, wsem, *, tm, sub, n_up, n_down):
    i = pl.program_id(0)
    j = pl.program_id(1)
    n_used = nt_ref[0]
    used = i < n_used
    nv = nv_ref[i]
    half = xbuf.shape[1]
    tf = wring.shape[-1]
    f = wgu_hbm.shape[2] // 2
    steps = n_up + n_down
    n_slots = wring.shape[0]
    rows_per_step = tm // n_up
    up_phase = j < n_up
    wslot = (i * steps + j) % n_slots

    def weight_copy(src, s, piece):
        return pltpu.make_async_copy(src, wring.at[s, piece], wsem.at[s])

    def request_weights(ahead):
        wrap = j + ahead >= steps
        tile = jnp.where(wrap, i + 1, i)
        step = jnp.where(wrap, j + ahead - steps, j + ahead)
        s = (i * steps + j + ahead) % n_slots
        e = te_ref[tile]
        wanted = tile < n_used

        @pl.when(jnp.logical_and(wanted, step < n_up))
        def _():
            col = pl.multiple_of(step * tf, tf)
            weight_copy(wgu_hbm.at[e, :, pl.ds(col, tf)], s, 0).start()
            weight_copy(wgu_hbm.at[e, :, pl.ds(f + col, tf)], s, 1).start()

        @pl.when(jnp.logical_and(wanted, step >= n_up))
        def _():
            col = pl.multiple_of((step - n_up) * 2 * tf, 2 * tf)
            weight_copy(wd_hbm.at[e, :, pl.ds(col, tf)], s, 0).start()
            weight_copy(wd_hbm.at[e, :, pl.ds(col + tf, tf)], s, 1).start()

    @pl.when(jnp.logical_and(i == 0, j == 0))
    def _():
        for ahead in range(n_slots - 1):
            request_weights(ahead)

    @pl.when(used)
    def _():
        request_weights(n_slots - 1)
        for piece in range(2):
            weight_copy(wd_hbm.at[0, :, pl.ds(0, tf)], wslot, piece).wait()

    def row_copy(q, r):
        return pltpu.make_async_copy(x_hbm.at[rows_ref[q]], xbuf.at[r], sem)

    def gather_next_rows():
        r0 = pl.multiple_of(j * rows_per_step, rows_per_step)
        q = q0_ref[i + 1] + r0
        stride = next(s for s in range(rows_per_step // 4 + 1, rows_per_step + 2)
                      if math.gcd(s, rows_per_step) == 1)
        for n in range(rows_per_step):
            r = n * stride % rows_per_step
            row_copy(q + r, r0 + r).start(priority=ROW_DMA_QUEUE)

    @pl.when(jnp.logical_and(i == 0, j == 0))
    def _():
        def body(r, carry):
            row_copy(q0_ref[0] + r, r).start(priority=ROW_DMA_QUEUE)
            return carry
        lax.fori_loop(0, tm, body, 0)

    @pl.when(jnp.logical_and(i <= n_used, j == 0))
    def _():
        pltpu.make_async_copy(xbuf, xbuf, sem).wait()

    @pl.when(jnp.logical_and(used, j == 0))
    def _():
        for s in range(tm // sub):
            r = slice(s * sub, (s + 1) * sub)
            lo, hi = _unpack_bf16_pairs(xbuf[r, :])
            xb_ref[r, :half] = lo
            xb_ref[r, half:] = hi

    def activation(x, wg, wu):
        gate = jnp.dot(x, wg, preferred_element_type=F32) + bg_ref[0]
        up = jnp.dot(x, wu, preferred_element_type=F32) + bu_ref[0]
        gate = jnp.minimum(gate, SWIGLU_LIMIT)
        up = jnp.clip(up, -SWIGLU_LIMIT, SWIGLU_LIMIT)
        glu = gate * jax.nn.sigmoid(SWIGLU_ALPHA * gate)
        return ((up + 1.0) * glu).astype(BF16)

    def down(rows, wd, cols):
        act = jnp.concatenate([act_ref[c, rows, :] for c in range(n_up)], axis=1)
        return jnp.dot(act, wd, preferred_element_type=F32) + bd_ref[0, :, cols]

    whole = jnp.logical_and(used, nv > tm - sub)
    partial = jnp.logical_and(used, nv <= tm - sub)
    part_rows = [slice(p * tm // WHOLE_PARTS, (p + 1) * tm // WHOLE_PARTS) for p in range(WHOLE_PARTS)]

    @pl.when(jnp.logical_and(whole, up_phase))
    def _():
        gather_next_rows()
        wg = wring[wslot, 0].astype(BF16)
        wu = wring[wslot, 1].astype(BF16)
        for rows in part_rows:
            act_ref[j, rows, :] = activation(xb_ref[rows, :], wg, wu)

    @pl.when(jnp.logical_and(whole, jnp.logical_not(up_phase)))
    def _():
        for piece in range(2):
            cols = slice(piece * tf, (piece + 1) * tf)
            wd = wring[wslot, piece].astype(BF16)
            for rows in part_rows:
                o_ref[rows, cols] = down(rows, wd, cols)

    for s in range(tm // sub):
        r = slice(s * sub, (s + 1) * sub)
        live = jnp.logical_and(partial, s * sub < nv)
        dead = jnp.logical_not(jnp.logical_or(whole, live))

        @pl.when(jnp.logical_and(live, up_phase))
        def _():
            if s == 0:
                gather_next_rows()
            act_ref[j, r, :] = activation(xb_ref[r, :], wring[wslot, 0].astype(BF16),
                                          wring[wslot, 1].astype(BF16))

        @pl.when(jnp.logical_and(live, jnp.logical_not(up_phase)))
        def _():
            for piece in range(2):
                cols = slice(piece * tf, (piece + 1) * tf)
                o_ref[r, cols] = down(r, wring[wslot, piece].astype(BF16), cols)

        @pl.when(jnp.logical_and(dead, jnp.logical_not(up_phase)))
        def _():
            o_ref[r, :] = jnp.zeros((sub, o_ref.shape[1]), F32)


def _experts(tile_expert, tile_valid, tile_q0, n_used, sorted_rows, x_packed,
             w_gate_up, b_gate_up, w_down, b_down, *, n_tiles):
    tm, sub = EXPERT_TILE, EXPERT_SUB
    half = x_packed.shape[1]
    ne, d, f2 = w_gate_up.shape
    f = f2 // 2
    assert d == 2 * half
    tf = EXPERT_COLS
    tn = 2 * tf
    n_up, n_down = f // tf, d // tn
    assert tm % n_up == 0 and tm % sub == 0
    assert f == d

    def up_chunk(i, j, nt):
        return jnp.where(i < nt[0], jnp.minimum(j, n_up - 1), n_up - 1)

    def down_chunk(i, j, nt):
        return jnp.where(i < nt[0], jnp.maximum(j - n_up, 0), 0)

    b_gu = b_gate_up.reshape(ne, 1, f2)
    b_d = b_down.reshape(ne, 1, d)
    return pl.pallas_call(
        functools.partial(_expert_kernel, tm=tm, sub=sub, n_up=n_up, n_down=n_down),
        grid_spec=pltpu.PrefetchScalarGridSpec(
            num_scalar_prefetch=5,
            grid=(n_tiles, n_up + n_down),
            in_specs=[
                pl.BlockSpec(memory_space=pl.ANY),
                pl.BlockSpec(memory_space=pl.ANY),
                pl.BlockSpec(memory_space=pl.ANY),
                pl.BlockSpec((1, 1, tf), lambda i, j, te, nv, q0, nt, rows: (te[i], 0, up_chunk(i, j, nt))),
                pl.BlockSpec((1, 1, tf), lambda i, j, te, nv, q0, nt, rows: (te[i], 0, n_up + up_chunk(i, j, nt))),
                pl.BlockSpec((1, 1, tn), lambda i, j, te, nv, q0, nt, rows: (te[i], 0, down_chunk(i, j, nt))),
            ],
            out_specs=pl.BlockSpec((tm, tn), lambda i, j, te, nv, q0, nt, rows: (i, jnp.maximum(j - n_up, 0))),
            scratch_shapes=[
                pltpu.VMEM((tm, half), jnp.uint32),
                pltpu.VMEM((tm, d), BF16),
                pltpu.VMEM((n_up, tm, tf), BF16),
                pltpu.VMEM((EXPERT_WEIGHT_SLOTS, 2, d, tf), F32),
                pltpu.SemaphoreType.DMA,
                pltpu.SemaphoreType.DMA((EXPERT_WEIGHT_SLOTS,)),
            ],
        ),
        out_shape=jax.ShapeDtypeStruct((n_tiles * tm, d), F32),
        compiler_params=_compiler_params(("arbitrary", "arbitrary")),
        name="experts",
    )(tile_expert, tile_valid, tile_q0, n_used, sorted_rows, x_packed,
      w_gate_up, w_down, b_gu, b_gu, b_d)


def _combine_kernel(dest_ref, h1_ref, gate_ref, y_hbm, o_ref, buf, sem, *, tt, tiles_per_batch,
                    tok_per_batch, n_steps):
    t = pl.program_id(0)
    slot = t % 2

    def request_rows(step, s):
        base = ((step // tiles_per_batch) * tok_per_batch + N_META + (step % tiles_per_batch) * tt) * TOP_K
        for r in range(tt):
            for k in range(TOP_K):
                pltpu.make_async_copy(y_hbm.at[dest_ref[base + r * TOP_K + k]], buf.at[s, k, r],
                                      sem.at[s]).start(priority=ROW_DMA_QUEUE)

    def wait_rows(s):
        pltpu.make_async_copy(buf.at[s], buf.at[s], sem.at[s]).wait()

    @pl.when(t == 0)
    def _():
        request_rows(0, 0)

    wait_rows(slot)
    request_rows(jnp.minimum(t + 1, n_steps - 1), 1 - slot)
    g = gate_ref[0]
    moe = buf[slot, 0] * g[:, 0:1]
    for k in range(1, TOP_K):
        moe = moe + buf[slot, k] * g[:, k:k + 1]
    o_ref[0] = h1_ref[0] + moe

    @pl.when(t == n_steps - 1)
    def _():
        wait_rows(1 - slot)


def _combine(dest_flat, h1, gates, y_sorted, *, tt):
    b, lp, d = h1.shape
    s = lp - CHUNK
    tok_per_batch = s + N_META
    tiles = s // tt
    off = CHUNK // tt
    gates_tok = gates[:, N_META:, :]
    n_steps = b * tiles
    return pl.pallas_call(
        functools.partial(_combine_kernel, tt=tt, tiles_per_batch=tiles, tok_per_batch=tok_per_batch,
                          n_steps=n_steps),
        grid_spec=pltpu.PrefetchScalarGridSpec(
            num_scalar_prefetch=1,
            grid=(n_steps,),
            in_specs=[
                pl.BlockSpec((1, tt, d), lambda t, dest: (t // tiles, off + t % tiles, 0)),
                pl.BlockSpec((1, tt, TOP_K), lambda t, dest: (t // tiles, t % tiles, 0)),
                pl.BlockSpec(memory_space=pl.ANY),
            ],
            out_specs=pl.BlockSpec((1, tt, d), lambda t, dest: (t // tiles, t % tiles, 0)),
            scratch_shapes=[pltpu.VMEM((2, TOP_K, tt, d), F32), pltpu.SemaphoreType.DMA((2,))],
        ),
        out_shape=jax.ShapeDtypeStruct((b, s, d), F32),
        compiler_params=_compiler_params(("arbitrary",)),
        name="combine",
    )(dest_flat, h1, gates_tok, y_sorted)


def _lookup(table, idx):
    n = table.shape[0]
    hit = idx[..., None] == jnp.arange(n, dtype=idx.dtype)
    return jnp.sum(jnp.where(hit, table, 0), axis=-1)


def _routing(top_idx, rank, counts, lp, n_pad, *, tm):
    b, l, k = top_idx.shape
    ne = counts.shape[0]
    p = b * l * k
    n_tiles = p // tm + ne + 1
    tiles_per = (counts + tm - 1) // tm
    tile_end = jnp.cumsum(tiles_per)
    tile_start = tile_end - tiles_per
    n_used = tile_end[-1]
    starts = jnp.cumsum(counts) - counts
    tile_id = jnp.arange(n_tiles, dtype=jnp.int32)
    tile_expert = jnp.minimum(jnp.sum(tile_end[None, :] <= tile_id[:, None], axis=-1), ne - 1).astype(jnp.int32)
    within = tile_id - _lookup(tile_start, tile_expert)
    tile_valid = jnp.where(tile_id < n_used, jnp.clip(_lookup(counts, tile_expert) - within * tm, 0, tm), 0)
    tile_q0 = jnp.where(tile_id < n_used, _lookup(starts, tile_expert) + within * tm, 0)
    tok = jnp.arange(b * l, dtype=jnp.int32)
    tok_bits = max(1, (b * l - 1).bit_length())
    keys = jnp.sort((top_idx.reshape(b * l, k) << tok_bits | tok[:, None]).reshape(p))
    sorted_tok = keys & ((1 << tok_bits) - 1)
    sorted_rows = (sorted_tok // l) * lp + n_pad + sorted_tok % l
    sorted_rows = jnp.concatenate([sorted_rows, jnp.arange(tm, dtype=jnp.int32) % (b * lp)])
    dest = _lookup(tile_start * tm, top_idx) + rank
    return (tile_expert, tile_valid.astype(jnp.int32), tile_q0.astype(jnp.int32),
            n_used.astype(jnp.int32).reshape(1), sorted_rows.astype(jnp.int32),
            dest.astype(jnp.int32).reshape(p), n_tiles)


def kernel(x, meta_tokens, attn_norm_gain, w_in, ret_decay_fwd, ret_decay_bwd, ret_out_gain,
           diff_q_gain, diff_k_gain, diff_lambda_q1, diff_lambda_k1, diff_lambda_q2, diff_lambda_k2,
           diff_out_gain, w_out, ffn_norm_gain, w_router, b_router, w_gate_up, b_gate_up,
           w_down, b_down):
    b, s, d = x.shape
    assert s % CHUNK == 0 and w_in.shape[0] == 1, "single layer, token count a chunk multiple"
    n_pad = CHUNK - N_META
    lp = s + CHUNK
    l = s + N_META

    hp, u = _prep(x, meta_tokens.astype(x.dtype), attn_norm_gain[0])
    z = _in_proj(u.reshape(b * lp, d), w_in[0].astype(BF16), tm=lp // 4, tn=1024)
    z = z.reshape(b, lp, -1)

    lg_f = jnp.log1p(-jnp.exp2(-ret_decay_fwd[0].astype(F32)))
    lg_b = jnp.log1p(-jnp.exp2(-ret_decay_bwd[0].astype(F32)))
    ret = _retention(z, lg_f, lg_b, ret_out_gain[0])

    lam = (jnp.exp(jnp.sum(diff_lambda_q1[0].astype(F32) * diff_lambda_k1[0].astype(F32)))
           - jnp.exp(jnp.sum(diff_lambda_q2[0].astype(F32) * diff_lambda_k2[0].astype(F32)))
           + LAMBDA_INIT).reshape(1)
    slopes = jnp.exp2(-8.0 * jnp.arange(1, DIFF_HEADS + 1, dtype=F32) / DIFF_HEADS)
    att = _attention(z, lam, slopes, diff_q_gain[0], diff_k_gain[0], diff_out_gain[0], tq=384, tk=384)

    h1, xt, route, counts = _out_proj(
        ret.reshape(b * lp, -1), att.reshape(b * lp, -1), hp.reshape(b * lp, d),
        w_out[0].astype(BF16), ffn_norm_gain[0], w_router[0], b_router[0], tm=lp // 8, lp=lp, n_pad=n_pad)

    route = route.reshape(b, lp, 128)[:, n_pad:, :]
    top_idx = route[:, :, :TOP_K].astype(jnp.int32)
    gates = route[:, :, TOP_K:2 * TOP_K]
    rank = route[:, :, 2 * TOP_K:3 * TOP_K].astype(jnp.int32)

    tile_expert, tile_valid, tile_q0, n_used, sorted_rows, dest, n_tiles = _routing(
        top_idx, rank, counts.reshape(-1).astype(jnp.int32), lp, n_pad, tm=EXPERT_TILE)
    ys = _experts(tile_expert, tile_valid, tile_q0, n_used, sorted_rows, xt,
                  w_gate_up[0], b_gate_up[0], w_down[0], b_down[0], n_tiles=n_tiles)
    return _combine(dest, h1.reshape(b, lp, d), gates, ys, tt=CHUNK)
```

```python
import functools
import math

import jax
import jax.numpy as jnp
from jax import lax
from jax.experimental import pallas as pl
from jax.experimental.pallas import tpu as pltpu

N_META = 16
CHUNK = 128
RET_HEADS = 4
RET_QK_DIM = 128
RET_V_DIM = 256
DIFF_HEADS = 4
DIFF_QK_DIM = 128
DIFF_V_DIM = 256
TOP_K = 4
SWIGLU_ALPHA = 1.702
SWIGLU_LIMIT = 7.0
NORM_EPS = 1e-6
RET_DECAY_BASE = 5.0
LAMBDA_INIT = 0.8 - 0.6 * math.exp(-0.3 * 0)
NEG_BIG = -1e30

V7X_VMEM_BYTES = 64 * 1024 * 1024
VMEM_LIMIT_BYTES = V7X_VMEM_BYTES - 8 * 1024 * 1024

F32 = jnp.float32
BF16 = jnp.bfloat16

ROW_DMA_QUEUE = 1


def _compiler_params(semantics):
    return pltpu.CompilerParams(dimension_semantics=semantics, vmem_limit_bytes=VMEM_LIMIT_BYTES)


def _prep_kernel(x_ref, meta_ref, g_ref, hp_ref, u_ref, *, n_pad):
    c = pl.program_id(1)

    def emit(h):
        hp_ref[0] = h
        ms = jnp.mean(h * h, axis=-1, keepdims=True)
        u_ref[0] = (h * lax.rsqrt(ms + NORM_EPS) * g_ref[...]).astype(u_ref.dtype)

    @pl.when(c == 0)
    def _():
        d = meta_ref.shape[-1]
        emit(jnp.concatenate([jnp.zeros((n_pad, d), F32), meta_ref[...]], axis=0))

    @pl.when(c > 0)
    def _():
        emit(x_ref[0])


def _prep(x, meta, gain):
    b, s, d = x.shape
    n_pad = CHUNK - N_META
    nc = s // CHUNK + 1
    lp = nc * CHUNK
    return pl.pallas_call(
        functools.partial(_prep_kernel, n_pad=n_pad),
        grid=(b, nc),
        in_specs=[
            pl.BlockSpec((1, CHUNK, d), lambda i, c: (i, jnp.maximum(c - 1, 0), 0)),
            pl.BlockSpec((N_META, d), lambda i, c: (0, 0)),
            pl.BlockSpec((1, d), lambda i, c: (0, 0)),
        ],
        out_specs=[
            pl.BlockSpec((1, CHUNK, d), lambda i, c: (i, c, 0)),
            pl.BlockSpec((1, CHUNK, d), lambda i, c: (i, c, 0)),
        ],
        out_shape=[
            jax.ShapeDtypeStruct((b, lp, d), F32),
            jax.ShapeDtypeStruct((b, lp, d), BF16),
        ],
        compiler_params=_compiler_params(("parallel", "arbitrary")),
        name="prep",
    )(x, meta, gain.reshape(1, d))


def _matmul_kernel(a_ref, b_ref, o_ref):
    o_ref[...] = jnp.dot(a_ref[...], b_ref[...], preferred_element_type=F32).astype(o_ref.dtype)


def _in_proj(u2d, w_bf16, *, tm, tn):
    m, k = u2d.shape
    n = w_bf16.shape[1]
    return pl.pallas_call(
        _matmul_kernel,
        grid=(m // tm, n // tn),
        in_specs=[
            pl.BlockSpec((tm, k), lambda i, j: (i, 0)),
            pl.BlockSpec((k, tn), lambda i, j: (0, j)),
        ],
        out_specs=pl.BlockSpec((tm, tn), lambda i, j: (i, j)),
        out_shape=jax.ShapeDtypeStruct((m, n), BF16),
        compiler_params=_compiler_params(("parallel", "arbitrary")),
        name="in_proj",
    )(u2d, w_bf16)


def _retention_kernel(lgf_ref, lgb_ref, q_ref, k_ref, v_ref, g_ref, gain_ref, o_ref,
                      ob_ref, sf_ref, sb_ref, dec_ref, *, nc, heads):
    c = CHUNK
    dk, dv = RET_QK_DIM, RET_V_DIM
    scale = dk ** -0.5
    row = lax.broadcasted_iota(jnp.int32, (c, c), 0)
    col = lax.broadcasted_iota(jnp.int32, (c, c), 1)
    rel = (row - col).astype(F32)
    pos = lax.broadcasted_iota(jnp.int32, (c, 1), 0).astype(F32)
    q_dec_f, k_dec_f, q_dec_b, k_dec_b, chunk_f, chunk_b = [], [], [], [], [], []
    for hh in range(heads):
        lgf = lgf_ref[pl.program_id(1) * heads + hh]
        lgb = lgb_ref[pl.program_id(1) * heads + hh]
        dec_ref[hh] = jnp.where(rel >= 0, jnp.exp(lgf * jnp.maximum(rel, 0.0)),
                                jnp.exp(lgb * jnp.maximum(-rel, 0.0))) * scale
        q_dec_f.append(jnp.exp(lgf * (pos + 1.0)))
        k_dec_f.append(jnp.exp(lgf * (c - 1.0 - pos)) * scale)
        q_dec_b.append(jnp.exp(lgb * (c - pos)))
        k_dec_b.append(jnp.exp(lgb * pos) * scale)
        chunk_f.append(jnp.exp(lgf * c))
        chunk_b.append(jnp.exp(lgb * c))

    def rows(n):
        return pl.ds(pl.multiple_of(n * c, c), c)

    def qk_cols(hh):
        return slice(hh * dk, (hh + 1) * dk)

    def v_cols(hh):
        return slice(hh * dv, (hh + 1) * dv)

    def state_update(s_ref, hh, kn, vn, k_dec, chunk_decay):
        kd_t = (kn.astype(F32) * k_dec).T.astype(BF16)
        s_ref[hh] = s_ref[hh] * chunk_decay + jnp.dot(kd_t, vn, preferred_element_type=F32)

    sb_ref[...] = jnp.zeros_like(sb_ref)
    sf_ref[...] = jnp.zeros_like(sf_ref)

    def bwd_body(t, carry):
        n = nc - 1 - t
        for hh in range(heads):
            qn = q_ref[0, rows(n), qk_cols(hh)]
            kn = k_ref[0, rows(n), qk_cols(hh)]
            vn = v_ref[0, rows(n), v_cols(hh)]
            qd = (qn.astype(F32) * q_dec_b[hh]).astype(BF16)
            ob_ref[rows(n), v_cols(hh)] = jnp.dot(qd, sb_ref[hh].astype(BF16), preferred_element_type=F32)
            state_update(sb_ref, hh, kn, vn, k_dec_b[hh], chunk_b[hh])
        return carry

    lax.fori_loop(0, nc, bwd_body, 0)

    def fwd_body(n, carry):
        for hh in range(heads):
            qn = q_ref[0, rows(n), qk_cols(hh)]
            kn = k_ref[0, rows(n), qk_cols(hh)]
            vn = v_ref[0, rows(n), v_cols(hh)]
            sc = lax.dot_general(qn, kn, (((1,), (1,)), ((), ())), preferred_element_type=F32)
            inner = jnp.dot((sc * dec_ref[hh]).astype(BF16), vn, preferred_element_type=F32)
            qd = (qn.astype(F32) * q_dec_f[hh]).astype(BF16)
            cross = jnp.dot(qd, sf_ref[hh].astype(BF16), preferred_element_type=F32)
            o = inner + cross + ob_ref[rows(n), v_cols(hh)]
            state_update(sf_ref, hh, kn, vn, k_dec_f[hh], chunk_f[hh])
            o = o * lax.rsqrt(jnp.mean(o * o, axis=-1, keepdims=True) + NORM_EPS) * gain_ref[:, v_cols(hh)]
            gate = g_ref[0, rows(n), v_cols(hh)].astype(F32)
            o_ref[0, rows(n), v_cols(hh)] = (o * (gate * jax.nn.sigmoid(gate))).astype(o_ref.dtype)
        return carry

    lax.fori_loop(0, nc, fwd_body, 0)


def _retention(z, lg_f, lg_b, ret_gain):
    b, lp, _ = z.shape
    nc = lp // CHUNK
    dk, dv, nh = RET_QK_DIM, RET_V_DIM, RET_HEADS
    heads = 2
    groups = nh // heads
    k_blk0 = groups
    v_blk0 = 2 * nh * dk // (heads * dv)
    g_blk0 = v_blk0 + groups
    smem = pl.BlockSpec(memory_space=pltpu.SMEM)
    return pl.pallas_call(
        functools.partial(_retention_kernel, nc=nc, heads=heads),
        grid=(b, groups),
        in_specs=[
            smem, smem,
            pl.BlockSpec((1, lp, heads * dk), lambda i, h: (i, 0, h)),
            pl.BlockSpec((1, lp, heads * dk), lambda i, h: (i, 0, k_blk0 + h)),
            pl.BlockSpec((1, lp, heads * dv), lambda i, h: (i, 0, v_blk0 + h)),
            pl.BlockSpec((1, lp, heads * dv), lambda i, h: (i, 0, g_blk0 + h)),
            pl.BlockSpec((1, heads * dv), lambda i, h: (0, h)),
        ],
        out_specs=pl.BlockSpec((1, lp, heads * dv), lambda i, h: (i, 0, h)),
        out_shape=jax.ShapeDtypeStruct((b, lp, nh * dv), BF16),
        scratch_shapes=[
            pltpu.VMEM((lp, heads * dv), F32),
            pltpu.VMEM((heads, dk, dv), F32),
            pltpu.VMEM((heads, dk, dv), F32),
            pltpu.VMEM((heads, CHUNK, CHUNK), F32),
        ],
        compiler_params=_compiler_params(("parallel", "arbitrary")),
        name="retention",
    )(lg_f, lg_b, z, z, z, z, ret_gain.reshape(1, nh * dv))


def _attn_kernel(lam_ref, slope_ref, q_ref, k_ref, v_ref, qg_ref, kg_ref, og_ref, o_ref,
                 kaug_ref, corr_ref, qaug_ref, s_ref, p_ref, m_ref, *, tq, tk, n_pad):
    h = pl.program_id(1)
    qi = pl.program_id(2)
    lp = k_ref.shape[1]
    d = DIFF_QK_DIM
    n_chunks = lp // tk
    lanes = 128
    lam = lam_ref[0]
    log2e = math.log2(math.e)
    slope = slope_ref[h] * log2e
    contract_last = (((1,), (1,)), ((), ()))

    def qk_norm(x, gain):
        return x * lax.rsqrt(jnp.mean(x * x, axis=-1, keepdims=True) + NORM_EPS) * gain

    def split3(v):
        hi = v.astype(BF16).astype(F32)
        mid = (v - hi).astype(BF16).astype(F32)
        lo = (v - hi - mid).astype(BF16).astype(F32)
        return [hi, mid, lo]

    def extra_lanes(cols):
        rows = cols[0].shape[0]
        lane = lax.broadcasted_iota(jnp.int32, (rows, d), 1)
        out = jnp.zeros((rows, d), F32)
        for t, col in enumerate(cols):
            out = jnp.where(lane == t, col, out)
        return out.astype(BF16)

    @pl.when(qi == 0)
    def _():
        for c in range(n_chunks):
            r = slice(c * tk, (c + 1) * tk)
            kk = k_ref[0, r, :].astype(F32)
            k_row = c * tk + lax.broadcasted_iota(jnp.int32, (tk, 1), 0)
            real = k_row >= CHUNK
            one = jnp.where(real, 1.0, 0.0)
            first = jnp.where(k_row < n_pad, NEG_BIG, 0.0)
            sk = [jnp.where(real, t, 0.0) for t in split3(slope * k_row.astype(F32))]
            extra = extra_lanes([sk[0] + first, sk[1], sk[2], one, one, one])
            for m in range(2):
                kaug_ref[m, r, :d] = qk_norm(kk[:, m * d:(m + 1) * d], kg_ref[...]).astype(BF16)
                kaug_ref[m, r, d:] = extra
        i_loc = lax.broadcasted_iota(jnp.int32, (tq, tk), 0)
        j_loc = lax.broadcasted_iota(jnp.int32, (tq, tk), 1)
        corr_ref[...] = (2.0 * slope) * jnp.maximum(j_loc - i_loc, 0).astype(F32)

    qq = q_ref[0].astype(F32)
    qs = [(qk_norm(qq[:, m * d:(m + 1) * d], qg_ref[...]) * (d ** -0.5 * log2e)).astype(BF16)
          for m in range(2)]
    q_row = qi * tq + lax.broadcasted_iota(jnp.int32, (tq, 1), 0)
    real_q = q_row >= CHUNK
    sq = [jnp.where(real_q, t, 0.0) for t in split3(slope * q_row.astype(F32))]
    one = jnp.where(real_q, 1.0, 0.0)
    extra_before = extra_lanes([one, one, one] + [-t for t in sq])
    extra_after = extra_lanes([-one, -one, -one] + sq)
    for m in range(2):
        qaug_ref[0, m] = jnp.concatenate([qs[m], extra_before], axis=1)
        qaug_ref[1, m] = jnp.concatenate([qs[m], extra_after], axis=1)

    def lane_fold(x, op):
        out = x[:, :lanes]
        for t in range(1, x.shape[1] // lanes):
            out = op(out, x[:, t * lanes:(t + 1) * lanes])
        return out

    m_ref[...] = jnp.full_like(m_ref, NEG_BIG)
    for c in range(n_chunks):
        r = slice(c * tk, (c + 1) * tk)
        side = jnp.where(c <= qi, 0, 1)
        for m in range(2):
            s = lax.dot_general(qaug_ref[side, m], kaug_ref[m, r, :], contract_last,
                                preferred_element_type=F32)
            s_ref[m, c] = s
            m_ref[m] = jnp.maximum(m_ref[m], jnp.where(c == qi, NEG_BIG, lane_fold(s, jnp.maximum)))
    for m in range(2):
        s = s_ref[m, qi] - corr_ref[...]
        s_ref[m, qi] = s
        m_ref[m] = jnp.maximum(m_ref[m], lane_fold(s, jnp.maximum))

    @pl.when(qi == 0)
    def _():
        q_pos = lax.broadcasted_iota(jnp.int32, (tq, tk), 0)
        k_pos = lax.broadcasted_iota(jnp.int32, (tq, tk), 1)
        dist = jnp.abs(q_pos - k_pos).astype(F32)
        both_real = jnp.logical_and(q_pos >= CHUNK, k_pos >= CHUNK)
        bias = jnp.where(both_real, -slope * dist, 0.0)
        for m in range(2):
            s = lax.dot_general(qs[m], kaug_ref[m, :tk, :d], contract_last, preferred_element_type=F32)
            s = jnp.where(k_pos >= n_pad, s + bias, NEG_BIG)
            s_ref[m, 0] = s
            m_ref[m] = jnp.maximum(m_ref[m], lane_fold(s, jnp.maximum))

    band = lanes
    inv_l = []
    for m in range(2):
        for rb in range(tq // band):
            rows = slice(rb * band, (rb + 1) * band)
            row_max = jnp.max(m_ref[m, rows, :], axis=-1, keepdims=True)
            shift = jnp.concatenate([jnp.broadcast_to(row_max, (band, lanes))] * (tk // lanes), axis=1)
            l_part = jnp.zeros((band, lanes), F32)
            for c in range(n_chunks):
                e = jnp.exp2(s_ref[m, c, rows, :] - shift)
                l_part = l_part + lane_fold(e, jnp.add)
                p_ref[m * tq + rb * band:m * tq + (rb + 1) * band, c * tk:(c + 1) * tk] = e.astype(BF16)
            inv_l.append(1.0 / jnp.sum(l_part, axis=-1, keepdims=True))
    acc = jnp.dot(p_ref[...], v_ref[0], preferred_element_type=F32) * jnp.concatenate(inv_l, axis=0)

    o = acc[:tq] - lam * acc[tq:]
    o = o * lax.rsqrt(jnp.mean(o * o, axis=-1, keepdims=True) + NORM_EPS) * og_ref[...]
    o_ref[0] = (o * (1.0 - LAMBDA_INIT)).astype(o_ref.dtype)


def _attention(z, lam, slopes, q_gain, k_gain, out_gain, *, tq, tk):
    b, lp, _ = z.shape
    nh, d, dv = DIFF_HEADS, DIFF_QK_DIM, DIFF_V_DIM
    assert tq == tk and tq >= CHUNK and lp % tq == 0
    ret_cols = 2 * RET_HEADS * RET_QK_DIM + 2 * RET_HEADS * RET_V_DIM
    q_blk0 = ret_cols // (2 * d)
    k_blk0 = q_blk0 + nh
    v_blk0 = k_blk0 + nh
    smem = pl.BlockSpec(memory_space=pltpu.SMEM)
    return pl.pallas_call(
        functools.partial(_attn_kernel, tq=tq, tk=tk, n_pad=CHUNK - N_META),
        grid=(b, nh, lp // tq),
        in_specs=[
            smem, smem,
            pl.BlockSpec((1, tq, 2 * d), lambda i, h, q: (i, q, q_blk0 + h)),
            pl.BlockSpec((1, lp, 2 * d), lambda i, h, q: (i, 0, k_blk0 + h)),
            pl.BlockSpec((1, lp, dv), lambda i, h, q: (i, 0, v_blk0 + h)),
            pl.BlockSpec((1, d), lambda i, h, q: (0, 0)),
            pl.BlockSpec((1, d), lambda i, h, q: (0, 0)),
            pl.BlockSpec((1, dv), lambda i, h, q: (0, h)),
        ],
        out_specs=pl.BlockSpec((1, tq, dv), lambda i, h, q: (i, q, h)),
        out_shape=jax.ShapeDtypeStruct((b, lp, nh * dv), BF16),
        scratch_shapes=[
            pltpu.VMEM((2, lp, 2 * d), BF16),
            pltpu.VMEM((tq, tk), F32),
            pltpu.VMEM((2, 2, tq, 2 * d), BF16),
            pltpu.VMEM((2, lp // tk, tq, tk), F32),
            pltpu.VMEM((2 * tq, lp), BF16),
            pltpu.VMEM((2, tq, 128), F32),
        ],
        compiler_params=_compiler_params(("parallel", "parallel", "arbitrary")),
        name="attention",
    )(lam, slopes, z, z, z, q_gain.reshape(1, d), k_gain.reshape(1, d), out_gain.reshape(1, nh * dv))


def _pack_bf16_pairs(x):
    n = x.shape[1] // 2
    xb = x.astype(BF16).astype(F32)
    lo = lax.bitcast_convert_type(xb[:, :n], jnp.uint32)
    hi = lax.bitcast_convert_type(xb[:, n:], jnp.uint32)
    return (hi & jnp.uint32(0xFFFF0000)) | (lo >> 16)


def _unpack_bf16_pairs(p):
    lo = lax.bitcast_convert_type(p << 16, F32).astype(BF16)
    hi = lax.bitcast_convert_type(p & jnp.uint32(0xFFFF0000), F32).astype(BF16)
    return lo, hi


def _out_proj_kernel(ret_ref, att_ref, hp_ref, w_ref, g_ref, wr_ref, br_ref,
                     h1_ref, xt_ref, route_ref, cnt_ref, run_ref, *, lp, n_pad):
    i = pl.program_id(0)

    @pl.when(i == 0)
    def _():
        run_ref[...] = jnp.zeros_like(run_ref)

    kr = ret_ref.shape[1]
    y = jnp.dot(ret_ref[...], w_ref[:kr, :], preferred_element_type=F32)
    y = y + jnp.dot(att_ref[...], w_ref[kr:, :], preferred_element_type=F32)
    h1 = hp_ref[...] + y
    h1_ref[...] = h1
    xt = h1 * lax.rsqrt(jnp.mean(h1 * h1, axis=-1, keepdims=True) + NORM_EPS) * g_ref[...]
    xt_ref[...] = _pack_bf16_pairs(xt)
    xt_hi = xt.astype(BF16)
    xt_lo = (xt - xt_hi.astype(F32)).astype(BF16)
    wr = wr_ref[...]
    wr_hi = wr.astype(BF16)
    wr_lo = (wr - wr_hi.astype(F32)).astype(BF16)
    logits = (jnp.dot(xt_hi, wr_hi, preferred_element_type=F32)
              + jnp.dot(xt_lo, wr_hi, preferred_element_type=F32)
              + jnp.dot(xt_hi, wr_lo, preferred_element_type=F32)) + br_ref[...]
    tm, ne = logits.shape
    lane = lax.broadcasted_iota(jnp.int32, (tm, ne), 1).astype(F32)
    out_lane = lax.broadcasted_iota(jnp.int32, route_ref.shape, 1)
    vals, idxs = [], []
    work = logits
    for _ in range(TOP_K):
        top = jnp.max(work, axis=-1, keepdims=True)
        idx = jnp.min(jnp.where(work == top, lane, float(ne)), axis=-1, keepdims=True)
        work = jnp.where(lane == idx, -jnp.inf, work)
        vals.append(top)
        idxs.append(idx)
    exps = [jnp.exp(v - vals[0]) for v in vals]
    denom = exps[0] + exps[1] + exps[2] + exps[3]

    local_row = (i * tm) % lp + lax.broadcasted_iota(jnp.int32, (tm, 1), 0)
    member = jnp.zeros((tm, ne), F32)
    for k in range(TOP_K):
        member = member + jnp.where(lane == idxs[k], 1.0, 0.0)
    member = jnp.where(local_row >= n_pad, member, 0.0)
    earlier = (lax.broadcasted_iota(jnp.int32, (tm, tm), 0)
               > lax.broadcasted_iota(jnp.int32, (tm, tm), 1))
    before = run_ref[...] + jnp.dot(jnp.where(earlier, 1.0, 0.0).astype(BF16), member.astype(BF16),
                                    preferred_element_type=F32)
    run_ref[...] = run_ref[...] + jnp.sum(member, axis=0, keepdims=True)
    cnt_ref[...] = run_ref[...]

    slab = jnp.zeros(route_ref.shape, F32)
    for k in range(TOP_K):
        rank = jnp.sum(jnp.where(lane == idxs[k], before, 0.0), axis=-1, keepdims=True)
        slab = jnp.where(out_lane == k, idxs[k], slab)
        slab = jnp.where(out_lane == TOP_K + k, exps[k] / denom, slab)
        slab = jnp.where(out_lane == 2 * TOP_K + k, rank, slab)
    route_ref[...] = slab


def _out_proj(ret2d, att2d, hp2d, w_bf16, gain, w_router, b_router, *, tm, lp, n_pad):
    m, kr = ret2d.shape
    d = hp2d.shape[1]
    ne = w_router.shape[1]
    assert lp % tm == 0
    row = lambda i: (i, 0)
    const = lambda i: (0, 0)
    return pl.pallas_call(
        functools.partial(_out_proj_kernel, lp=lp, n_pad=n_pad),
        grid=(m // tm,),
        in_specs=[
            pl.BlockSpec((tm, kr), row),
            pl.BlockSpec((tm, kr), row),
            pl.BlockSpec((tm, d), row),
            pl.BlockSpec((2 * kr, d), const),
            pl.BlockSpec((1, d), const),
            pl.BlockSpec((d, ne), const),
            pl.BlockSpec((1, ne), const),
        ],
        out_specs=[
            pl.BlockSpec((tm, d), row),
            pl.BlockSpec((tm, d // 2), row),
            pl.BlockSpec((tm, 128), row),
            pl.BlockSpec((1, ne), const),
        ],
        out_shape=[
            jax.ShapeDtypeStruct((m, d), F32),
            jax.ShapeDtypeStruct((m, d // 2), jnp.uint32),
            jax.ShapeDtypeStruct((m, 128), F32),
            jax.ShapeDtypeStruct((1, ne), F32),
        ],
        scratch_shapes=[pltpu.VMEM((1, ne), F32)],
        compiler_params=_compiler_params(("arbitrary",)),
        name="out_proj",
    )(ret2d, att2d, hp2d, w_bf16, gain.reshape(1, d), w_router, b_router.reshape(1, ne))


EXPERT_TILE = 2176
EXPERT_SUB = 272
EXPERT_COLS = 256
EXPERT_WEIGHT_SLOTS = 3


def _expert_kernel(te_ref, nv_ref, q0_ref, nt_ref, rows_ref,
                   x_hbm, wgu_hbm, wd_hbm, bg_ref, bu_ref, bd_ref, o_ref,
                   xbuf, xb_ref, act_ref, wring, sem, wsem, *, tm, sub, n_up, n_down):
    i = pl.program_id(0)
    j = pl.program_id(1)
    n_used = nt_ref[0]
    used = i < n_used
    nv = nv_ref[i]
    half = xbuf.shape[1]
    tf = wring.shape[-1]
    f = wgu_hbm.shape[2] // 2
    steps = n_up + n_down
    n_slots = wring.shape[0]
    rows_per_step = tm // n_up
    up_phase = j < n_up
    wslot = (i * steps + j) % n_slots

    def weight_copy(src, s, piece):
        return pltpu.make_async_copy(src, wring.at[s, piece], wsem.at[s])

    def request_weights(ahead):
        wrap = j + ahead >= steps
        tile = jnp.where(wrap, i + 1, i)
        step = jnp.where(wrap, j + ahead - steps, j + ahead)
        s = (i * steps + j + ahead) % n_slots
        e = te_ref[tile]
        wanted = tile < n_used

        @pl.when(jnp.logical_and(wanted, step < n_up))
        def _():
            col = pl.multiple_of(step * tf, tf)
            weight_copy(wgu_hbm.at[e, :, pl.ds(col, tf)], s, 0).start()
            weight_copy(wgu_hbm.at[e, :, pl.ds(f + col, tf)], s, 1).start()

        @pl.when(jnp.logical_and(wanted, step >= n_up))
        def _():
            col = pl.multiple_of((step - n_up) * 2 * tf, 2 * tf)
            weight_copy(wd_hbm.at[e, :, pl.ds(col, tf)], s, 0).start()
            weight_copy(wd_hbm.at[e, :, pl.ds(col + tf, tf)], s, 1).start()

    @pl.when(jnp.logical_and(i == 0, j == 0))
    def _():
        for ahead in range(n_slots - 1):
            request_weights(ahead)

    @pl.when(used)
    def _():
        request_weights(n_slots - 1)
        for piece in range(2):
            weight_copy(wd_hbm.at[0, :, pl.ds(0, tf)], wslot, piece).wait()

    def row_copy(q, r):
        return pltpu.make_async_copy(x_hbm.at[rows_ref[q]], xbuf.at[r], sem)

    def gather_next_rows():
        r0 = pl.multiple_of(j * rows_per_step, rows_per_step)
        q = q0_ref[i + 1] + r0
        stride = next(s for s in range(rows_per_step // 4 + 1, rows_per_step + 2)
                      if math.gcd(s, rows_per_step) == 1)
        for n in range(rows_per_step):
            r = n * stride % rows_per_step
            row_copy(q + r, r0 + r).start(priority=n % 2)

    @pl.when(jnp.logical_and(i == 0, j == 0))
    def _():
        def body(r, carry):
            row_copy(q0_ref[0] + r, r).start(priority=ROW_DMA_QUEUE)
            return carry
        lax.fori_loop(0, tm, body, 0)

    @pl.when(jnp.logical_and(i <= n_used, j == 0))
    def _():
        pltpu.make_async_copy(xbuf, xbuf, sem).wait()

    @pl.when(jnp.logical_and(used, j == 0))
    def _():
        for s in range(tm // sub):
            r = slice(s * sub, (s + 1) * sub)
            lo, hi = _unpack_bf16_pairs(xbuf[r, :])
            xb_ref[r, :half] = lo
            xb_ref[r, half:] = hi

    def activation(x, wg, wu):
        gate = jnp.dot(x, wg, preferred_element_type=F32) + bg_ref[0]
        up = jnp.dot(x, wu, preferred_element_type=F32) + bu_ref[0]
        gate = jnp.minimum(gate, SWIGLU_LIMIT)
        up = jnp.clip(up, -SWIGLU_LIMIT, SWIGLU_LIMIT)
        glu = gate * jax.nn.sigmoid(SWIGLU_ALPHA * gate)
        return ((up + 1.0) * glu).astype(BF16)

    def down(rows, wd, cols):
        act = jnp.concatenate([act_ref[c, rows, :] for c in range(n_up)], axis=1)
        return jnp.dot(act, wd, preferred_element_type=F32) + bd_ref[0, :, cols]

    whole = jnp.logical_and(used, nv > tm - sub)
    partial = jnp.logical_and(used, nv <= tm - sub)

    @pl.when(jnp.logical_and(whole, up_phase))
    def _():
        gather_next_rows()
        wg = wring[wslot, 0].astype(BF16)
        wu = wring[wslot, 1].astype(BF16)
        for part in range(2):
            rows = slice(part * tm // 2, (part + 1) * tm // 2)
            act_ref[j, rows, :] = activation(xb_ref[rows, :], wg, wu)

    @pl.when(jnp.logical_and(whole, jnp.logical_not(up_phase)))
    def _():
        for piece in range(2):
            cols = slice(piece * tf, (piece + 1) * tf)
            o_ref[:, cols] = down(slice(None), wring[wslot, piece].astype(BF16), cols)

    for s in range(tm // sub):
        r = slice(s * sub, (s + 1) * sub)
        live = jnp.logical_and(partial, s * sub < nv)
        dead = jnp.logical_not(jnp.logical_or(whole, live))

        @pl.when(jnp.logical_and(live, up_phase))
        def _():
            if s == 0:
                gather_next_rows()
            act_ref[j, r, :] = activation(xb_ref[r, :], wring[wslot, 0].astype(BF16),
                                          wring[wslot, 1].astype(BF16))

        @pl.when(jnp.logical_and(live, jnp.logical_not(up_phase)))
        def _():
            for piece in range(2):
                cols = slice(piece * tf, (piece + 1) * tf)
                o_ref[r, cols] = down(r, wring[wslot, piece].astype(BF16), cols)

        @pl.when(jnp.logical_and(dead, jnp.logical_not(up_phase)))
        def _():
            o_ref[r, :] = jnp.zeros((sub, o_ref.shape[1]), F32)


def _experts(tile_expert, tile_valid, tile_q0, n_used, sorted_rows, x_packed,
             w_gate_up, b_gate_up, w_down, b_down, *, n_tiles):
    tm, sub = EXPERT_TILE, EXPERT_SUB
    half = x_packed.shape[1]
    ne, d, f2 = w_gate_up.shape
    f = f2 // 2
    assert d == 2 * half
    tf = EXPERT_COLS
    tn = 2 * tf
    n_up, n_down = f // tf, d // tn
    assert tm % n_up == 0 and tm % sub == 0
    assert f == d

    def up_chunk(i, j, nt):
        return jnp.where(i < nt[0], jnp.minimum(j, n_up - 1), n_up - 1)

    def down_chunk(i, j, nt):
        return jnp.where(i < nt[0], jnp.maximum(j - n_up, 0), 0)

    b_gu = b_gate_up.reshape(ne, 1, f2)
    b_d = b_down.reshape(ne, 1, d)
    return pl.pallas_call(
        functools.partial(_expert_kernel, tm=tm, sub=sub, n_up=n_up, n_down=n_down),
        grid_spec=pltpu.PrefetchScalarGridSpec(
            num_scalar_prefetch=5,
            grid=(n_tiles, n_up + n_down),
            in_specs=[
                pl.BlockSpec(memory_space=pl.ANY),
                pl.BlockSpec(memory_space=pl.ANY),
                pl.BlockSpec(memory_space=pl.ANY),
                pl.BlockSpec((1, 1, tf), lambda i, j, te, nv, q0, nt, rows: (te[i], 0, up_chunk(i, j, nt))),
                pl.BlockSpec((1, 1, tf), lambda i, j, te, nv, q0, nt, rows: (te[i], 0, n_up + up_chunk(i, j, nt))),
                pl.BlockSpec((1, 1, tn), lambda i, j, te, nv, q0, nt, rows: (te[i], 0, down_chunk(i, j, nt))),
            ],
            out_specs=pl.BlockSpec((tm, tn), lambda i, j, te, nv, q0, nt, rows: (i, jnp.maximum(j - n_up, 0))),
            scratch_shapes=[
                pltpu.VMEM((tm, half), jnp.uint32),
                pltpu.VMEM((tm, d), BF16),
                pltpu.VMEM((n_up, tm, tf), BF16),
                pltpu.VMEM((EXPERT_WEIGHT_SLOTS, 2, d, tf), F32),
                pltpu.SemaphoreType.DMA,
                pltpu.SemaphoreType.DMA((EXPERT_WEIGHT_SLOTS,)),
            ],
        ),
        out_shape=jax.ShapeDtypeStruct((n_tiles * tm, d), F32),
        compiler_params=_compiler_params(("arbitrary", "arbitrary")),
        name="experts",
    )(tile_expert, tile_valid, tile_q0, n_used, sorted_rows, x_packed,
      w_gate_up, w_down, b_gu, b_gu, b_d)


def _combine_kernel(dest_ref, h1_ref, gate_ref, y_hbm, o_ref, buf, sem, *, tt, tiles_per_batch,
                    tok_per_batch, n_steps):
    t = pl.program_id(0)
    slot = t % 2

    def request_rows(step, s):
        base = ((step // tiles_per_batch) * tok_per_batch + N_META + (step % tiles_per_batch) * tt) * TOP_K
        for r in range(tt):
            for k in range(TOP_K):
                pltpu.make_async_copy(y_hbm.at[dest_ref[base + r * TOP_K + k]], buf.at[s, k, r],
                                      sem.at[s]).start(priority=ROW_DMA_QUEUE)

    def wait_rows(s):
        pltpu.make_async_copy(buf.at[s], buf.at[s], sem.at[s]).wait()

    @pl.when(t == 0)
    def _():
        request_rows(0, 0)

    wait_rows(slot)
    request_rows(jnp.minimum(t + 1, n_steps - 1), 1 - slot)
    g = gate_ref[0]
    moe = buf[slot, 0] * g[:, 0:1]
    for k in range(1, TOP_K):
        moe = moe + buf[slot, k] * g[:, k:k + 1]
    o_ref[0] = h1_ref[0] + moe

    @pl.when(t == n_steps - 1)
    def _():
        wait_rows(1 - slot)


def _combine(dest_flat, h1, gates, y_sorted, *, tt):
    b, lp, d = h1.shape
    s = lp - CHUNK
    tok_per_batch = s + N_META
    tiles = s // tt
    off = CHUNK // tt
    gates_tok = gates[:, N_META:, :]
    n_steps = b * tiles
    return pl.pallas_call(
        functools.partial(_combine_kernel, tt=tt, tiles_per_batch=tiles, tok_per_batch=tok_per_batch,
                          n_steps=n_steps),
        grid_spec=pltpu.PrefetchScalarGridSpec(
            num_scalar_prefetch=1,
            grid=(n_steps,),
            in_specs=[
                pl.BlockSpec((1, tt, d), lambda t, dest: (t // tiles, off + t % tiles, 0)),
                pl.BlockSpec((1, tt, TOP_K), lambda t, dest: (t // tiles, t % tiles, 0)),
                pl.BlockSpec(memory_space=pl.ANY),
            ],
            out_specs=pl.BlockSpec((1, tt, d), lambda t, dest: (t // tiles, t % tiles, 0)),
            scratch_shapes=[pltpu.VMEM((2, TOP_K, tt, d), F32), pltpu.SemaphoreType.DMA((2,))],
        ),
        out_shape=jax.ShapeDtypeStruct((b, s, d), F32),
        compiler_params=_compiler_params(("arbitrary",)),
        name="combine",
    )(dest_flat, h1, gates_tok, y_sorted)


def _lookup(table, idx):
    n = table.shape[0]
    hit = idx[..., None] == jnp.arange(n, dtype=idx.dtype)
    return jnp.sum(jnp.where(hit, table, 0), axis=-1)


def _routing(top_idx, rank, counts, lp, n_pad, *, tm):
    b, l, k = top_idx.shape
    ne = counts.shape[0]
    p = b * l * k
    n_tiles = p // tm + ne + 1
    tiles_per = (counts + tm - 1) // tm
    tile_end = jnp.cumsum(tiles_per)
    tile_start = tile_end - tiles_per
    n_used = tile_end[-1]
    starts = jnp.cumsum(counts) - counts
    tile_id = jnp.arange(n_tiles, dtype=jnp.int32)
    tile_expert = jnp.minimum(jnp.sum(tile_end[None, :] <= tile_id[:, None], axis=-1), ne - 1).astype(jnp.int32)
    within = tile_id - _lookup(tile_start, tile_expert)
    tile_valid = jnp.where(tile_id < n_used, jnp.clip(_lookup(counts, tile_expert) - within * tm, 0, tm), 0)
    tile_q0 = jnp.where(tile_id < n_used, _lookup(starts, tile_expert) + within * tm, 0)
    tok = jnp.arange(b * l, dtype=jnp.int32)
    tok_bits = max(1, (b * l - 1).bit_length())
    keys = jnp.sort((top_idx.reshape(b * l, k) << tok_bits | tok[:, None]).reshape(p))
    sorted_tok = keys & ((1 << tok_bits) - 1)
    sorted_rows = (sorted_tok // l) * lp + n_pad + sorted_tok % l
    sorted_rows = jnp.concatenate([sorted_rows, jnp.arange(tm, dtype=jnp.int32) % (b * lp)])
    dest = _lookup(tile_start * tm, top_idx) + rank
    return (tile_expert, tile_valid.astype(jnp.int32), tile_q0.astype(jnp.int32),
            n_used.astype(jnp.int32).reshape(1), sorted_rows.astype(jnp.int32),
            dest.astype(jnp.int32).reshape(p), n_tiles)


def kernel(x, meta_tokens, attn_norm_gain, w_in, ret_decay_fwd, ret_decay_bwd, ret_out_gain,
           diff_q_gain, diff_k_gain, diff_lambda_q1, diff_lambda_k1, diff_lambda_q2, diff_lambda_k2,
           diff_out_gain, w_out, ffn_norm_gain, w_router, b_router, w_gate_up, b_gate_up,
           w_down, b_down):
    b, s, d = x.shape
    assert s % CHUNK == 0 and w_in.shape[0] == 1, "single layer, token count a chunk multiple"
    n_pad = CHUNK - N_META
    lp = s + CHUNK
    l = s + N_META

    hp, u = _prep(x, meta_tokens.astype(x.dtype), attn_norm_gain[0])
    z = _in_proj(u.reshape(b * lp, d), w_in[0].astype(BF16), tm=lp // 4, tn=1024)
    z = z.reshape(b, lp, -1)

    lg_f = jnp.log1p(-jnp.exp2(-ret_decay_fwd[0].astype(F32)))
    lg_b = jnp.log1p(-jnp.exp2(-ret_decay_bwd[0].astype(F32)))
    ret = _retention(z, lg_f, lg_b, ret_out_gain[0])

    lam = (jnp.exp(jnp.sum(diff_lambda_q1[0].astype(F32) * diff_lambda_k1[0].astype(F32)))
           - jnp.exp(jnp.sum(diff_lambda_q2[0].astype(F32) * diff_lambda_k2[0].astype(F32)))
           + LAMBDA_INIT).reshape(1)
    slopes = jnp.exp2(-8.0 * jnp.arange(1, DIFF_HEADS + 1, dtype=F32) / DIFF_HEADS)
    att = _attention(z, lam, slopes, diff_q_gain[0], diff_k_gain[0], diff_out_gain[0], tq=384, tk=384)

    h1, xt, route, counts = _out_proj(
        ret.reshape(b * lp, -1), att.reshape(b * lp, -1), hp.reshape(b * lp, d),
        w_out[0].astype(BF16), ffn_norm_gain[0], w_router[0], b_router[0], tm=lp // 8, lp=lp, n_pad=n_pad)

    route = route.reshape(b, lp, 128)[:, n_pad:, :]
    top_idx = route[:, :, :TOP_K].astype(jnp.int32)
    gates = route[:, :, TOP_K:2 * TOP_K]
    rank = route[:, :, 2 * TOP_K:3 * TOP_K].astype(jnp.int32)

    tile_expert, tile_valid, tile_q0, n_used, sorted_rows, dest, n_tiles = _routing(
        top_idx, rank, counts.reshape(-1).astype(jnp.int32), lp, n_pad, tm=EXPERT_TILE)
    ys = _experts(tile_expert, tile_valid, tile_q0, n_used, sorted_rows, xt,
                  w_gate_up[0], b_gate_up[0], w_down[0], b_down[0], n_tiles=n_tiles)
    return _combine(dest, h1.reshape(b, lp, d), gates, ys, tt=CHUNK)
```

```python
import functools
import math

import jax
import jax.numpy as jnp
from jax import lax
from jax.experimental import pallas as pl
from jax.experimental.pallas import tpu as pltpu

N_META = 16
CHUNK = 128
RET_HEADS = 4
RET_QK_DIM = 128
RET_V_DIM = 256
DIFF_HEADS = 4
DIFF_QK_DIM = 128
DIFF_V_DIM = 256
TOP_K = 4
SWIGLU_ALPHA = 1.702
SWIGLU_LIMIT = 7.0
NORM_EPS = 1e-6
RET_DECAY_BASE = 5.0
LAMBDA_INIT = 0.8 - 0.6 * math.exp(-0.3 * 0)
NEG_BIG = -1e30

V7X_VMEM_BYTES = 64 * 1024 * 1024
VMEM_LIMIT_BYTES = V7X_VMEM_BYTES - 8 * 1024 * 1024

F32 = jnp.float32
BF16 = jnp.bfloat16

ROW_DMA_QUEUE = 1


def _compiler_params(semantics):
    return pltpu.CompilerParams(dimension_semantics=semantics, vmem_limit_bytes=VMEM_LIMIT_BYTES)


def _prep_kernel(x_ref, meta_ref, g_ref, hp_ref, u_ref, *, n_pad):
    c = pl.program_id(1)

    def emit(h):
        hp_ref[0] = h
        ms = jnp.mean(h * h, axis=-1, keepdims=True)
        u_ref[0] = (h * lax.rsqrt(ms + NORM_EPS) * g_ref[...]).astype(u_ref.dtype)

    @pl.when(c == 0)
    def _():
        d = meta_ref.shape[-1]
        emit(jnp.concatenate([jnp.zeros((n_pad, d), F32), meta_ref[...]], axis=0))

    @pl.when(c > 0)
    def _():
        emit(x_ref[0])


def _prep(x, meta, gain):
    b, s, d = x.shape
    n_pad = CHUNK - N_META
    nc = s // CHUNK + 1
    lp = nc * CHUNK
    return pl.pallas_call(
        functools.partial(_prep_kernel, n_pad=n_pad),
        grid=(b, nc),
        in_specs=[
            pl.BlockSpec((1, CHUNK, d), lambda i, c: (i, jnp.maximum(c - 1, 0), 0)),
            pl.BlockSpec((N_META, d), lambda i, c: (0, 0)),
            pl.BlockSpec((1, d), lambda i, c: (0, 0)),
        ],
        out_specs=[
            pl.BlockSpec((1, CHUNK, d), lambda i, c: (i, c, 0)),
            pl.BlockSpec((1, CHUNK, d), lambda i, c: (i, c, 0)),
        ],
        out_shape=[
            jax.ShapeDtypeStruct((b, lp, d), F32),
            jax.ShapeDtypeStruct((b, lp, d), BF16),
        ],
        compiler_params=_compiler_params(("parallel", "arbitrary")),
        name="prep",
    )(x, meta, gain.reshape(1, d))


def _matmul_kernel(a_ref, b_ref, o_ref):
    o_ref[...] = jnp.dot(a_ref[...], b_ref[...], preferred_element_type=F32).astype(o_ref.dtype)


def _in_proj(u2d, w_bf16, *, tm, tn):
    m, k = u2d.shape
    n = w_bf16.shape[1]
    return pl.pallas_call(
        _matmul_kernel,
        grid=(m // tm, n // tn),
        in_specs=[
            pl.BlockSpec((tm, k), lambda i, j: (i, 0)),
            pl.BlockSpec((k, tn), lambda i, j: (0, j)),
        ],
        out_specs=pl.BlockSpec((tm, tn), lambda i, j: (i, j)),
        out_shape=jax.ShapeDtypeStruct((m, n), BF16),
        compiler_params=_compiler_params(("parallel", "arbitrary")),
        name="in_proj",
    )(u2d, w_bf16)


def _retention_kernel(lgf_ref, lgb_ref, q_ref, k_ref, v_ref, g_ref, gain_ref, o_ref,
                      ob_ref, sf_ref, sb_ref, dec_ref, *, nc, heads):
    c = CHUNK
    dk, dv = RET_QK_DIM, RET_V_DIM
    scale = dk ** -0.5
    row = lax.broadcasted_iota(jnp.int32, (c, c), 0)
    col = lax.broadcasted_iota(jnp.int32, (c, c), 1)
    rel = (row - col).astype(F32)
    pos = lax.broadcasted_iota(jnp.int32, (c, 1), 0).astype(F32)
    q_dec_f, k_dec_f, q_dec_b, k_dec_b, chunk_f, chunk_b = [], [], [], [], [], []
    for hh in range(heads):
        lgf = lgf_ref[pl.program_id(1) * heads + hh]
        lgb = lgb_ref[pl.program_id(1) * heads + hh]
        dec_ref[hh] = jnp.where(rel >= 0, jnp.exp(lgf * jnp.maximum(rel, 0.0)),
                                jnp.exp(lgb * jnp.maximum(-rel, 0.0))) * scale
        q_dec_f.append(jnp.exp(lgf * (pos + 1.0)))
        k_dec_f.append(jnp.exp(lgf * (c - 1.0 - pos)) * scale)
        q_dec_b.append(jnp.exp(lgb * (c - pos)))
        k_dec_b.append(jnp.exp(lgb * pos) * scale)
        chunk_f.append(jnp.exp(lgf * c))
        chunk_b.append(jnp.exp(lgb * c))

    def rows(n):
        return pl.ds(pl.multiple_of(n * c, c), c)

    def qk_cols(hh):
        return slice(hh * dk, (hh + 1) * dk)

    def v_cols(hh):
        return slice(hh * dv, (hh + 1) * dv)

    def state_update(s_ref, hh, kn, vn, k_dec, chunk_decay):
        kd_t = (kn.astype(F32) * k_dec).T.astype(BF16)
        s_ref[hh] = s_ref[hh] * chunk_decay + jnp.dot(kd_t, vn, preferred_element_type=F32)

    sb_ref[...] = jnp.zeros_like(sb_ref)
    sf_ref[...] = jnp.zeros_like(sf_ref)

    def bwd_body(t, carry):
        n = nc - 1 - t
        for hh in range(heads):
            qn = q_ref[0, rows(n), qk_cols(hh)]
            kn = k_ref[0, rows(n), qk_cols(hh)]
            vn = v_ref[0, rows(n), v_cols(hh)]
            qd = (qn.astype(F32) * q_dec_b[hh]).astype(BF16)
            ob_ref[rows(n), v_cols(hh)] = jnp.dot(qd, sb_ref[hh].astype(BF16), preferred_element_type=F32)
            state_update(sb_ref, hh, kn, vn, k_dec_b[hh], chunk_b[hh])
        return carry

    lax.fori_loop(0, nc, bwd_body, 0)

    def fwd_body(n, carry):
        for hh in range(heads):
            qn = q_ref[0, rows(n), qk_cols(hh)]
            kn = k_ref[0, rows(n), qk_cols(hh)]
            vn = v_ref[0, rows(n), v_cols(hh)]
            sc = lax.dot_general(qn, kn, (((1,), (1,)), ((), ())), preferred_element_type=F32)
            inner = jnp.dot((sc * dec_ref[hh]).astype(BF16), vn, preferred_element_type=F32)
            qd = (qn.astype(F32) * q_dec_f[hh]).astype(BF16)
            cross = jnp.dot(qd, sf_ref[hh].astype(BF16), preferred_element_type=F32)
            o = inner + cross + ob_ref[rows(n), v_cols(hh)]
            state_update(sf_ref, hh, kn, vn, k_dec_f[hh], chunk_f[hh])
            o = o * lax.rsqrt(jnp.mean(o * o, axis=-1, keepdims=True) + NORM_EPS) * gain_ref[:, v_cols(hh)]
            gate = g_ref[0, rows(n), v_cols(hh)].astype(F32)
            o_ref[0, rows(n), v_cols(hh)] = (o * (gate * jax.nn.sigmoid(gate))).astype(o_ref.dtype)
        return carry

    lax.fori_loop(0, nc, fwd_body, 0)


def _retention(z, lg_f, lg_b, ret_gain):
    b, lp, _ = z.shape
    nc = lp // CHUNK
    dk, dv, nh = RET_QK_DIM, RET_V_DIM, RET_HEADS
    heads = 2
    groups = nh // heads
    k_blk0 = groups
    v_blk0 = 2 * nh * dk // (heads * dv)
    g_blk0 = v_blk0 + groups
    smem = pl.BlockSpec(memory_space=pltpu.SMEM)
    return pl.pallas_call(
        functools.partial(_retention_kernel, nc=nc, heads=heads),
        grid=(b, groups),
        in_specs=[
            smem, smem,
            pl.BlockSpec((1, lp, heads * dk), lambda i, h: (i, 0, h)),
            pl.BlockSpec((1, lp, heads * dk), lambda i, h: (i, 0, k_blk0 + h)),
            pl.BlockSpec((1, lp, heads * dv), lambda i, h: (i, 0, v_blk0 + h)),
            pl.BlockSpec((1, lp, heads * dv), lambda i, h: (i, 0, g_blk0 + h)),
            pl.BlockSpec((1, heads * dv), lambda i, h: (0, h)),
        ],
        out_specs=pl.BlockSpec((1, lp, heads * dv), lambda i, h: (i, 0, h)),
        out_shape=jax.ShapeDtypeStruct((b, lp, nh * dv), BF16),
        scratch_shapes=[
            pltpu.VMEM((lp, heads * dv), F32),
            pltpu.VMEM((heads, dk, dv), F32),
            pltpu.VMEM((heads, dk, dv), F32),
            pltpu.VMEM((heads, CHUNK, CHUNK), F32),
        ],
        compiler_params=_compiler_params(("parallel", "arbitrary")),
        name="retention",
    )(lg_f, lg_b, z, z, z, z, ret_gain.reshape(1, nh * dv))


def _attn_kernel(lam_ref, slope_ref, q_ref, k_ref, v_ref, qg_ref, kg_ref, og_ref, o_ref,
                 kaug_ref, corr_ref, qaug_ref, s_ref, p_ref, m_ref, *, tq, tk, n_pad):
    h = pl.program_id(1)
    qi = pl.program_id(2)
    lp = k_ref.shape[1]
    d = DIFF_QK_DIM
    n_chunks = lp // tk
    lanes = 128
    lam = lam_ref[0]
    log2e = math.log2(math.e)
    slope = slope_ref[h] * log2e
    contract_last = (((1,), (1,)), ((), ()))

    def qk_norm(x, gain):
        return x * lax.rsqrt(jnp.mean(x * x, axis=-1, keepdims=True) + NORM_EPS) * gain

    def split3(v):
        hi = v.astype(BF16).astype(F32)
        mid = (v - hi).astype(BF16).astype(F32)
        lo = (v - hi - mid).astype(BF16).astype(F32)
        return [hi, mid, lo]

    def extra_lanes(cols):
        rows = cols[0].shape[0]
        lane = lax.broadcasted_iota(jnp.int32, (rows, d), 1)
        out = jnp.zeros((rows, d), F32)
        for t, col in enumerate(cols):
            out = jnp.where(lane == t, col, out)
        return out.astype(BF16)

    @pl.when(qi == 0)
    def _():
        for c in range(n_chunks):
            r = slice(c * tk, (c + 1) * tk)
            kk = k_ref[0, r, :].astype(F32)
            k_row = c * tk + lax.broadcasted_iota(jnp.int32, (tk, 1), 0)
            real = k_row >= CHUNK
            one = jnp.where(real, 1.0, 0.0)
            first = jnp.where(k_row < n_pad, NEG_BIG, 0.0)
            sk = [jnp.where(real, t, 0.0) for t in split3(slope * k_row.astype(F32))]
            extra = extra_lanes([sk[0] + first, sk[1], sk[2], one, one, one])
            for m in range(2):
                kaug_ref[m, r, :d] = qk_norm(kk[:, m * d:(m + 1) * d], kg_ref[...]).astype(BF16)
                kaug_ref[m, r, d:] = extra
        i_loc = lax.broadcasted_iota(jnp.int32, (tq, tk), 0)
        j_loc = lax.broadcasted_iota(jnp.int32, (tq, tk), 1)
        corr_ref[...] = (2.0 * slope) * jnp.maximum(j_loc - i_loc, 0).astype(F32)

    qq = q_ref[0].astype(F32)
    qs = [(qk_norm(qq[:, m * d:(m + 1) * d], qg_ref[...]) * (d ** -0.5 * log2e)).astype(BF16)
          for m in range(2)]
    q_row = qi * tq + lax.broadcasted_iota(jnp.int32, (tq, 1), 0)
    real_q = q_row >= CHUNK
    sq = [jnp.where(real_q, t, 0.0) for t in split3(slope * q_row.astype(F32))]
    one = jnp.where(real_q, 1.0, 0.0)
    extra_before = extra_lanes([one, one, one] + [-t for t in sq])
    extra_after = extra_lanes([-one, -one, -one] + sq)
    for m in range(2):
        qaug_ref[0, m] = jnp.concatenate([qs[m], extra_before], axis=1)
        qaug_ref[1, m] = jnp.concatenate([qs[m], extra_after], axis=1)

    def lane_fold(x, op):
        out = x[:, :lanes]
        for t in range(1, x.shape[1] // lanes):
            out = op(out, x[:, t * lanes:(t + 1) * lanes])
        return out

    m_ref[...] = jnp.full_like(m_ref, NEG_BIG)
    for c in range(n_chunks):
        r = slice(c * tk, (c + 1) * tk)
        side = jnp.where(c <= qi, 0, 1)
        for m in range(2):
            s = lax.dot_general(qaug_ref[side, m], kaug_ref[m, r, :], contract_last,
                                preferred_element_type=F32)
            s_ref[m, c] = s
            m_ref[m] = jnp.maximum(m_ref[m], jnp.where(c == qi, NEG_BIG, lane_fold(s, jnp.maximum)))
    for m in range(2):
        s = s_ref[m, qi] - corr_ref[...]
        s_ref[m, qi] = s
        m_ref[m] = jnp.maximum(m_ref[m], lane_fold(s, jnp.maximum))

    @pl.when(qi == 0)
    def _():
        q_pos = lax.broadcasted_iota(jnp.int32, (tq, tk), 0)
        k_pos = lax.broadcasted_iota(jnp.int32, (tq, tk), 1)
        dist = jnp.abs(q_pos - k_pos).astype(F32)
        both_real = jnp.logical_and(q_pos >= CHUNK, k_pos >= CHUNK)
        bias = jnp.where(both_real, -slope * dist, 0.0)
        for m in range(2):
            s = lax.dot_general(qs[m], kaug_ref[m, :tk, :d], contract_last, preferred_element_type=F32)
            s = jnp.where(k_pos >= n_pad, s + bias, NEG_BIG)
            s_ref[m, 0] = s
            m_ref[m] = jnp.maximum(m_ref[m], lane_fold(s, jnp.maximum))

    band = lanes
    inv_l = []
    for m in range(2):
        for rb in range(tq // band):
            rows = slice(rb * band, (rb + 1) * band)
            row_max = jnp.max(m_ref[m, rows, :], axis=-1, keepdims=True)
            shift = jnp.concatenate([jnp.broadcast_to(row_max, (band, lanes))] * (tk // lanes), axis=1)
            l_part = jnp.zeros((band, lanes), F32)
            for c in range(n_chunks):
                e = jnp.exp2(s_ref[m, c, rows, :] - shift)
                l_part = l_part + lane_fold(e, jnp.add)
                p_ref[m * tq + rb * band:m * tq + (rb + 1) * band, c * tk:(c + 1) * tk] = e.astype(BF16)
            inv_l.append(1.0 / jnp.sum(l_part, axis=-1, keepdims=True))
    acc = jnp.dot(p_ref[...], v_ref[0], preferred_element_type=F32) * jnp.concatenate(inv_l, axis=0)

    o = acc[:tq] - lam * acc[tq:]
    o = o * lax.rsqrt(jnp.mean(o * o, axis=-1, keepdims=True) + NORM_EPS) * og_ref[...]
    o_ref[0] = (o * (1.0 - LAMBDA_INIT)).astype(o_ref.dtype)


def _attention(z, lam, slopes, q_gain, k_gain, out_gain, *, tq, tk):
    b, lp, _ = z.shape
    nh, d, dv = DIFF_HEADS, DIFF_QK_DIM, DIFF_V_DIM
    assert tq == tk and tq >= CHUNK and lp % tq == 0
    ret_cols = 2 * RET_HEADS * RET_QK_DIM + 2 * RET_HEADS * RET_V_DIM
    q_blk0 = ret_cols // (2 * d)
    k_blk0 = q_blk0 + nh
    v_blk0 = k_blk0 + nh
    smem = pl.BlockSpec(memory_space=pltpu.SMEM)
    return pl.pallas_call(
        functools.partial(_attn_kernel, tq=tq, tk=tk, n_pad=CHUNK - N_META),
        grid=(b, nh, lp // tq),
        in_specs=[
            smem, smem,
            pl.BlockSpec((1, tq, 2 * d), lambda i, h, q: (i, q, q_blk0 + h)),
            pl.BlockSpec((1, lp, 2 * d), lambda i, h, q: (i, 0, k_blk0 + h)),
            pl.BlockSpec((1, lp, dv), lambda i, h, q: (i, 0, v_blk0 + h)),
            pl.BlockSpec((1, d), lambda i, h, q: (0, 0)),
            pl.BlockSpec((1, d), lambda i, h, q: (0, 0)),
            pl.BlockSpec((1, dv), lambda i, h, q: (0, h)),
        ],
        out_specs=pl.BlockSpec((1, tq, dv), lambda i, h, q: (i, q, h)),
        out_shape=jax.ShapeDtypeStruct((b, lp, nh * dv), BF16),
        scratch_shapes=[
            pltpu.VMEM((2, lp, 2 * d), BF16),
            pltpu.VMEM((tq, tk), F32),
            pltpu.VMEM((2, 2, tq, 2 * d), BF16),
            pltpu.VMEM((2, lp // tk, tq, tk), F32),
            pltpu.VMEM((2 * tq, lp), BF16),
            pltpu.VMEM((2, tq, 128), F32),
        ],
        compiler_params=_compiler_params(("parallel", "parallel", "arbitrary")),
        name="attention",
    )(lam, slopes, z, z, z, q_gain.reshape(1, d), k_gain.reshape(1, d), out_gain.reshape(1, nh * dv))


def _pack_bf16_pairs(x):
    n = x.shape[1] // 2
    xb = x.astype(BF16).astype(F32)
    lo = lax.bitcast_convert_type(xb[:, :n], jnp.uint32)
    hi = lax.bitcast_convert_type(xb[:, n:], jnp.uint32)
    return (hi & jnp.uint32(0xFFFF0000)) | (lo >> 16)


def _unpack_bf16_pairs(p):
    lo = lax.bitcast_convert_type(p << 16, F32).astype(BF16)
    hi = lax.bitcast_convert_type(p & jnp.uint32(0xFFFF0000), F32).astype(BF16)
    return lo, hi


def _out_proj_kernel(ret_ref, att_ref, hp_ref, w_ref, g_ref, wr_ref, br_ref,
                     h1_ref, xt_ref, route_ref, cnt_ref, run_ref, *, lp, n_pad):
    i = pl.program_id(0)

    @pl.when(i == 0)
    def _():
        run_ref[...] = jnp.zeros_like(run_ref)

    kr = ret_ref.shape[1]
    y = jnp.dot(ret_ref[...], w_ref[:kr, :], preferred_element_type=F32)
    y = y + jnp.dot(att_ref[...], w_ref[kr:, :], preferred_element_type=F32)
    h1 = hp_ref[...] + y
    h1_ref[...] = h1
    xt = h1 * lax.rsqrt(jnp.mean(h1 * h1, axis=-1, keepdims=True) + NORM_EPS) * g_ref[...]
    xt_ref[...] = _pack_bf16_pairs(xt)
    xt_hi = xt.astype(BF16)
    xt_lo = (xt - xt_hi.astype(F32)).astype(BF16)
    wr = wr_ref[...]
    wr_hi = wr.astype(BF16)
    wr_lo = (wr - wr_hi.astype(F32)).astype(BF16)
    logits = (jnp.dot(xt_hi, wr_hi, preferred_element_type=F32)
              + jnp.dot(xt_lo, wr_hi, preferred_element_type=F32)
              + jnp.dot(xt_hi, wr_lo, preferred_element_type=F32)) + br_ref[...]
    tm, ne = logits.shape
    lane = lax.broadcasted_iota(jnp.int32, (tm, ne), 1).astype(F32)
    out_lane = lax.broadcasted_iota(jnp.int32, route_ref.shape, 1)
    vals, idxs = [], []
    work = logits
    for _ in range(TOP_K):
        top = jnp.max(work, axis=-1, keepdims=True)
        idx = jnp.min(jnp.where(work == top, lane, float(ne)), axis=-1, keepdims=True)
        work = jnp.where(lane == idx, -jnp.inf, work)
        vals.append(top)
        idxs.append(idx)
    exps = [jnp.exp(v - vals[0]) for v in vals]
    denom = exps[0] + exps[1] + exps[2] + exps[3]

    local_row = (i * tm) % lp + lax.broadcasted_iota(jnp.int32, (tm, 1), 0)
    member = jnp.zeros((tm, ne), F32)
    for k in range(TOP_K):
        member = member + jnp.where(lane == idxs[k], 1.0, 0.0)
    member = jnp.where(local_row >= n_pad, member, 0.0)
    earlier = (lax.broadcasted_iota(jnp.int32, (tm, tm), 0)
               > lax.broadcasted_iota(jnp.int32, (tm, tm), 1))
    before = run_ref[...] + jnp.dot(jnp.where(earlier, 1.0, 0.0).astype(BF16), member.astype(BF16),
                                    preferred_element_type=F32)
    run_ref[...] = run_ref[...] + jnp.sum(member, axis=0, keepdims=True)
    cnt_ref[...] = run_ref[...]

    slab = jnp.zeros(route_ref.shape, F32)
    for k in range(TOP_K):
        rank = jnp.sum(jnp.where(lane == idxs[k], before, 0.0), axis=-1, keepdims=True)
        slab = jnp.where(out_lane == k, idxs[k], slab)
        slab = jnp.where(out_lane == TOP_K + k, exps[k] / denom, slab)
        slab = jnp.where(out_lane == 2 * TOP_K + k, rank, slab)
    route_ref[...] = slab


def _out_proj(ret2d, att2d, hp2d, w_bf16, gain, w_router, b_router, *, tm, lp, n_pad):
    m, kr = ret2d.shape
    d = hp2d.shape[1]
    ne = w_router.shape[1]
    assert lp % tm == 0
    row = lambda i: (i, 0)
    const = lambda i: (0, 0)
    return pl.pallas_call(
        functools.partial(_out_proj_kernel, lp=lp, n_pad=n_pad),
        grid=(m // tm,),
        in_specs=[
            pl.BlockSpec((tm, kr), row),
            pl.BlockSpec((tm, kr), row),
            pl.BlockSpec((tm, d), row),
            pl.BlockSpec((2 * kr, d), const),
            pl.BlockSpec((1, d), const),
            pl.BlockSpec((d, ne), const),
            pl.BlockSpec((1, ne), const),
        ],
        out_specs=[
            pl.BlockSpec((tm, d), row),
            pl.BlockSpec((tm, d // 2), row),
            pl.BlockSpec((tm, 128), row),
            pl.BlockSpec((1, ne), const),
        ],
        out_shape=[
            jax.ShapeDtypeStruct((m, d), F32),
            jax.ShapeDtypeStruct((m, d // 2), jnp.uint32),
            jax.ShapeDtypeStruct((m, 128), F32),
            jax.ShapeDtypeStruct((1, ne), F32),
        ],
        scratch_shapes=[pltpu.VMEM((1, ne), F32)],
        compiler_params=_compiler_params(("arbitrary",)),
        name="out_proj",
    )(ret2d, att2d, hp2d, w_bf16, gain.reshape(1, d), w_router, b_router.reshape(1, ne))


EXPERT_TILE = 2176
EXPERT_SUB = 272
EXPERT_COLS = 256
EXPERT_WEIGHT_SLOTS = 3


def _expert_kernel(te_ref, nv_ref, q0_ref, nt_ref, rows_ref,
                   x_hbm, wgu_hbm, wd_hbm, bg_ref, bu_ref, bd_ref, o_ref,
                   xbuf, xb_ref, act_ref, wring, sem, wsem, *, tm, sub, n_up, n_down):
    i = pl.program_id(0)
    j = pl.program_id(1)
    n_used = nt_ref[0]
    used = i < n_used
    nv = nv_ref[i]
    half = xbuf.shape[1]
    tf = wring.shape[-1]
    f = wgu_hbm.shape[2] // 2
    steps = n_up + n_down
    n_slots = wring.shape[0]
    request_steps = max(1, n_up // 2)
    rows_per_step = tm // request_steps
    up_phase = j < n_up
    wslot = (i * steps + j) % n_slots

    def weight_copy(src, s, piece):
        return pltpu.make_async_copy(src, wring.at[s, piece], wsem.at[s])

    def request_weights(ahead):
        wrap = j + ahead >= steps
        tile = jnp.where(wrap, i + 1, i)
        step = jnp.where(wrap, j + ahead - steps, j + ahead)
        s = (i * steps + j + ahead) % n_slots
        e = te_ref[tile]
        wanted = tile < n_used

        @pl.when(jnp.logical_and(wanted, step < n_up))
        def _():
            col = pl.multiple_of(step * tf, tf)
            weight_copy(wgu_hbm.at[e, :, pl.ds(col, tf)], s, 0).start()
            weight_copy(wgu_hbm.at[e, :, pl.ds(f + col, tf)], s, 1).start()

        @pl.when(jnp.logical_and(wanted, step >= n_up))
        def _():
            col = pl.multiple_of((step - n_up) * 2 * tf, 2 * tf)
            weight_copy(wd_hbm.at[e, :, pl.ds(col, tf)], s, 0).start()
            weight_copy(wd_hbm.at[e, :, pl.ds(col + tf, tf)], s, 1).start()

    @pl.when(jnp.logical_and(i == 0, j == 0))
    def _():
        for ahead in range(n_slots - 1):
            request_weights(ahead)

    @pl.when(used)
    def _():
        request_weights(n_slots - 1)
        for piece in range(2):
            weight_copy(wd_hbm.at[0, :, pl.ds(0, tf)], wslot, piece).wait()

    def row_copy(q, r):
        return pltpu.make_async_copy(x_hbm.at[rows_ref[q]], xbuf.at[r], sem)

    def gather_next_rows():
        r0 = pl.multiple_of(j * rows_per_step, rows_per_step)
        q = q0_ref[i + 1] + r0
        stride = next(s for s in range(rows_per_step // 4 + 1, rows_per_step + 2)
                      if math.gcd(s, rows_per_step) == 1)
        for n in range(rows_per_step):
            r = n * stride % rows_per_step
            row_copy(q + r, r0 + r).start(priority=n % 2)

    @pl.when(jnp.logical_and(i == 0, j == 0))
    def _():
        def body(r, carry):
            row_copy(q0_ref[0] + r, r).start(priority=ROW_DMA_QUEUE)
            return carry
        lax.fori_loop(0, tm, body, 0)

    @pl.when(jnp.logical_and(i <= n_used, j == 0))
    def _():
        pltpu.make_async_copy(xbuf, xbuf, sem).wait()

    @pl.when(jnp.logical_and(used, j == 0))
    def _():
        for s in range(tm // sub):
            r = slice(s * sub, (s + 1) * sub)
            lo, hi = _unpack_bf16_pairs(xbuf[r, :])
            xb_ref[r, :half] = lo
            xb_ref[r, half:] = hi

    @pl.when(jnp.logical_and(used, j < request_steps))
    def _():
        gather_next_rows()

    def activation(x, wg, wu):
        gate = jnp.dot(x, wg, preferred_element_type=F32) + bg_ref[0]
        up = jnp.dot(x, wu, preferred_element_type=F32) + bu_ref[0]
        gate = jnp.minimum(gate, SWIGLU_LIMIT)
        up = jnp.clip(up, -SWIGLU_LIMIT, SWIGLU_LIMIT)
        glu = gate * jax.nn.sigmoid(SWIGLU_ALPHA * gate)
        return ((up + 1.0) * glu).astype(BF16)

    def down(rows, wd, cols):
        act = jnp.concatenate([act_ref[c, rows, :] for c in range(n_up)], axis=1)
        return jnp.dot(act, wd, preferred_element_type=F32) + bd_ref[0, :, cols]

    whole = jnp.logical_and(used, nv > tm - sub)
    partial = jnp.logical_and(used, nv <= tm - sub)

    @pl.when(jnp.logical_and(whole, up_phase))
    def _():
        wg = wring[wslot, 0].astype(BF16)
        wu = wring[wslot, 1].astype(BF16)
        for part in range(2):
            rows = slice(part * tm // 2, (part + 1) * tm // 2)
            act_ref[j, rows, :] = activation(xb_ref[rows, :], wg, wu)

    @pl.when(jnp.logical_and(whole, jnp.logical_not(up_phase)))
    def _():
        for piece in range(2):
            cols = slice(piece * tf, (piece + 1) * tf)
            o_ref[:, cols] = down(slice(None), wring[wslot, piece].astype(BF16), cols)

    for s in range(tm // sub):
        r = slice(s * sub, (s + 1) * sub)
        live = jnp.logical_and(partial, s * sub < nv)
        dead = jnp.logical_not(jnp.logical_or(whole, live))

        @pl.when(jnp.logical_and(live, up_phase))
        def _():
            act_ref[j, r, :] = activation(xb_ref[r, :], wring[wslot, 0].astype(BF16),
                                          wring[wslot, 1].astype(BF16))

        @pl.when(jnp.logical_and(live, jnp.logical_not(up_phase)))
        def _():
            for piece in range(2):
                cols = slice(piece * tf, (piece + 1) * tf)
                o_ref[r, cols] = down(r, wring[wslot, piece].astype(BF16), cols)

        @pl.when(jnp.logical_and(dead, jnp.logical_not(up_phase)))
        def _():
            o_ref[r, :] = jnp.zeros((sub, o_ref.shape[1]), F32)


def _experts(tile_expert, tile_valid, tile_q0, n_used, sorted_rows, x_packed,
             w_gate_up, b_gate_up, w_down, b_down, *, n_tiles):
    tm, sub = EXPERT_TILE, EXPERT_SUB
    half = x_packed.shape[1]
    ne, d, f2 = w_gate_up.shape
    f = f2 // 2
    assert d == 2 * half
    tf = EXPERT_COLS
    tn = 2 * tf
    n_up, n_down = f // tf, d // tn
    assert tm % n_up == 0 and tm % sub == 0
    assert f == d

    def up_chunk(i, j, nt):
        return jnp.where(i < nt[0], jnp.minimum(j, n_up - 1), n_up - 1)

    def down_chunk(i, j, nt):
        return jnp.where(i < nt[0], jnp.maximum(j - n_up, 0), 0)

    b_gu = b_gate_up.reshape(ne, 1, f2)
    b_d = b_down.reshape(ne, 1, d)
    return pl.pallas_call(
        functools.partial(_expert_kernel, tm=tm, sub=sub, n_up=n_up, n_down=n_down),
        grid_spec=pltpu.PrefetchScalarGridSpec(
            num_scalar_prefetch=5,
            grid=(n_tiles, n_up + n_down),
            in_specs=[
                pl.BlockSpec(memory_space=pl.ANY),
                pl.BlockSpec(memory_space=pl.ANY),
                pl.BlockSpec(memory_space=pl.ANY),
                pl.BlockSpec((1, 1, tf), lambda i, j, te, nv, q0, nt, rows: (te[i], 0, up_chunk(i, j, nt))),
                pl.BlockSpec((1, 1, tf), lambda i, j, te, nv, q0, nt, rows: (te[i], 0, n_up + up_chunk(i, j, nt))),
                pl.BlockSpec((1, 1, tn), lambda i, j, te, nv, q0, nt, rows: (te[i], 0, down_chunk(i, j, nt))),
            ],
            out_specs=pl.BlockSpec((tm, tn), lambda i, j, te, nv, q0, nt, rows: (i, jnp.maximum(j - n_up, 0))),
            scratch_shapes=[
                pltpu.VMEM((tm, half), jnp.uint32),
                pltpu.VMEM((tm, d), BF16),
                pltpu.VMEM((n_up, tm, tf), BF16),
                pltpu.VMEM((EXPERT_WEIGHT_SLOTS, 2, d, tf), F32),
                pltpu.SemaphoreType.DMA,
                pltpu.SemaphoreType.DMA((EXPERT_WEIGHT_SLOTS,)),
            ],
        ),
        out_shape=jax.ShapeDtypeStruct((n_tiles * tm, d), F32),
        compiler_params=_compiler_params(("arbitrary", "arbitrary")),
        name="experts",
    )(tile_expert, tile_valid, tile_q0, n_used, sorted_rows, x_packed,
      w_gate_up, w_down, b_gu, b_gu, b_d)


def _combine_kernel(dest_ref, h1_ref, gate_ref, y_hbm, o_ref, buf, sem, *, tt, tiles_per_batch,
                    tok_per_batch, n_steps):
    t = pl.program_id(0)
    slot = t % 2

    def request_rows(step, s):
        base = ((step // tiles_per_batch) * tok_per_batch + N_META + (step % tiles_per_batch) * tt) * TOP_K
        for r in range(tt):
            for k in range(TOP_K):
                pltpu.make_async_copy(y_hbm.at[dest_ref[base + r * TOP_K + k]], buf.at[s, k, r],
                                      sem.at[s]).start(priority=ROW_DMA_QUEUE)

    def wait_rows(s):
        pltpu.make_async_copy(buf.at[s], buf.at[s], sem.at[s]).wait()

    @pl.when(t == 0)
    def _():
        request_rows(0, 0)

    wait_rows(slot)
    request_rows(jnp.minimum(t + 1, n_steps - 1), 1 - slot)
    g = gate_ref[0]
    moe = buf[slot, 0] * g[:, 0:1]
    for k in range(1, TOP_K):
        moe = moe + buf[slot, k] * g[:, k:k + 1]
    o_ref[0] = h1_ref[0] + moe

    @pl.when(t == n_steps - 1)
    def _():
        wait_rows(1 - slot)


def _combine(dest_flat, h1, gates, y_sorted, *, tt):
    b, lp, d = h1.shape
    s = lp - CHUNK
    tok_per_batch = s + N_META
    tiles = s // tt
    off = CHUNK // tt
    gates_tok = gates[:, N_META:, :]
    n_steps = b * tiles
    return pl.pallas_call(
        functools.partial(_combine_kernel, tt=tt, tiles_per_batch=tiles, tok_per_batch=tok_per_batch,
                          n_steps=n_steps),
        grid_spec=pltpu.PrefetchScalarGridSpec(
            num_scalar_prefetch=1,
            grid=(n_steps,),
            in_specs=[
                pl.BlockSpec((1, tt, d), lambda t, dest: (t // tiles, off + t % tiles, 0)),
                pl.BlockSpec((1, tt, TOP_K), lambda t, dest: (t // tiles, t % tiles, 0)),
                pl.BlockSpec(memory_space=pl.ANY),
            ],
            out_specs=pl.BlockSpec((1, tt, d), lambda t, dest: (t // tiles, t % tiles, 0)),
            scratch_shapes=[pltpu.VMEM((2, TOP_K, tt, d), F32), pltpu.SemaphoreType.DMA((2,))],
        ),
        out_shape=jax.ShapeDtypeStruct((b, s, d), F32),
        compiler_params=_compiler_params(("arbitrary",)),
        name="combine",
    )(dest_flat, h1, gates_tok, y_sorted)


def _lookup(table, idx):
    n = table.shape[0]
    hit = idx[..., None] == jnp.arange(n, dtype=idx.dtype)
    return jnp.sum(jnp.where(hit, table, 0), axis=-1)


def _routing(top_idx, rank, counts, lp, n_pad, *, tm):
    b, l, k = top_idx.shape
    ne = counts.shape[0]
    p = b * l * k
    n_tiles = p // tm + ne + 1
    tiles_per = (counts + tm - 1) // tm
    tile_end = jnp.cumsum(tiles_per)
    tile_start = tile_end - tiles_per
    n_used = tile_end[-1]
    starts = jnp.cumsum(counts) - counts
    tile_id = jnp.arange(n_tiles, dtype=jnp.int32)
    tile_expert = jnp.minimum(jnp.sum(tile_end[None, :] <= tile_id[:, None], axis=-1), ne - 1).astype(jnp.int32)
    within = tile_id - _lookup(tile_start, tile_expert)
    tile_valid = jnp.where(tile_id < n_used, jnp.clip(_lookup(counts, tile_expert) - within * tm, 0, tm), 0)
    tile_q0 = jnp.where(tile_id < n_used, _lookup(starts, tile_expert) + within * tm, 0)
    tok = jnp.arange(b * l, dtype=jnp.int32)
    tok_bits = max(1, (b * l - 1).bit_length())
    keys = jnp.sort((top_idx.reshape(b * l, k) << tok_bits | tok[:, None]).reshape(p))
    sorted_tok = keys & ((1 << tok_bits) - 1)
    sorted_rows = (sorted_tok // l) * lp + n_pad + sorted_tok % l
    sorted_rows = jnp.concatenate([sorted_rows, jnp.arange(tm, dtype=jnp.int32) % (b * lp)])
    dest = _lookup(tile_start * tm, top_idx) + rank
    return (tile_expert, tile_valid.astype(jnp.int32), tile_q0.astype(jnp.int32),
            n_used.astype(jnp.int32).reshape(1), sorted_rows.astype(jnp.int32),
            dest.astype(jnp.int32).reshape(p), n_tiles)


def kernel(x, meta_tokens, attn_norm_gain, w_in, ret_decay_fwd, ret_decay_bwd, ret_out_gain,
           diff_q_gain, diff_k_gain, diff_lambda_q1, diff_lambda_k1, diff_lambda_q2, diff_lambda_k2,
           diff_out_gain, w_out, ffn_norm_gain, w_router, b_router, w_gate_up, b_gate_up,
           w_down, b_down):
    b, s, d = x.shape
    assert s % CHUNK == 0 and w_in.shape[0] == 1, "single layer, token count a chunk multiple"
    n_pad = CHUNK - N_META
    lp = s + CHUNK
    l = s + N_META

    hp, u = _prep(x, meta_tokens.astype(x.dtype), attn_norm_gain[0])
    z = _in_proj(u.reshape(b * lp, d), w_in[0].astype(BF16), tm=lp // 4, tn=1024)
    z = z.reshape(b, lp, -1)

    lg_f = jnp.log1p(-jnp.exp2(-ret_decay_fwd[0].astype(F32)))
    lg_b = jnp.log1p(-jnp.exp2(-ret_decay_bwd[0].astype(F32)))
    ret = _retention(z, lg_f, lg_b, ret_out_gain[0])

    lam = (jnp.exp(jnp.sum(diff_lambda_q1[0].astype(F32) * diff_lambda_k1[0].astype(F32)))
           - jnp.exp(jnp.sum(diff_lambda_q2[0].astype(F32) * diff_lambda_k2[0].astype(F32)))
           + LAMBDA_INIT).reshape(1)
    slopes = jnp.exp2(-8.0 * jnp.arange(1, DIFF_HEADS + 1, dtype=F32) / DIFF_HEADS)
    att = _attention(z, lam, slopes, diff_q_gain[0], diff_k_gain[0], diff_out_gain[0], tq=384, tk=384)

    h1, xt, route, counts = _out_proj(
        ret.reshape(b * lp, -1), att.reshape(b * lp, -1), hp.reshape(b * lp, d),
        w_out[0].astype(BF16), ffn_norm_gain[0], w_router[0], b_router[0], tm=lp // 8, lp=lp, n_pad=n_pad)

    route = route.reshape(b, lp, 128)[:, n_pad:, :]
    top_idx = route[:, :, :TOP_K].astype(jnp.int32)
    gates = route[:, :, TOP_K:2 * TOP_K]
    rank = route[:, :, 2 * TOP_K:3 * TOP_K].astype(jnp.int32)

    tile_expert, tile_valid, tile_q0, n_used, sorted_rows, dest, n_tiles = _routing(
        top_idx, rank, counts.reshape(-1).astype(jnp.int32), lp, n_pad, tm=EXPERT_TILE)
    ys = _experts(tile_expert, tile_valid, tile_q0, n_used, sorted_rows, xt,
                  w_gate_up[0], b_gate_up[0], w_down[0], b_down[0], n_tiles=n_tiles)
    return _combine(dest, h1.reshape(b, lp, d), gates, ys, tt=CHUNK)
```

```python
import functools
import math

import jax
import jax.numpy as jnp
from jax import lax
from jax.experimental import pallas as pl
from jax.experimental.pallas import tpu as pltpu

N_META = 16
CHUNK = 128
RET_HEADS = 4
RET_QK_DIM = 128
RET_V_DIM = 256
DIFF_HEADS = 4
DIFF_QK_DIM = 128
DIFF_V_DIM = 256
TOP_K = 4
SWIGLU_ALPHA = 1.702
SWIGLU_LIMIT = 7.0
NORM_EPS = 1e-6
RET_DECAY_BASE = 5.0
LAMBDA_INIT = 0.8 - 0.6 * math.exp(-0.3 * 0)
NEG_BIG = -1e30

V7X_VMEM_BYTES = 64 * 1024 * 1024
VMEM_LIMIT_BYTES = V7X_VMEM_BYTES - 8 * 1024 * 1024

F32 = jnp.float32
BF16 = jnp.bfloat16

ROW_DMA_QUEUE = 1


def _compiler_params(semantics):
    return pltpu.CompilerParams(dimension_semantics=semantics, vmem_limit_bytes=VMEM_LIMIT_BYTES)


def _prep_kernel(x_ref, meta_ref, g_ref, hp_ref, u_ref, *, n_pad):
    c = pl.program_id(1)

    def emit(h):
        hp_ref[0] = h
        ms = jnp.mean(h * h, axis=-1, keepdims=True)
        u_ref[0] = (h * lax.rsqrt(ms + NORM_EPS) * g_ref[...]).astype(u_ref.dtype)

    @pl.when(c == 0)
    def _():
        d = meta_ref.shape[-1]
        emit(jnp.concatenate([jnp.zeros((n_pad, d), F32), meta_ref[...]], axis=0))

    @pl.when(c > 0)
    def _():
        emit(x_ref[0])


def _prep(x, meta, gain):
    b, s, d = x.shape
    n_pad = CHUNK - N_META
    nc = s // CHUNK + 1
    lp = nc * CHUNK
    return pl.pallas_call(
        functools.partial(_prep_kernel, n_pad=n_pad),
        grid=(b, nc),
        in_specs=[
            pl.BlockSpec((1, CHUNK, d), lambda i, c: (i, jnp.maximum(c - 1, 0), 0)),
            pl.BlockSpec((N_META, d), lambda i, c: (0, 0)),
            pl.BlockSpec((1, d), lambda i, c: (0, 0)),
        ],
        out_specs=[
            pl.BlockSpec((1, CHUNK, d), lambda i, c: (i, c, 0)),
            pl.BlockSpec((1, CHUNK, d), lambda i, c: (i, c, 0)),
        ],
        out_shape=[
            jax.ShapeDtypeStruct((b, lp, d), F32),
            jax.ShapeDtypeStruct((b, lp, d), BF16),
        ],
        compiler_params=_compiler_params(("parallel", "arbitrary")),
        name="prep",
    )(x, meta, gain.reshape(1, d))


def _matmul_kernel(a_ref, b_ref, o_ref):
    o_ref[...] = jnp.dot(a_ref[...], b_ref[...], preferred_element_type=F32).astype(o_ref.dtype)


def _in_proj(u2d, w_bf16, *, tm, tn):
    m, k = u2d.shape
    n = w_bf16.shape[1]
    return pl.pallas_call(
        _matmul_kernel,
        grid=(m // tm, n // tn),
        in_specs=[
            pl.BlockSpec((tm, k), lambda i, j: (i, 0)),
            pl.BlockSpec((k, tn), lambda i, j: (0, j)),
        ],
        out_specs=pl.BlockSpec((tm, tn), lambda i, j: (i, j)),
        out_shape=jax.ShapeDtypeStruct((m, n), BF16),
        compiler_params=_compiler_params(("parallel", "arbitrary")),
        name="in_proj",
    )(u2d, w_bf16)


def _retention_kernel(lgf_ref, lgb_ref, q_ref, k_ref, v_ref, g_ref, gain_ref, o_ref,
                      ob_ref, sf_ref, sb_ref, dec_ref, *, nc, heads):
    c = CHUNK
    dk, dv = RET_QK_DIM, RET_V_DIM
    scale = dk ** -0.5
    row = lax.broadcasted_iota(jnp.int32, (c, c), 0)
    col = lax.broadcasted_iota(jnp.int32, (c, c), 1)
    rel = (row - col).astype(F32)
    pos = lax.broadcasted_iota(jnp.int32, (c, 1), 0).astype(F32)
    q_dec_f, k_dec_f, q_dec_b, k_dec_b, chunk_f, chunk_b = [], [], [], [], [], []
    for hh in range(heads):
        lgf = lgf_ref[pl.program_id(1) * heads + hh]
        lgb = lgb_ref[pl.program_id(1) * heads + hh]
        dec_ref[hh] = jnp.where(rel >= 0, jnp.exp(lgf * jnp.maximum(rel, 0.0)),
                                jnp.exp(lgb * jnp.maximum(-rel, 0.0))) * scale
        q_dec_f.append(jnp.exp(lgf * (pos + 1.0)))
        k_dec_f.append(jnp.exp(lgf * (c - 1.0 - pos)) * scale)
        q_dec_b.append(jnp.exp(lgb * (c - pos)))
        k_dec_b.append(jnp.exp(lgb * pos) * scale)
        chunk_f.append(jnp.exp(lgf * c))
        chunk_b.append(jnp.exp(lgb * c))

    def rows(n):
        return pl.ds(pl.multiple_of(n * c, c), c)

    def qk_cols(hh):
        return slice(hh * dk, (hh + 1) * dk)

    def v_cols(hh):
        return slice(hh * dv, (hh + 1) * dv)

    def state_update(s_ref, hh, kn, vn, k_dec, chunk_decay):
        kd_t = (kn.astype(F32) * k_dec).T.astype(BF16)
        s_ref[hh] = s_ref[hh] * chunk_decay + jnp.dot(kd_t, vn, preferred_element_type=F32)

    sb_ref[...] = jnp.zeros_like(sb_ref)
    sf_ref[...] = jnp.zeros_like(sf_ref)

    def bwd_body(t, carry):
        n = nc - 1 - t
        for hh in range(heads):
            qn = q_ref[0, rows(n), qk_cols(hh)]
            kn = k_ref[0, rows(n), qk_cols(hh)]
            vn = v_ref[0, rows(n), v_cols(hh)]
            qd = (qn.astype(F32) * q_dec_b[hh]).astype(BF16)
            ob_ref[rows(n), v_cols(hh)] = jnp.dot(qd, sb_ref[hh].astype(BF16), preferred_element_type=F32)
            state_update(sb_ref, hh, kn, vn, k_dec_b[hh], chunk_b[hh])
        return carry

    lax.fori_loop(0, nc, bwd_body, 0)

    def fwd_body(n, carry):
        for hh in range(heads):
            qn = q_ref[0, rows(n), qk_cols(hh)]
            kn = k_ref[0, rows(n), qk_cols(hh)]
            vn = v_ref[0, rows(n), v_cols(hh)]
            sc = lax.dot_general(qn, kn, (((1,), (1,)), ((), ())), preferred_element_type=F32)
            inner = jnp.dot((sc * dec_ref[hh]).astype(BF16), vn, preferred_element_type=F32)
            qd = (qn.astype(F32) * q_dec_f[hh]).astype(BF16)
            cross = jnp.dot(qd, sf_ref[hh].astype(BF16), preferred_element_type=F32)
            o = inner + cross + ob_ref[rows(n), v_cols(hh)]
            state_update(sf_ref, hh, kn, vn, k_dec_f[hh], chunk_f[hh])
            o = o * lax.rsqrt(jnp.mean(o * o, axis=-1, keepdims=True) + NORM_EPS) * gain_ref[:, v_cols(hh)]
            gate = g_ref[0, rows(n), v_cols(hh)].astype(F32)
            o_ref[0, rows(n), v_cols(hh)] = (o * (gate * jax.nn.sigmoid(gate))).astype(o_ref.dtype)
        return carry

    lax.fori_loop(0, nc, fwd_body, 0)


def _retention(z, lg_f, lg_b, ret_gain):
    b, lp, _ = z.shape
    nc = lp // CHUNK
    dk, dv, nh = RET_QK_DIM, RET_V_DIM, RET_HEADS
    heads = 2
    groups = nh // heads
    k_blk0 = groups
    v_blk0 = 2 * nh * dk // (heads * dv)
    g_blk0 = v_blk0 + groups
    smem = pl.BlockSpec(memory_space=pltpu.SMEM)
    return pl.pallas_call(
        functools.partial(_retention_kernel, nc=nc, heads=heads),
        grid=(b, groups),
        in_specs=[
            smem, smem,
            pl.BlockSpec((1, lp, heads * dk), lambda i, h: (i, 0, h)),
            pl.BlockSpec((1, lp, heads * dk), lambda i, h: (i, 0, k_blk0 + h)),
            pl.BlockSpec((1, lp, heads * dv), lambda i, h: (i, 0, v_blk0 + h)),
            pl.BlockSpec((1, lp, heads * dv), lambda i, h: (i, 0, g_blk0 + h)),
            pl.BlockSpec((1, heads * dv), lambda i, h: (0, h)),
        ],
        out_specs=pl.BlockSpec((1, lp, heads * dv), lambda i, h: (i, 0, h)),
        out_shape=jax.ShapeDtypeStruct((b, lp, nh * dv), BF16),
        scratch_shapes=[
            pltpu.VMEM((lp, heads * dv), F32),
            pltpu.VMEM((heads, dk, dv), F32),
            pltpu.VMEM((heads, dk, dv), F32),
            pltpu.VMEM((heads, CHUNK, CHUNK), F32),
        ],
        compiler_params=_compiler_params(("parallel", "arbitrary")),
        name="retention",
    )(lg_f, lg_b, z, z, z, z, ret_gain.reshape(1, nh * dv))


def _attn_kernel(lam_ref, slope_ref, q_ref, k_ref, v_ref, qg_ref, kg_ref, og_ref, o_ref,
                 kaug_ref, corr_ref, qaug_ref, s_ref, p_ref, m_ref, *, tq, tk, n_pad):
    h = pl.program_id(1)
    qi = pl.program_id(2)
    lp = k_ref.shape[1]
    d = DIFF_QK_DIM
    n_chunks = lp // tk
    lanes = 128
    lam = lam_ref[0]
    log2e = math.log2(math.e)
    slope = slope_ref[h] * log2e
    contract_last = (((1,), (1,)), ((), ()))

    def qk_norm(x, gain):
        return x * lax.rsqrt(jnp.mean(x * x, axis=-1, keepdims=True) + NORM_EPS) * gain

    def split3(v):
        hi = v.astype(BF16).astype(F32)
        mid = (v - hi).astype(BF16).astype(F32)
        lo = (v - hi - mid).astype(BF16).astype(F32)
        return [hi, mid, lo]

    def extra_lanes(cols):
        rows = cols[0].shape[0]
        lane = lax.broadcasted_iota(jnp.int32, (rows, d), 1)
        out = jnp.zeros((rows, d), F32)
        for t, col in enumerate(cols):
            out = jnp.where(lane == t, col, out)
        return out.astype(BF16)

    @pl.when(qi == 0)
    def _():
        for c in range(n_chunks):
            r = slice(c * tk, (c + 1) * tk)
            kk = k_ref[0, r, :].astype(F32)
            k_row = c * tk + lax.broadcasted_iota(jnp.int32, (tk, 1), 0)
            real = k_row >= CHUNK
            one = jnp.where(real, 1.0, 0.0)
            first = jnp.where(k_row < n_pad, NEG_BIG, 0.0)
            sk = [jnp.where(real, t, 0.0) for t in split3(slope * k_row.astype(F32))]
            extra = extra_lanes([sk[0] + first, sk[1], sk[2], one, one, one])
            for m in range(2):
                kaug_ref[m, r, :d] = qk_norm(kk[:, m * d:(m + 1) * d], kg_ref[...]).astype(BF16)
                kaug_ref[m, r, d:] = extra
        i_loc = lax.broadcasted_iota(jnp.int32, (tq, tk), 0)
        j_loc = lax.broadcasted_iota(jnp.int32, (tq, tk), 1)
        corr_ref[...] = (2.0 * slope) * jnp.maximum(j_loc - i_loc, 0).astype(F32)

    qq = q_ref[0].astype(F32)
    qs = [(qk_norm(qq[:, m * d:(m + 1) * d], qg_ref[...]) * (d ** -0.5 * log2e)).astype(BF16)
          for m in range(2)]
    q_row = qi * tq + lax.broadcasted_iota(jnp.int32, (tq, 1), 0)
    real_q = q_row >= CHUNK
    sq = [jnp.where(real_q, t, 0.0) for t in split3(slope * q_row.astype(F32))]
    one = jnp.where(real_q, 1.0, 0.0)
    extra_before = extra_lanes([one, one, one] + [-t for t in sq])
    extra_after = extra_lanes([-one, -one, -one] + sq)
    for m in range(2):
        qaug_ref[0, m] = jnp.concatenate([qs[m], extra_before], axis=1)
        qaug_ref[1, m] = jnp.concatenate([qs[m], extra_after], axis=1)

    def lane_fold(x, op):
        out = x[:, :lanes]
        for t in range(1, x.shape[1] // lanes):
            out = op(out, x[:, t * lanes:(t + 1) * lanes])
        return out

    m_ref[...] = jnp.full_like(m_ref, NEG_BIG)
    for c in range(n_chunks):
        r = slice(c * tk, (c + 1) * tk)
        side = jnp.where(c <= qi, 0, 1)
        for m in range(2):
            s = lax.dot_general(qaug_ref[side, m], kaug_ref[m, r, :], contract_last,
                                preferred_element_type=F32)
            s_ref[m, c] = s
            m_ref[m] = jnp.maximum(m_ref[m], jnp.where(c == qi, NEG_BIG, lane_fold(s, jnp.maximum)))
    for m in range(2):
        s = s_ref[m, qi] - corr_ref[...]
        s_ref[m, qi] = s
        m_ref[m] = jnp.maximum(m_ref[m], lane_fold(s, jnp.maximum))

    @pl.when(qi == 0)
    def _():
        q_pos = lax.broadcasted_iota(jnp.int32, (tq, tk), 0)
        k_pos = lax.broadcasted_iota(jnp.int32, (tq, tk), 1)
        dist = jnp.abs(q_pos - k_pos).astype(F32)
        both_real = jnp.logical_and(q_pos >= CHUNK, k_pos >= CHUNK)
        bias = jnp.where(both_real, -slope * dist, 0.0)
        for m in range(2):
            s = lax.dot_general(qs[m], kaug_ref[m, :tk, :d], contract_last, preferred_element_type=F32)
            s = jnp.where(k_pos >= n_pad, s + bias, NEG_BIG)
            s_ref[m, 0] = s
            m_ref[m] = jnp.maximum(m_ref[m], lane_fold(s, jnp.maximum))

    band = lanes
    inv_l = []
    for m in range(2):
        for rb in range(tq // band):
            rows = slice(rb * band, (rb + 1) * band)
            row_max = jnp.max(m_ref[m, rows, :], axis=-1, keepdims=True)
            shift = jnp.concatenate([jnp.broadcast_to(row_max, (band, lanes))] * (tk // lanes), axis=1)
            l_part = jnp.zeros((band, lanes), F32)
            for c in range(n_chunks):
                e = jnp.exp2(s_ref[m, c, rows, :] - shift)
                l_part = l_part + lane_fold(e, jnp.add)
                p_ref[m * tq + rb * band:m * tq + (rb + 1) * band, c * tk:(c + 1) * tk] = e.astype(BF16)
            inv_l.append(1.0 / jnp.sum(l_part, axis=-1, keepdims=True))
    acc = jnp.dot(p_ref[...], v_ref[0], preferred_element_type=F32) * jnp.concatenate(inv_l, axis=0)

    o = acc[:tq] - lam * acc[tq:]
    o = o * lax.rsqrt(jnp.mean(o * o, axis=-1, keepdims=True) + NORM_EPS) * og_ref[...]
    o_ref[0] = (o * (1.0 - LAMBDA_INIT)).astype(o_ref.dtype)


def _attention(z, lam, slopes, q_gain, k_gain, out_gain, *, tq, tk):
    b, lp, _ = z.shape
    nh, d, dv = DIFF_HEADS, DIFF_QK_DIM, DIFF_V_DIM
    assert tq == tk and tq >= CHUNK and lp % tq == 0
    ret_cols = 2 * RET_HEADS * RET_QK_DIM + 2 * RET_HEADS * RET_V_DIM
    q_blk0 = ret_cols // (2 * d)
    k_blk0 = q_blk0 + nh
    v_blk0 = k_blk0 + nh
    smem = pl.BlockSpec(memory_space=pltpu.SMEM)
    return pl.pallas_call(
        functools.partial(_attn_kernel, tq=tq, tk=tk, n_pad=CHUNK - N_META),
        grid=(b, nh, lp // tq),
        in_specs=[
            smem, smem,
            pl.BlockSpec((1, tq, 2 * d), lambda i, h, q: (i, q, q_blk0 + h)),
            pl.BlockSpec((1, lp, 2 * d), lambda i, h, q: (i, 0, k_blk0 + h)),
            pl.BlockSpec((1, lp, dv), lambda i, h, q: (i, 0, v_blk0 + h)),
            pl.BlockSpec((1, d), lambda i, h, q: (0, 0)),
            pl.BlockSpec((1, d), lambda i, h, q: (0, 0)),
            pl.BlockSpec((1, dv), lambda i, h, q: (0, h)),
        ],
        out_specs=pl.BlockSpec((1, tq, dv), lambda i, h, q: (i, q, h)),
        out_shape=jax.ShapeDtypeStruct((b, lp, nh * dv), BF16),
        scratch_shapes=[
            pltpu.VMEM((2, lp, 2 * d), BF16),
            pltpu.VMEM((tq, tk), F32),
            pltpu.VMEM((2, 2, tq, 2 * d), BF16),
            pltpu.VMEM((2, lp // tk, tq, tk), F32),
            pltpu.VMEM((2 * tq, lp), BF16),
            pltpu.VMEM((2, tq, 128), F32),
        ],
        compiler_params=_compiler_params(("parallel", "parallel", "arbitrary")),
        name="attention",
    )(lam, slopes, z, z, z, q_gain.reshape(1, d), k_gain.reshape(1, d), out_gain.reshape(1, nh * dv))


def _pack_bf16_pairs(x):
    n = x.shape[1] // 2
    xb = x.astype(BF16).astype(F32)
    lo = lax.bitcast_convert_type(xb[:, :n], jnp.uint32)
    hi = lax.bitcast_convert_type(xb[:, n:], jnp.uint32)
    return (hi & jnp.uint32(0xFFFF0000)) | (lo >> 16)


def _unpack_bf16_pairs(p):
    lo = lax.bitcast_convert_type(p << 16, F32).astype(BF16)
    hi = lax.bitcast_convert_type(p & jnp.uint32(0xFFFF0000), F32).astype(BF16)
    return lo, hi


def _out_proj_kernel(ret_ref, att_ref, hp_ref, w_ref, g_ref, wr_ref, br_ref,
                     h1_ref, xt_ref, route_ref, cnt_ref, run_ref, *, lp, n_pad):
    i = pl.program_id(0)

    @pl.when(i == 0)
    def _():
        run_ref[...] = jnp.zeros_like(run_ref)

    kr = ret_ref.shape[1]
    y = jnp.dot(ret_ref[...], w_ref[:kr, :], preferred_element_type=F32)
    y = y + jnp.dot(att_ref[...], w_ref[kr:, :], preferred_element_type=F32)
    h1 = hp_ref[...] + y
    h1_ref[...] = h1
    xt = h1 * lax.rsqrt(jnp.mean(h1 * h1, axis=-1, keepdims=True) + NORM_EPS) * g_ref[...]
    xt_ref[...] = _pack_bf16_pairs(xt)
    xt_hi = xt.astype(BF16)
    xt_lo = (xt - xt_hi.astype(F32)).astype(BF16)
    wr = wr_ref[...]
    wr_hi = wr.astype(BF16)
    wr_lo = (wr - wr_hi.astype(F32)).astype(BF16)
    logits = (jnp.dot(xt_hi, wr_hi, preferred_element_type=F32)
              + jnp.dot(xt_lo, wr_hi, preferred_element_type=F32)
              + jnp.dot(xt_hi, wr_lo, preferred_element_type=F32)) + br_ref[...]
    tm, ne = logits.shape
    lane = lax.broadcasted_iota(jnp.int32, (tm, ne), 1).astype(F32)
    out_lane = lax.broadcasted_iota(jnp.int32, route_ref.shape, 1)
    vals, idxs = [], []
    work = logits
    for _ in range(TOP_K):
        top = jnp.max(work, axis=-1, keepdims=True)
        idx = jnp.min(jnp.where(work == top, lane, float(ne)), axis=-1, keepdims=True)
        work = jnp.where(lane == idx, -jnp.inf, work)
        vals.append(top)
        idxs.append(idx)
    exps = [jnp.exp(v - vals[0]) for v in vals]
    denom = exps[0] + exps[1] + exps[2] + exps[3]

    local_row = (i * tm) % lp + lax.broadcasted_iota(jnp.int32, (tm, 1), 0)
    member = jnp.zeros((tm, ne), F32)
    for k in range(TOP_K):
        member = member + jnp.where(lane == idxs[k], 1.0, 0.0)
    member = jnp.where(local_row >= n_pad, member, 0.0)
    earlier = (lax.broadcasted_iota(jnp.int32, (tm, tm), 0)
               > lax.broadcasted_iota(jnp.int32, (tm, tm), 1))
    before = run_ref[...] + jnp.dot(jnp.where(earlier, 1.0, 0.0).astype(BF16), member.astype(BF16),
                                    preferred_element_type=F32)
    run_ref[...] = run_ref[...] + jnp.sum(member, axis=0, keepdims=True)
    cnt_ref[...] = run_ref[...]

    slab = jnp.zeros(route_ref.shape, F32)
    for k in range(TOP_K):
        rank = jnp.sum(jnp.where(lane == idxs[k], before, 0.0), axis=-1, keepdims=True)
        slab = jnp.where(out_lane == k, idxs[k], slab)
        slab = jnp.where(out_lane == TOP_K + k, exps[k] / denom, slab)
        slab = jnp.where(out_lane == 2 * TOP_K + k, rank, slab)
    route_ref[...] = slab


def _out_proj(ret2d, att2d, hp2d, w_bf16, gain, w_router, b_router, *, tm, lp, n_pad):
    m, kr = ret2d.shape
    d = hp2d.shape[1]
    ne = w_router.shape[1]
    assert lp % tm == 0
    row = lambda i: (i, 0)
    const = lambda i: (0, 0)
    return pl.pallas_call(
        functools.partial(_out_proj_kernel, lp=lp, n_pad=n_pad),
        grid=(m // tm,),
        in_specs=[
            pl.BlockSpec((tm, kr), row),
            pl.BlockSpec((tm, kr), row),
            pl.BlockSpec((tm, d), row),
            pl.BlockSpec((2 * kr, d), const),
            pl.BlockSpec((1, d), const),
            pl.BlockSpec((d, ne), const),
            pl.BlockSpec((1, ne), const),
        ],
        out_specs=[
            pl.BlockSpec((tm, d), row),
            pl.BlockSpec((tm, d // 2), row),
            pl.BlockSpec((tm, 128), row),
            pl.BlockSpec((1, ne), const),
        ],
        out_shape=[
            jax.ShapeDtypeStruct((m, d), F32),
            jax.ShapeDtypeStruct((m, d // 2), jnp.uint32),
            jax.ShapeDtypeStruct((m, 128), F32),
            jax.ShapeDtypeStruct((1, ne), F32),
        ],
        scratch_shapes=[pltpu.VMEM((1, ne), F32)],
        compiler_params=_compiler_params(("arbitrary",)),
        name="out_proj",
    )(ret2d, att2d, hp2d, w_bf16, gain.reshape(1, d), w_router, b_router.reshape(1, ne))


EXPERT_TILE = 2176
EXPERT_SUB = 272
EXPERT_COLS = 256
EXPERT_WEIGHT_SLOTS = 3


def _expert_kernel(te_ref, nv_ref, q0_ref, nt_ref, rows_ref,
                   x_hbm, wgu_hbm, wd_hbm, bg_ref, bu_ref, bd_ref, o_ref,
                   xbuf, xb_ref, act_ref, wring, sem, wsem, *, tm, sub, n_up, n_down):
    i = pl.program_id(0)
    j = pl.program_id(1)
    n_used = nt_ref[0]
    used = i < n_used
    nv = nv_ref[i]
    half = xbuf.shape[1]
    tf = wring.shape[-1]
    f = wgu_hbm.shape[2] // 2
    steps = n_up + n_down
    n_slots = wring.shape[0]
    rows_per_step = tm // n_up
    up_phase = j < n_up
    wslot = (i * steps + j) % n_slots

    def weight_copy(src, s, piece):
        return pltpu.make_async_copy(src, wring.at[s, piece], wsem.at[s])

    def request_weights(ahead):
        wrap = j + ahead >= steps
        tile = jnp.where(wrap, i + 1, i)
        step = jnp.where(wrap, j + ahead - steps, j + ahead)
        s = (i * steps + j + ahead) % n_slots
        e = te_ref[tile]
        wanted = tile < n_used

        @pl.when(jnp.logical_and(wanted, step < n_up))
        def _():
            col = pl.multiple_of(step * tf, tf)
            weight_copy(wgu_hbm.at[e, :, pl.ds(col, tf)], s, 0).start()
            weight_copy(wgu_hbm.at[e, :, pl.ds(f + col, tf)], s, 1).start()

        @pl.when(jnp.logical_and(wanted, step >= n_up))
        def _():
            col = pl.multiple_of((step - n_up) * 2 * tf, 2 * tf)
            weight_copy(wd_hbm.at[e, :, pl.ds(col, tf)], s, 0).start()
            weight_copy(wd_hbm.at[e, :, pl.ds(col + tf, tf)], s, 1).start()

    @pl.when(jnp.logical_and(i == 0, j == 0))
    def _():
        for ahead in range(n_slots - 1):
            request_weights(ahead)

    @pl.when(used)
    def _():
        request_weights(n_slots - 1)
        for piece in range(2):
            weight_copy(wd_hbm.at[0, :, pl.ds(0, tf)], wslot, piece).wait()

    def row_copy(q, r):
        return pltpu.make_async_copy(x_hbm.at[rows_ref[q]], xbuf.at[r], sem)

    def gather_next_rows():
        r0 = pl.multiple_of(j * rows_per_step, rows_per_step)
        q = q0_ref[i + 1] + r0
        stride = next(s for s in range(rows_per_step // 4 + 1, rows_per_step + 2)
                      if math.gcd(s, rows_per_step) == 1)
        for n in range(rows_per_step):
            r = n * stride % rows_per_step
            row_copy(q + r, r0 + r).start(priority=ROW_DMA_QUEUE)

    @pl.when(jnp.logical_and(i == 0, j == 0))
    def _():
        def body(r, carry):
            row_copy(q0_ref[0] + r, r).start(priority=ROW_DMA_QUEUE)
            return carry
        lax.fori_loop(0, tm, body, 0)

    @pl.when(jnp.logical_and(i <= n_used, j == 0))
    def _():
        pltpu.make_async_copy(xbuf, xbuf, sem).wait()

    @pl.when(jnp.logical_and(used, j == 0))
    def _():
        for s in range(tm // sub):
            r = slice(s * sub, (s + 1) * sub)
            lo, hi = _unpack_bf16_pairs(xbuf[r, :])
            xb_ref[r, :half] = lo
            xb_ref[r, half:] = hi

    def activation(x, wg, wu):
        gate = jnp.dot(x, wg, preferred_element_type=F32) + bg_ref[0]
        up = jnp.dot(x, wu, preferred_element_type=F32) + bu_ref[0]
        gate = jnp.minimum(gate, SWIGLU_LIMIT)
        up = jnp.clip(up, -SWIGLU_LIMIT, SWIGLU_LIMIT)
        glu = gate * jax.nn.sigmoid(SWIGLU_ALPHA * gate)
        return ((up + 1.0) * glu).astype(BF16)

    def down(rows, wd, cols):
        act = jnp.concatenate([act_ref[c, rows, :] for c in range(n_up)], axis=1)
        return jnp.dot(act, wd, preferred_element_type=F32) + bd_ref[0, :, cols]

    whole = jnp.logical_and(used, nv > tm - sub)
    partial = jnp.logical_and(used, nv <= tm - sub)

    @pl.when(jnp.logical_and(whole, up_phase))
    def _():
        gather_next_rows()
        wg = wring[wslot, 0].astype(BF16)
        wu = wring[wslot, 1].astype(BF16)
        for part in range(2):
            rows = slice(part * tm // 2, (part + 1) * tm // 2)
            act_ref[j, rows, :] = activation(xb_ref[rows, :], wg, wu)

    @pl.when(jnp.logical_and(whole, jnp.logical_not(up_phase)))
    def _():
        for piece in range(2):
            cols = slice(piece * tf, (piece + 1) * tf)
            o_ref[:, cols] = down(slice(None), wring[wslot, piece].astype(BF16), cols)

    for s in range(tm // sub):
        r = slice(s * sub, (s + 1) * sub)
        live = jnp.logical_and(partial, s * sub < nv)
        dead = jnp.logical_not(jnp.logical_or(whole, live))

        @pl.when(jnp.logical_and(live, up_phase))
        def _():
            if s == 0:
                gather_next_rows()
            act_ref[j, r, :] = activation(xb_ref[r, :], wring[wslot, 0].astype(BF16),
                                          wring[wslot, 1].astype(BF16))

        @pl.when(jnp.logical_and(live, jnp.logical_not(up_phase)))
        def _():
            for piece in range(2):
                cols = slice(piece * tf, (piece + 1) * tf)
                o_ref[r, cols] = down(r, wring[wslot, piece].astype(BF16), cols)

        @pl.when(jnp.logical_and(dead, jnp.logical_not(up_phase)))
        def _():
            o_ref[r, :] = jnp.zeros((sub, o_ref.shape[1]), F32)


def _experts(tile_expert, tile_valid, tile_q0, n_used, sorted_rows, x_packed,
             w_gate_up, b_gate_up, w_down, b_down, *, n_tiles):
    tm, sub = EXPERT_TILE, EXPERT_SUB
    half = x_packed.shape[1]
    ne, d, f2 = w_gate_up.shape
    f = f2 // 2
    assert d == 2 * half
    tf = EXPERT_COLS
    tn = 2 * tf
    n_up, n_down = f // tf, d // tn
    assert tm % n_up == 0 and tm % sub == 0
    assert f == d

    def up_chunk(i, j, nt):
        return jnp.where(i < nt[0], jnp.minimum(j, n_up - 1), n_up - 1)

    def down_chunk(i, j, nt):
        return jnp.where(i < nt[0], jnp.maximum(j - n_up, 0), 0)

    b_gu = b_gate_up.reshape(ne, 1, f2)
    b_d = b_down.reshape(ne, 1, d)
    return pl.pallas_call(
        functools.partial(_expert_kernel, tm=tm, sub=sub, n_up=n_up, n_down=n_down),
        grid_spec=pltpu.PrefetchScalarGridSpec(
            num_scalar_prefetch=5,
            grid=(n_tiles, n_up + n_down),
            in_specs=[
                pl.BlockSpec(memory_space=pl.ANY),
                pl.BlockSpec(memory_space=pl.ANY),
                pl.BlockSpec(memory_space=pl.ANY),
                pl.BlockSpec((1, 1, tf), lambda i, j, te, nv, q0, nt, rows: (te[i], 0, up_chunk(i, j, nt))),
                pl.BlockSpec((1, 1, tf), lambda i, j, te, nv, q0, nt, rows: (te[i], 0, n_up + up_chunk(i, j, nt))),
                pl.BlockSpec((1, 1, tn), lambda i, j, te, nv, q0, nt, rows: (te[i], 0, down_chunk(i, j, nt))),
            ],
            out_specs=pl.BlockSpec((tm, tn), lambda i, j, te, nv, q0, nt, rows: (i, jnp.maximum(j - n_up, 0))),
            scratch_shapes=[
                pltpu.VMEM((tm, half), jnp.uint32),
                pltpu.VMEM((tm, d), BF16),
                pltpu.VMEM((n_up, tm, tf), BF16),
                pltpu.VMEM((EXPERT_WEIGHT_SLOTS, 2, d, tf), F32),
                pltpu.SemaphoreType.DMA,
                pltpu.SemaphoreType.DMA((EXPERT_WEIGHT_SLOTS,)),
            ],
        ),
        out_shape=jax.ShapeDtypeStruct((n_tiles * tm, d), F32),
        compiler_params=_compiler_params(("arbitrary", "arbitrary")),
        name="experts",
    )(tile_expert, tile_valid, tile_q0, n_used, sorted_rows, x_packed,
      w_gate_up, w_down, b_gu, b_gu, b_d)


def _combine_kernel(dest_ref, h1_ref, gate_ref, y_hbm, o_ref, buf, sem, *, tt, tiles_per_batch,
                    tok_per_batch, n_steps):
    t = pl.program_id(0)
    slot = t % 2

    def request_rows(step, s):
        base = ((step // tiles_per_batch) * tok_per_batch + N_META + (step % tiles_per_batch) * tt) * TOP_K
        for r in range(tt):
            for k in range(TOP_K):
                pltpu.make_async_copy(y_hbm.at[dest_ref[base + r * TOP_K + k]], buf.at[s, k, r],
                                      sem.at[s]).start(priority=(r + k) % 2)

    def wait_rows(s):
        pltpu.make_async_copy(buf.at[s], buf.at[s], sem.at[s]).wait()

    @pl.when(t == 0)
    def _():
        request_rows(0, 0)

    wait_rows(slot)
    request_rows(jnp.minimum(t + 1, n_steps - 1), 1 - slot)
    g = gate_ref[0]
    moe = buf[slot, 0] * g[:, 0:1]
    for k in range(1, TOP_K):
        moe = moe + buf[slot, k] * g[:, k:k + 1]
    o_ref[0] = h1_ref[0] + moe

    @pl.when(t == n_steps - 1)
    def _():
        wait_rows(1 - slot)


def _combine(dest_flat, h1, gates, y_sorted, *, tt):
    b, lp, d = h1.shape
    s = lp - CHUNK
    tok_per_batch = s + N_META
    tiles = s // tt
    off = CHUNK // tt
    gates_tok = gates[:, N_META:, :]
    n_steps = b * tiles
    return pl.pallas_call(
        functools.partial(_combine_kernel, tt=tt, tiles_per_batch=tiles, tok_per_batch=tok_per_batch,
                          n_steps=n_steps),
        grid_spec=pltpu.PrefetchScalarGridSpec(
            num_scalar_prefetch=1,
            grid=(n_steps,),
            in_specs=[
                pl.BlockSpec((1, tt, d), lambda t, dest: (t // tiles, off + t % tiles, 0)),
                pl.BlockSpec((1, tt, TOP_K), lambda t, dest: (t // tiles, t % tiles, 0)),
                pl.BlockSpec(memory_space=pl.ANY),
            ],
            out_specs=pl.BlockSpec((1, tt, d), lambda t, dest: (t // tiles, t % tiles, 0)),
            scratch_shapes=[pltpu.VMEM((2, TOP_K, tt, d), F32), pltpu.SemaphoreType.DMA((2,))],
        ),
        out_shape=jax.ShapeDtypeStruct((b, s, d), F32),
        compiler_params=_compiler_params(("arbitrary",)),
        name="combine",
    )(dest_flat, h1, gates_tok, y_sorted)


def _lookup(table, idx):
    n = table.shape[0]
    hit = idx[..., None] == jnp.arange(n, dtype=idx.dtype)
    return jnp.sum(jnp.where(hit, table, 0), axis=-1)


def _routing(top_idx, rank, counts, lp, n_pad, *, tm):
    b, l, k = top_idx.shape
    ne = counts.shape[0]
    p = b * l * k
    n_tiles = p // tm + ne + 1
    tiles_per = (counts + tm - 1) // tm
    tile_end = jnp.cumsum(tiles_per)
    tile_start = tile_end - tiles_per
    n_used = tile_end[-1]
    starts = jnp.cumsum(counts) - counts
    tile_id = jnp.arange(n_tiles, dtype=jnp.int32)
    tile_expert = jnp.minimum(jnp.sum(tile_end[None, :] <= tile_id[:, None], axis=-1), ne - 1).astype(jnp.int32)
    within = tile_id - _lookup(tile_start, tile_expert)
    tile_valid = jnp.where(tile_id < n_used, jnp.clip(_lookup(counts, tile_expert) - within * tm, 0, tm), 0)
    tile_q0 = jnp.where(tile_id < n_used, _lookup(starts, tile_expert) + within * tm, 0)
    tok = jnp.arange(b * l, dtype=jnp.int32)
    tok_bits = max(1, (b * l - 1).bit_length())
    keys = jnp.sort((top_idx.reshape(b * l, k) << tok_bits | tok[:, None]).reshape(p))
    sorted_tok = keys & ((1 << tok_bits) - 1)
    sorted_rows = (sorted_tok // l) * lp + n_pad + sorted_tok % l
    sorted_rows = jnp.concatenate([sorted_rows, jnp.arange(tm, dtype=jnp.int32) % (b * lp)])
    dest = _lookup(tile_start * tm, top_idx) + rank
    return (tile_expert, tile_valid.astype(jnp.int32), tile_q0.astype(jnp.int32),
            n_used.astype(jnp.int32).reshape(1), sorted_rows.astype(jnp.int32),
            dest.astype(jnp.int32).reshape(p), n_tiles)


def kernel(x, meta_tokens, attn_norm_gain, w_in, ret_decay_fwd, ret_decay_bwd, ret_out_gain,
           diff_q_gain, diff_k_gain, diff_lambda_q1, diff_lambda_k1, diff_lambda_q2, diff_lambda_k2,
           diff_out_gain, w_out, ffn_norm_gain, w_router, b_router, w_gate_up, b_gate_up,
           w_down, b_down):
    b, s, d = x.shape
    assert s % CHUNK == 0 and w_in.shape[0] == 1, "single layer, token count a chunk multiple"
    n_pad = CHUNK - N_META
    lp = s + CHUNK
    l = s + N_META

    hp, u = _prep(x, meta_tokens.astype(x.dtype), attn_norm_gain[0])
    z = _in_proj(u.reshape(b * lp, d), w_in[0].astype(BF16), tm=lp // 4, tn=1024)
    z = z.reshape(b, lp, -1)

    lg_f = jnp.log1p(-jnp.exp2(-ret_decay_fwd[0].astype(F32)))
    lg_b = jnp.log1p(-jnp.exp2(-ret_decay_bwd[0].astype(F32)))
    ret = _retention(z, lg_f, lg_b, ret_out_gain[0])

    lam = (jnp.exp(jnp.sum(diff_lambda_q1[0].astype(F32) * diff_lambda_k1[0].astype(F32)))
           - jnp.exp(jnp.sum(diff_lambda_q2[0].astype(F32) * diff_lambda_k2[0].astype(F32)))
           + LAMBDA_INIT).reshape(1)
    slopes = jnp.exp2(-8.0 * jnp.arange(1, DIFF_HEADS + 1, dtype=F32) / DIFF_HEADS)
    att = _attention(z, lam, slopes, diff_q_gain[0], diff_k_gain[0], diff_out_gain[0], tq=384, tk=384)

    h1, xt, route, counts = _out_proj(
        ret.reshape(b * lp, -1), att.reshape(b * lp, -1), hp.reshape(b * lp, d),
        w_out[0].astype(BF16), ffn_norm_gain[0], w_router[0], b_router[0], tm=lp // 8, lp=lp, n_pad=n_pad)

    route = route.reshape(b, lp, 128)[:, n_pad:, :]
    top_idx = route[:, :, :TOP_K].astype(jnp.int32)
    gates = route[:, :, TOP_K:2 * TOP_K]
    rank = route[:, :, 2 * TOP_K:3 * TOP_K].astype(jnp.int32)

    tile_expert, tile_valid, tile_q0, n_used, sorted_rows, dest, n_tiles = _routing(
        top_idx, rank, counts.reshape(-1).astype(jnp.int32), lp, n_pad, tm=EXPERT_TILE)
    ys = _experts(tile_expert, tile_valid, tile_q0, n_used, sorted_rows, xt,
                  w_gate_up[0], b_gate_up[0], w_down[0], b_down[0], n_tiles=n_tiles)
    return _combine(dest, h1.reshape(b, lp, d), gates, ys, tt=CHUNK)
```
